```python
import math
import jax, jax.numpy as jnp
from jax import lax
import numpy as np

D_MODEL = 1024
BATCH = 8
SEQ = 8192
DEPTH = 4

CTX_LEN = 256
GRID_W = 64
EPS = 1e-6

ATT_HEADS = 8
ATT_KV_HEADS = 2
HEAD_DIM = 64
ATT_GROUP = ATT_HEADS // ATT_KV_HEADS
ATT_WIDTH = ATT_HEADS * HEAD_DIM
KV_WIDTH = ATT_KV_HEADS * HEAD_DIM
Q_BLOCK = 128
ROPE_THETA = 10000.0

RNN_WIDTH = D_MODEL // 2
RNN_BLOCKS = 8
RNN_BLOCK_DIM = RNN_WIDTH // RNN_BLOCKS
CONV_WIDTH = 4
LRU_C = 8.0

S5_WIDTH = D_MODEL
S5_GROUP = 16
S5_GROUPS = S5_WIDTH // S5_GROUP
S5_STATE = 64
DT_MIN = 1e-3
DT_MAX = 1e-1

EVEN_IN = ATT_WIDTH + 2 * KV_WIDTH + ATT_WIDTH + 2 * RNN_WIDTH
EVEN_SPLITS = (ATT_WIDTH, ATT_WIDTH + KV_WIDTH, ATT_WIDTH + 2 * KV_WIDTH,
               2 * ATT_WIDTH + 2 * KV_WIDTH, 2 * ATT_WIDTH + 2 * KV_WIDTH + RNN_WIDTH)
EVEN_MIX = ATT_WIDTH + RNN_WIDTH
N_EVEN = (DEPTH + 1) // 2
N_ODD = DEPTH // 2

kernel_name = "hybrid_gqa_rglru_s5_ctx_prefix_dit"


def rms_norm(x, gain=None):
    xf = x.astype(jnp.float32)
    y = xf * lax.rsqrt(jnp.mean(xf * xf, axis=-1, keepdims=True) + EPS)
    if gain is not None:
        y = y * gain.astype(jnp.float32)
    return y.astype(x.dtype)


def ada_mod(cond, w, b):
    mod = jnp.dot(jax.nn.silu(cond), w) + b
    return jnp.split(mod, 3, axis=-1)


def axial_rope_tables(n_tokens):
    rows = n_tokens // GRID_W
    row = jnp.repeat(jnp.arange(rows, dtype=jnp.float32), GRID_W)
    col = jnp.tile(jnp.arange(GRID_W, dtype=jnp.float32), rows)
    n_freq = HEAD_DIM // 4
    inv = ROPE_THETA ** (-jnp.arange(n_freq, dtype=jnp.float32) / n_freq)
    ang = jnp.concatenate([row[:, None] * inv, col[:, None] * inv], axis=-1)
    return jnp.cos(ang), jnp.sin(ang)


def apply_rope(x, cos, sin):
    xf = x.astype(jnp.float32).reshape(*x.shape[:-1], HEAD_DIM // 2, 2)
    x0, x1 = xf[..., 0], xf[..., 1]
    cs, sn = cos[None, :, None, :], sin[None, :, None, :]
    out = jnp.stack([x0 * cs - x1 * sn, x0 * sn + x1 * cs], axis=-1)
    return out.reshape(x.shape).astype(x.dtype)


def _attend(q, k, v):
    s = jnp.einsum('bqhgd,bkhd->bhgqk', q, k).astype(jnp.float32) * (HEAD_DIM ** -0.5)
    p = jax.nn.softmax(s, axis=-1).astype(v.dtype)
    return jnp.einsum('bhgqk,bkhd->bqhgd', p, v)


def blocked_attention(q, k, v):
    b, s = q.shape[0], q.shape[1]
    nblk = s // Q_BLOCK
    qb = q.reshape(b, nblk, Q_BLOCK, *q.shape[2:]).swapaxes(0, 1)
    out = lax.map(lambda qi: _attend(qi, k, v), qb)
    return out.swapaxes(0, 1).reshape(b, s, ATT_WIDTH)


def attention_mixer(qc, kc, vc, qx, kx, vx, q_gain, k_gain, cos, sin):
    def heads(t, n):
        return t.reshape(*t.shape[:-1], n, HEAD_DIM)
    b, lc = qc.shape[0], qc.shape[1]
    qc = rms_norm(heads(qc, ATT_HEADS), q_gain)
    kc = rms_norm(heads(kc, ATT_KV_HEADS), k_gain)
    vc = heads(vc, ATT_KV_HEADS)
    qx = apply_rope(rms_norm(heads(qx, ATT_HEADS), q_gain), cos, sin)
    kx = apply_rope(rms_norm(heads(kx, ATT_KV_HEADS), k_gain), cos, sin)
    vx = heads(vx, ATT_KV_HEADS)
    qc5 = qc.reshape(b, lc, ATT_KV_HEADS, ATT_GROUP, HEAD_DIM)
    out_c = _attend(qc5, kc, vc).reshape(b, lc, ATT_WIDTH)
    k_all = jnp.concatenate([kx, kc], axis=1)
    v_all = jnp.concatenate([vx, vc], axis=1)
    qx5 = qx.reshape(b, qx.shape[1], ATT_KV_HEADS, ATT_GROUP, HEAD_DIM)
    out_x = blocked_attention(qx5, k_all, v_all)
    return out_c, out_x


def depthwise_conv(x, w, b):
    ch = x.shape[-1]
    y = lax.conv_general_dilated(x, w[:, None, :].astype(x.dtype), window_strides=(1,),
                                 padding=[(2, 1)], dimension_numbers=('NWC', 'WIO', 'NWC'),
                                 feature_group_count=ch)
    return y + b


def _combine_real(e1, e2):
    a1, b1 = e1
    a2, b2 = e2
    return a1 * a2, a2 * b1 + b2


def linear_scan(a, b, h0, reverse):
    idx = -1 if reverse else 0
    b = b.at[:, idx].add(a[:, idx] * h0)
    _, h = lax.associative_scan(_combine_real, (a, b), reverse=reverse, axis=1)
    return h


def rglru_gates(x, wa, ba, wx, bx, lam):
    xb = x.reshape(*x.shape[:-1], RNN_BLOCKS, RNN_BLOCK_DIM)
    r = jax.nn.sigmoid(jnp.einsum('blhi,hij->blhj', xb, wa) + ba).reshape(x.shape)
    i = jax.nn.sigmoid(jnp.einsum('blhi,hij->blhj', xb, wx) + bx).reshape(x.shape)
    log_a = (-LRU_C * jax.nn.softplus(-lam.astype(jnp.float32))) * r.astype(jnp.float32)
    a = jnp.exp(log_a)
    bterm = jnp.sqrt(-jnp.expm1(2.0 * log_a)) * (i * x).astype(jnp.float32)
    return a, bterm


def rglru_mixer(uc, ux, conv_w, conv_b, wa, ba, wx, bx, lam):
    uc = depthwise_conv(uc, conv_w, conv_b)
    ux = depthwise_conv(ux, conv_w, conv_b)
    yc = jnp.zeros(uc.shape, jnp.float32)
    yx = jnp.zeros(ux.shape, jnp.float32)
    for direction, reverse in enumerate((False, True)):
        ac, bc = rglru_gates(uc, wa[direction], ba[direction], wx[direction], bx[direction], lam[direction])
        hc = linear_scan(ac, bc, jnp.zeros_like(bc[:, 0]), reverse)
        h_last = hc[:, 0] if reverse else hc[:, -1]
        ax, bxt = rglru_gates(ux, wa[direction], ba[direction], wx[direction], bx[direction], lam[direction])
        hx = linear_scan(ax, bxt, h_last, reverse)
        yc = yc + hc
        yx = yx + hx
    return yc.astype(uc.dtype), yx.astype(ux.dtype)


def s5_discretize(lam_re, lam_im, log_step, b_re, b_im):
    dt = jnp.exp(log_step)[:, None]
    mag = jnp.exp(lam_re * dt)
    ab_re = mag * jnp.cos(lam_im * dt)
    ab_im = mag * jnp.sin(lam_im * dt)
    den = lam_re * lam_re + lam_im * lam_im
    nr, ni = ab_re - 1.0, ab_im
    f_re = (nr * lam_re + ni * lam_im) / den
    f_im = (ni * lam_re - nr * lam_im) / den
    bb_re = f_re[..., None] * b_re - f_im[..., None] * b_im
    bb_im = f_re[..., None] * b_im + f_im[..., None] * b_re
    return ab_re, ab_im, bb_re, bb_im


def _combine_complex(e1, e2):
    ar1, ai1, br1, bi1 = e1
    ar2, ai2, br2, bi2 = e2
    return (ar1 * ar2 - ai1 * ai2, ar1 * ai2 + ai1 * ar2,
            ar2 * br1 - ai2 * bi1 + br2, ar2 * bi1 + ai2 * br1 + bi2)


def complex_scan(a_re, a_im, b_re, b_im, h_re, h_im, reverse):
    idx = -1 if reverse else 0
    ar0, ai0 = a_re[:, idx], a_im[:, idx]
    b_re = b_re.at[:, idx].add(ar0 * h_re - ai0 * h_im)
    b_im = b_im.at[:, idx].add(ar0 * h_im + ai0 * h_re)
    _, _, s_re, s_im = lax.associative_scan(_combine_complex, (a_re, a_im, b_re, b_im),
                                            reverse=reverse, axis=1)
    return s_re, s_im


def s5_direction(u, ab_re, ab_im, bb_re, bb_im, c_re, c_im, h_re, h_im, reverse):
    n = u.shape[1]
    bu_re = jnp.einsum('blgj,gpj->blgp', u, bb_re)
    bu_im = jnp.einsum('blgj,gpj->blgp', u, bb_im)
    shape = (1, n) + ab_re.shape
    s_re, s_im = complex_scan(jnp.broadcast_to(ab_re, shape), jnp.broadcast_to(ab_im, shape),
                              bu_re, bu_im, h_re, h_im, reverse)
    y = jnp.einsum('blgp,gjp->blgj', s_re, c_re) - jnp.einsum('blgp,gjp->blgj', s_im, c_im)
    idx = 0 if reverse else -1
    return y, s_re[:, idx], s_im[:, idx]


def s5_mixer(uc, ux, lam_re, lam_im, log_step, b_re, b_im, c_re, c_im, d_skip):
    f32 = jnp.float32

    def grouped(u):
        return u.astype(f32).reshape(*u.shape[:-1], S5_GROUPS, S5_GROUP)

    ucg, uxg = grouped(uc), grouped(ux)
    yc = jnp.zeros_like(ucg)
    yx = jnp.zeros_like(uxg)
    h0 = jnp.zeros((uc.shape[0], S5_GROUPS, S5_STATE), f32)
    for direction, reverse in enumerate((False, True)):
        ab_re, ab_im, bb_re, bb_im = s5_discretize(
            lam_re[direction].astype(f32), lam_im[direction].astype(f32),
            log_step[direction].astype(f32), b_re[direction].astype(f32), b_im[direction].astype(f32))
        cr, ci = c_re[direction].astype(f32), c_im[direction].astype(f32)
        y_c, h_re, h_im = s5_direction(ucg, ab_re, ab_im, bb_re, bb_im, cr, ci, h0, h0, reverse)
        y_x, _, _ = s5_direction(uxg, ab_re, ab_im, bb_re, bb_im, cr, ci, h_re, h_im, reverse)
        yc = yc + y_c
        yx = yx + y_x
    dd = d_skip.astype(f32)
    yc = yc.reshape(uc.shape) + dd * uc.astype(f32)
    yx = yx.reshape(ux.shape) + dd * ux.astype(f32)
    return yc.astype(uc.dtype), yx.astype(ux.dtype)


def even_mixer(hc, hx, w_in, w_out, q_gain, k_gain, conv_w, conv_b, wa, ba, wx, bx, lam, cos, sin):
    qc, kc, vc, gac, uc, grc = jnp.split(hc @ w_in, EVEN_SPLITS, axis=-1)
    qx, kx, vx, gax, ux, grx = jnp.split(hx @ w_in, EVEN_SPLITS, axis=-1)
    att_c, att_x = attention_mixer(qc, kc, vc, qx, kx, vx, q_gain, k_gain, cos, sin)
    rnn_c, rnn_x = rglru_mixer(uc, ux, conv_w, conv_b, wa, ba, wx, bx, lam)
    mix_c = jnp.concatenate([att_c * jax.nn.silu(gac), rnn_c * jax.nn.silu(grc)], axis=-1)
    mix_x = jnp.concatenate([att_x * jax.nn.silu(gax), rnn_x * jax.nn.silu(grx)], axis=-1)
    return mix_c @ w_out, mix_x @ w_out


def odd_mixer(hc, hx, w_in, lam_re, lam_im, log_step, b_re, b_im, c_re, c_im, d_skip, glu_w, glu_b, w_out):
    uc, gc = jnp.split(hc @ w_in, 2, axis=-1)
    ux, gx = jnp.split(hx @ w_in, 2, axis=-1)
    yc, yx = s5_mixer(uc, ux, lam_re, lam_im, log_step, b_re, b_im, c_re, c_im, d_skip)

    def finish(y, g):
        a, bgate = jnp.split(jax.nn.gelu(y) @ glu_w + glu_b, 2, axis=-1)
        return (a * jax.nn.sigmoid(bgate) * jax.nn.silu(g)) @ w_out

    return finish(yc, gc), finish(yx, gx)


def setup_inputs(seed: int = 0) -> dict:
    key = jax.random.key(seed)
    k = jax.random.split(key, 30)
    f32 = jnp.float32
    D = D_MODEL

    def nrm(i, shape, scale):
        return jax.random.normal(k[i], shape, f32) * scale

    a_pow = jax.random.uniform(k[16], (N_EVEN, 2, RNN_WIDTH), f32, minval=0.9, maxval=0.999) ** (1.0 / LRU_C)
    n_idx = jnp.arange(S5_STATE, dtype=f32)
    return {
        "x": nrm(0, (BATCH, SEQ, D), 1.0),
        "c": nrm(1, (BATCH, D), 1.0),
        "ctx": nrm(2, (BATCH, CTX_LEN, D), 1.0),
        "c_ctx": nrm(3, (D,), 1.0),
        "ada_w": nrm(4, (DEPTH, D, 3 * D), 0.5 * D ** -0.5),
        "ada_b": nrm(5, (DEPTH, 3 * D), 0.02),
        "ev_w_in": nrm(6, (N_EVEN, D, EVEN_IN), D ** -0.5),
        "ev_w_out": nrm(7, (N_EVEN, EVEN_MIX, D), EVEN_MIX ** -0.5),
        "q_norm_w": 1.0 + nrm(8, (N_EVEN, HEAD_DIM), 0.02),
        "k_norm_w": 1.0 + nrm(9, (N_EVEN, HEAD_DIM), 0.02),
        "rg_conv_w": nrm(10, (N_EVEN, CONV_WIDTH, RNN_WIDTH), CONV_WIDTH ** -0.5),
        "rg_conv_b": nrm(11, (N_EVEN, RNN_WIDTH), 0.02),
        "rg_wa": nrm(12, (N_EVEN, 2, RNN_BLOCKS, RNN_BLOCK_DIM, RNN_BLOCK_DIM), RNN_BLOCK_DIM ** -0.5),
        "rg_ba": nrm(13, (N_EVEN, 2, RNN_BLOCKS, RNN_BLOCK_DIM), 0.02),
        "rg_wx": nrm(14, (N_EVEN, 2, RNN_BLOCKS, RNN_BLOCK_DIM, RNN_BLOCK_DIM), RNN_BLOCK_DIM ** -0.5),
        "rg_bx": nrm(15, (N_EVEN, 2, RNN_BLOCKS, RNN_BLOCK_DIM), 0.02),
        "rg_lambda": jnp.log(a_pow) - jnp.log1p(-a_pow),
        "od_w_in": nrm(17, (N_ODD, D, 2 * S5_WIDTH), D ** -0.5),
        "s5_lambda_re": -0.5 + nrm(18, (N_ODD, 2, S5_GROUPS, S5_STATE), 0.01),
        "s5_lambda_im": math.pi * n_idx + nrm(19, (N_ODD, 2, S5_GROUPS, S5_STATE), 0.01),
        "s5_log_step": jax.random.uniform(k[20], (N_ODD, 2, S5_GROUPS), f32,
                                          minval=math.log(DT_MIN), maxval=math.log(DT_MAX)),
        "s5_b_re": nrm(21, (N_ODD, 2, S5_GROUPS, S5_STATE, S5_GROUP), (2 * S5_GROUP) ** -0.5),
        "s5_b_im": nrm(22, (N_ODD, 2, S5_GROUPS, S5_STATE, S5_GROUP), (2 * S5_GROUP) ** -0.5),
        "s5_c_re": nrm(23, (N_ODD, 2, S5_GROUPS, S5_GROUP, S5_STATE), S5_STATE ** -0.5),
        "s5_c_im": nrm(24, (N_ODD, 2, S5_GROUPS, S5_GROUP, S5_STATE), S5_STATE ** -0.5),
        "s5_d": nrm(25, (N_ODD, S5_WIDTH), 1.0),
        "glu_w": nrm(26, (N_ODD, S5_WIDTH, 2 * S5_WIDTH), S5_WIDTH ** -0.5),
        "glu_b": nrm(27, (N_ODD, 2 * S5_WIDTH), 0.02),
        "od_w_out": nrm(28, (N_ODD, S5_WIDTH, D), S5_WIDTH ** -0.5),
        "final_norm_w": 1.0 + nrm(29, (D,), 0.02),
    }


def reference(x, c, ctx, c_ctx, ada_w, ada_b, ev_w_in, ev_w_out, q_norm_w, k_norm_w,
              rg_conv_w, rg_conv_b, rg_wa, rg_ba, rg_wx, rg_bx, rg_lambda,
              od_w_in, s5_lambda_re, s5_lambda_im, s5_log_step, s5_b_re, s5_b_im,
              s5_c_re, s5_c_im, s5_d, glu_w, glu_b, od_w_out, final_norm_w):
    cos, sin = axial_rope_tables(x.shape[1])
    xc = ctx
    for layer in range(DEPTH):
        shift_x, scale_x, gate_x = ada_mod(c, ada_w[layer], ada_b[layer])
        shift_c, scale_c, gate_c = ada_mod(c_ctx, ada_w[layer], ada_b[layer])
        hx = rms_norm(x) * (1.0 + scale_x[:, None, :]) + shift_x[:, None, :]
        hc = rms_norm(xc) * (1.0 + scale_c) + shift_c
        j = layer // 2
        if layer % 2 == 0:
            oc, ox = even_mixer(hc, hx, ev_w_in[j], ev_w_out[j], q_norm_w[j], k_norm_w[j],
                                rg_conv_w[j], rg_conv_b[j], rg_wa[j], rg_ba[j], rg_wx[j], rg_bx[j],
                                rg_lambda[j], cos, sin)
        else:
            oc, ox = odd_mixer(hc, hx, od_w_in[j], s5_lambda_re[j], s5_lambda_im[j], s5_log_step[j],
                               s5_b_re[j], s5_b_im[j], s5_c_re[j], s5_c_im[j], s5_d[j],
                               glu_w[j], glu_b[j], od_w_out[j])
        x = x + gate_x[:, None, :] * ox
        if layer < DEPTH - 1:
            xc = xc + gate_c * oc
    return rms_norm(x, final_norm_w)
```

```python
import functools
import math

import jax
import jax.numpy as jnp
from jax import lax
from jax.experimental import pallas as pl
from jax.experimental.pallas import tpu as pltpu

F32 = jnp.float32
BF16 = jnp.bfloat16

D_MODEL = 1024
DEPTH = 4
CTX_LEN = 256
GRID_W = 64
EPS = 1e-6

ATT_HEADS = 8
ATT_KV_HEADS = 2
HEAD_DIM = 64
ATT_GROUP = ATT_HEADS // ATT_KV_HEADS
ATT_WIDTH = ATT_HEADS * HEAD_DIM
KV_WIDTH = ATT_KV_HEADS * HEAD_DIM
ROPE_THETA = 10000.0

RNN_WIDTH = D_MODEL // 2
RNN_BLOCKS = 8
RNN_BLOCK_DIM = RNN_WIDTH // RNN_BLOCKS
LRU_C = 8.0

S5_WIDTH = D_MODEL
S5_GROUP = 16
S5_GROUPS = S5_WIDTH // S5_GROUP
S5_STATE = 64

SUBLANES = 8
LANES = 128
ROW_TILE = 256
ATT_TQ = 128
ATT_TK = 1024
RNN_TT = 64
S5_TT = 128
S5_GB = LANES // S5_GROUP
S5_NB = S5_GROUPS // S5_GB
S5_SW = S5_GB * S5_STATE
VMEM_LIMIT = 56 * 1024 * 1024


def _cparams(sem):
    return pltpu.CompilerParams(dimension_semantics=sem, vmem_limit_bytes=VMEM_LIMIT)


def _sigmoid(x):
    return jax.nn.sigmoid(x)


def _silu(x):
    return x * jax.nn.sigmoid(x)


def _mod_kernel(c_ref, w_ref, b_ref, o_ref):
    c = c_ref[...]
    sc = _silu(c).astype(BF16)
    o_ref[0] = jnp.dot(sc, w_ref[0].astype(BF16), preferred_element_type=F32) + b_ref[0]


def _modulation(cond, ada_w, ada_b):
    depth, d, d3 = ada_w.shape
    nblk = d3 // d
    return pl.pallas_call(
        _mod_kernel,
        grid=(depth, nblk),
        in_specs=[
            pl.BlockSpec((16, d), lambda l, n: (0, 0)),
            pl.BlockSpec((1, d, d), lambda l, n: (l, 0, n)),
            pl.BlockSpec((1, 1, d), lambda l, n: (l, 0, n)),
        ],
        out_specs=pl.BlockSpec((1, 16, d), lambda l, n: (l, 0, n)),
        out_shape=jax.ShapeDtypeStruct((depth, 16, d3), F32),
        compiler_params=_cparams(("parallel", "parallel")),
        name="ada_mod",
    )(cond, ada_w, ada_b.reshape(depth, 1, d3))


def _norm_mod(x, mod_ref):
    ms = jnp.mean(x * x, axis=-1, keepdims=True)
    xn = x * lax.rsqrt(ms + EPS)
    return xn * (1.0 + mod_ref[0, 1:2, :]) + mod_ref[0, 0:1, :]


def _mod_index(t, b):
    return (jnp.where(t == 0, 8, b), 0, 0)


def _group_mean_sq(y, ones_ref):
    sq = y * y
    hi = sq.astype(BF16)
    lo = (sq - hi.astype(F32)).astype(BF16)
    ones = ones_ref[...]
    s = jnp.dot(hi, ones, preferred_element_type=F32) + jnp.dot(lo, ones, preferred_element_type=F32)
    return s * (1.0 / HEAD_DIM)


def _in_even_kernel(x_ref, mod_ref, w_ref, taq_ref, tbq_ref, tak_ref, tbk_ref, oq_ref, ok_ref,
                    q_ref, kt_ref, v_ref, ga_ref, u_ref, gr_ref):
    h = _norm_mod(x_ref[...], mod_ref).astype(BF16)
    y = jnp.dot(h, w_ref[...], preferred_element_type=F32)
    aw, kw = ATT_WIDTH, KV_WIDTH
    qp = y[:, 0:aw]
    qs = y[:, aw:2 * aw]
    rq = lax.rsqrt(_group_mean_sq(qp, oq_ref) + EPS)
    taq = taq_ref[...]
    tbq = tbq_ref[...]
    for j in range(aw // LANES):
        sl = slice(LANES * j, LANES * (j + 1))
        q_ref[:, sl] = (rq[:, sl] * (qp[:, sl] * taq + qs[:, sl] * tbq)).astype(BF16)
    o = 2 * aw
    kp = y[:, o:o + kw]
    ks = y[:, o + kw:o + 2 * kw]
    rk = lax.rsqrt(_group_mean_sq(kp, ok_ref) + EPS)
    kk = rk * (kp * tak_ref[...] + ks * tbk_ref[...])
    kt_ref[...] = kk.T.astype(BF16)
    o += 2 * kw
    v_ref[...] = y[:, o:o + kw].astype(BF16)
    o += kw
    ga_ref[...] = y[:, o:o + aw].astype(BF16)
    o += aw
    u_ref[...] = y[:, o:o + RNN_WIDTH]
    o += RNN_WIDTH
    gr_ref[...] = y[:, o:o + RNN_WIDTH].astype(BF16)


def _in_even(xc, mods, w, taq, tbq, tak, tbk, ones_q, ones_k, nb):
    lt = xc.shape[0]
    nt = lt // ROW_TILE
    d = D_MODEL
    aw, kw, rw = ATT_WIDTH, KV_WIDTH, RNN_WIDTH
    row = lambda t, b: (t, b)
    tab = lambda t, b: (t, 0)
    const = lambda t, b: (0, 0)
    return pl.pallas_call(
        _in_even_kernel,
        grid=(nt, nb),
        in_specs=[
            pl.BlockSpec((ROW_TILE, d), row),
            pl.BlockSpec((1, 3, d), _mod_index),
            pl.BlockSpec(w.shape, const),
            pl.BlockSpec((ROW_TILE, LANES), tab),
            pl.BlockSpec((ROW_TILE, LANES), tab),
            pl.BlockSpec((ROW_TILE, LANES), tab),
            pl.BlockSpec((ROW_TILE, LANES), tab),
            pl.BlockSpec(ones_q.shape, const),
            pl.BlockSpec(ones_k.shape, const),
        ],
        out_specs=[
            pl.BlockSpec((ROW_TILE, aw), row),
            pl.BlockSpec((kw, ROW_TILE), lambda t, b: (b, t)),
            pl.BlockSpec((ROW_TILE, kw), row),
            pl.BlockSpec((ROW_TILE, aw), row),
            pl.BlockSpec((ROW_TILE, rw), row),
            pl.BlockSpec((ROW_TILE, rw), row),
        ],
        out_shape=[
            jax.ShapeDtypeStruct((lt, nb * aw), BF16),
            jax.ShapeDtypeStruct((nb * kw, lt), BF16),
            jax.ShapeDtypeStruct((lt, nb * kw), BF16),
            jax.ShapeDtypeStruct((lt, nb * aw), BF16),
            jax.ShapeDtypeStruct((lt, nb * rw), F32),
            jax.ShapeDtypeStruct((lt, nb * rw), BF16),
        ],
        compiler_params=_cparams(("parallel", "parallel")),
        name="even_in_proj",
    )(xc, mods, w, taq, tbq, tak, tbk, ones_q, ones_k)


def _attn_kernel(q_ref, kt_ref, v_ref, o_ref, *, n_ctx_tiles, n_x_chunks):
    qt = pl.program_id(1)
    lane = lax.broadcasted_iota(jnp.int32, (1, LANES), 1)
    tq = ATT_TQ
    outs = []
    for h in range(ATT_KV_HEADS):
        in_head = (lane // HEAD_DIM) == h
        qs = jnp.concatenate(
            [jnp.where(in_head, q_ref[:, LANES * g:LANES * (g + 1)], jnp.zeros((), BF16))
             for g in range(ATT_GROUP)], axis=0)

        def chunk(start, size, m, l, acc, qs=qs):
            kc = kt_ref[:, pl.ds(start, size)]
            s = jnp.dot(qs, kc, preferred_element_type=F32)
            m_new = jnp.maximum(m, jnp.max(s, axis=-1, keepdims=True))
            alpha = jnp.exp(m - m_new)
            p = jnp.exp(s - m_new)
            l_new = alpha * l + jnp.sum(p, axis=-1, keepdims=True)
            pv = jnp.dot(p.astype(BF16), v_ref[pl.ds(start, size), :], preferred_element_type=F32)
            return m_new, l_new, alpha * acc + pv

        rows = ATT_GROUP * tq
        m0 = jnp.full((rows, 1), -1e30, F32)
        l0 = jnp.zeros((rows, 1), F32)
        a0 = jnp.zeros((rows, LANES), F32)
        m, l, acc = chunk(0, CTX_LEN, m0, l0, a0)

        def body(i, carry):
            start = pl.multiple_of(CTX_LEN + i * ATT_TK, LANES)
            return chunk(start, ATT_TK, *carry)

        n = jnp.where(qt < n_ctx_tiles, 0, n_x_chunks)
        m, l, acc = lax.fori_loop(0, n, body, (m, l, acc))
        outs.append(acc / l)
    first = lane < HEAD_DIM
    for g in range(ATT_GROUP):
        rs = slice(tq * g, tq * (g + 1))
        o_ref[:, LANES * g:LANES * (g + 1)] = jnp.where(first, outs[0][rs], outs[1][rs])


def _attention(q, kt, v, nb):
    lt = q.shape[0]
    nq = lt // ATT_TQ
    kern = functools.partial(_attn_kernel, n_ctx_tiles=CTX_LEN // ATT_TQ,
                             n_x_chunks=(lt - CTX_LEN) // ATT_TK)
    return pl.pallas_call(
        kern,
        grid=(nb, nq),
        in_specs=[
            pl.BlockSpec((ATT_TQ, ATT_WIDTH), lambda b, i: (i, b)),
            pl.BlockSpec((KV_WIDTH, lt), lambda b, i: (b, 0)),
            pl.BlockSpec((lt, KV_WIDTH), lambda b, i: (0, b)),
        ],
        out_specs=pl.BlockSpec((ATT_TQ, ATT_WIDTH), lambda b, i: (i, b)),
        out_shape=jax.ShapeDtypeStruct((lt, nb * ATT_WIDTH), F32),
        compiler_params=_cparams(("parallel", "arbitrary")),
        name="attention",
    )(q, kt, v)


def _rnn_kernel(*refs, reverse, n_ctx_tiles, n_tiles, lt):
    if reverse:
        (u_ref, prev_ref, next_ref, cw_ref, cb_ref, wg_ref, bg_ref, clam_ref, yin_ref,
         y_ref, a_scr, b_scr, h_scr) = refs
    else:
        (u_ref, prev_ref, next_ref, cw_ref, cb_ref, wg_ref, bg_ref, clam_ref,
         y_ref, a_scr, b_scr, h_scr) = refs
        yin_ref = None
    j = pl.program_id(0)
    tile = _scan_tile(j, reverse, n_ctx_tiles, n_tiles)
    rows = RNN_TT * SUBLANES
    rw = RNN_WIDTH

    @pl.when(j == 0)
    def _():
        h_scr[...] = jnp.zeros_like(h_scr)

    t0 = tile * RNN_TT
    t1 = t0 + RNN_TT
    has_prev = jnp.logical_and(t0 != 0, t0 != CTX_LEN)
    has_next = jnp.logical_and(t1 != CTX_LEN, t1 != lt)
    u = u_ref[...]
    prev = jnp.where(has_prev, prev_ref[...], 0.0)
    nxt = jnp.where(has_next, next_ref[0:SUBLANES, :], 0.0)
    ext = jnp.concatenate([prev, u, nxt], axis=0)
    cw = cw_ref[...]
    s = SUBLANES
    cv = (cw[0:1] * ext[0:rows] + cw[1:2] * ext[s:rows + s] + cw[2:3] * ext[2 * s:rows + 2 * s]
          + cw[3:4] * ext[3 * s:rows + 3 * s] + cb_ref[...])
    g = jnp.dot(cv.astype(BF16), wg_ref[...], preferred_element_type=F32) + bg_ref[...]
    r = _sigmoid(g[:, 0:rw])
    i = _sigmoid(g[:, rw:2 * rw])
    log_a = clam_ref[...] * r
    a_scr[...] = jnp.exp(log_a)
    th = jnp.tanh(log_a)
    b_scr[...] = jnp.sqrt(-2.0 * th / (1.0 - th)) * (i * cv)

    def step(k, h):
        t = (RNN_TT - 1 - k) if reverse else k
        rs = pl.ds(pl.multiple_of(t * SUBLANES, SUBLANES), SUBLANES)
        h = a_scr[rs, :] * h + b_scr[rs, :]
        if reverse:
            y_ref[rs, :] = h + yin_ref[rs, :]
        else:
            y_ref[rs, :] = h
        return h

    h_scr[...] = lax.fori_loop(0, RNN_TT, step, h_scr[...], unroll=8)


def _scan_tile(j, reverse, n_ctx_tiles, n_tiles):
    if not reverse:
        return j
    return jnp.where(j < n_ctx_tiles, n_ctx_tiles - 1 - j, n_tiles + n_ctx_tiles - 1 - j)


def _rnn_dir(u, conv_w, conv_b, wg, bg, clam, yin, reverse):
    nrows = u.shape[0]
    rows = RNN_TT * SUBLANES
    n_tiles = nrows // rows
    n_ctx = CTX_LEN // RNN_TT
    lt = nrows // SUBLANES
    halo = 2 * SUBLANES
    per = rows // halo
    nhalo = nrows // halo
    tile_of = lambda j: _scan_tile(j, reverse, n_ctx, n_tiles)
    rw = RNN_WIDTH
    const = lambda j: (0, 0)
    in_specs = [
        pl.BlockSpec((rows, rw), lambda j: (tile_of(j), 0)),
        pl.BlockSpec((halo, rw), lambda j: (jnp.maximum(tile_of(j) * per - 1, 0), 0)),
        pl.BlockSpec((halo, rw), lambda j: (jnp.minimum((tile_of(j) + 1) * per, nhalo - 1), 0)),
        pl.BlockSpec((4, rw), const),
        pl.BlockSpec((1, rw), const),
        pl.BlockSpec((rw, 2 * rw), const),
        pl.BlockSpec((1, 2 * rw), const),
        pl.BlockSpec((1, rw), const),
    ]
    args = [u, u, u, conv_w, conv_b, wg, bg, clam]
    if reverse:
        in_specs.append(pl.BlockSpec((rows, rw), lambda j: (tile_of(j), 0)))
        args.append(yin)
    kern = functools.partial(_rnn_kernel, reverse=reverse, n_ctx_tiles=n_ctx, n_tiles=n_tiles, lt=lt)
    return pl.pallas_call(
        kern,
        grid=(n_tiles,),
        in_specs=in_specs,
        out_specs=pl.BlockSpec((rows, rw), lambda j: (tile_of(j), 0)),
        out_shape=jax.ShapeDtypeStruct((nrows, rw), F32),
        scratch_shapes=[pltpu.VMEM((rows, rw), F32), pltpu.VMEM((rows, rw), F32),
                        pltpu.VMEM((SUBLANES, rw), F32)],
        compiler_params=_cparams(("arbitrary",)),
        name="rglru_bwd" if reverse else "rglru_fwd",
    )(*args)


def _out_even_kernel(att_ref, ga_ref, y_ref, gr_ref, x_ref, mod_ref, wa_ref, wr_ref, o_ref):
    m1 = (att_ref[...] * _silu(ga_ref[...].astype(F32))).astype(BF16)
    m2 = (y_ref[...] * _silu(gr_ref[...].astype(F32))).astype(BF16)
    o = (jnp.dot(m1, wa_ref[...], preferred_element_type=F32)
         + jnp.dot(m2, wr_ref[...], preferred_element_type=F32))
    o_ref[...] = x_ref[...] + mod_ref[0, 2:3, :] * o


def _out_even(att, ga, y, gr, xc, mods, wa, wr, nb):
    lt = xc.shape[0]
    nt = lt // ROW_TILE
    d = D_MODEL
    row = lambda t, b: (t, b)
    const = lambda t, b: (0, 0)
    return pl.pallas_call(
        _out_even_kernel,
        grid=(nt, nb),
        in_specs=[
            pl.BlockSpec((ROW_TILE, ATT_WIDTH), row),
            pl.BlockSpec((ROW_TILE, ATT_WIDTH), row),
            pl.BlockSpec((ROW_TILE, RNN_WIDTH), row),
            pl.BlockSpec((ROW_TILE, RNN_WIDTH), row),
            pl.BlockSpec((ROW_TILE, d), row),
            pl.BlockSpec((1, 3, d), _mod_index),
            pl.BlockSpec(wa.shape, const),
            pl.BlockSpec(wr.shape, const),
        ],
        out_specs=pl.BlockSpec((ROW_TILE, d), row),
        out_shape=jax.ShapeDtypeStruct(xc.shape, F32),
        compiler_params=_cparams(("parallel", "parallel")),
        name="even_out_proj",
    )(att, ga, y, gr, xc, mods, wa, wr)


def _in_odd_kernel(x_ref, mod_ref, w_ref, u_ref, g_ref):
    h = _norm_mod(x_ref[...], mod_ref).astype(BF16)
    y = jnp.dot(h, w_ref[...], preferred_element_type=F32)
    u_ref[...] = y[:, 0:S5_WIDTH]
    g_ref[...] = y[:, S5_WIDTH:2 * S5_WIDTH].astype(BF16)


def _in_odd(xc, mods, w, nb):
    lt = xc.shape[0]
    nt = lt // ROW_TILE
    d = D_MODEL
    row = lambda t, b: (t, b)
    return pl.pallas_call(
        _in_odd_kernel,
        grid=(nt, nb),
        in_specs=[
            pl.BlockSpec((ROW_TILE, d), row),
            pl.BlockSpec((1, 3, d), _mod_index),
            pl.BlockSpec(w.shape, lambda t, b: (0, 0)),
        ],
        out_specs=[pl.BlockSpec((ROW_TILE, S5_WIDTH), row), pl.BlockSpec((ROW_TILE, S5_WIDTH), row)],
        out_shape=[jax.ShapeDtypeStruct((lt, nb * S5_WIDTH), F32),
                   jax.ShapeDtypeStruct((lt, nb * S5_WIDTH), BF16)],
        compiler_params=_cparams(("parallel", "parallel")),
        name="odd_in_proj",
    )(xc, mods, w)


def _s5_kernel(u_ref, bm_ref, cm_ref, a_ref, extra_ref, y_ref, x_scr, h_scr, *, reverse):
    j = pl.program_id(1)

    @pl.when(j == 0)
    def _():
        h_scr[...] = jnp.zeros_like(h_scr)

    u = u_ref[...]
    x_scr[...] = jnp.dot(u.astype(BF16), bm_ref[0], preferred_element_type=F32)
    sw = S5_SW
    ar = a_ref[0, :, 0:sw]
    ai = a_ref[0, :, sw:2 * sw]

    def step(k, carry):
        hr, hi = carry
        t = (S5_TT - 1 - k) if reverse else k
        rs = pl.ds(pl.multiple_of(t * SUBLANES, SUBLANES), SUBLANES)
        nr = ar * hr - ai * hi + x_scr[rs, 0:sw]
        ni = ar * hi + ai * hr + x_scr[rs, sw:2 * sw]
        x_scr[rs, 0:sw] = nr
        x_scr[rs, sw:2 * sw] = ni
        return nr, ni

    hr, hi = lax.fori_loop(0, S5_TT, step, (h_scr[:, 0:sw], h_scr[:, sw:2 * sw]), unroll=4)
    h_scr[:, 0:sw] = hr
    h_scr[:, sw:2 * sw] = hi
    y = jnp.dot(x_scr[...].astype(BF16), cm_ref[0], preferred_element_type=F32)
    if reverse:
        y_ref[...] = y + extra_ref[...]
    else:
        y_ref[...] = y + extra_ref[...] * u


def _s5_dir(u, bm, cm, a, extra, reverse):
    nrows = u.shape[0]
    rows = S5_TT * SUBLANES
    n_tiles = nrows // rows
    n_ctx = CTX_LEN // S5_TT
    tile_of = lambda j: _scan_tile(j, reverse, n_ctx, n_tiles)
    blk = lambda k, j: (tile_of(j), k)
    wblk = lambda k, j: (k, 0, 0)
    if reverse:
        extra_spec = pl.BlockSpec((rows, LANES), blk)
    else:
        extra_spec = pl.BlockSpec((1, LANES), lambda k, j: (0, k))
    return pl.pallas_call(
        functools.partial(_s5_kernel, reverse=reverse),
        grid=(S5_NB, n_tiles),
        in_specs=[
            pl.BlockSpec((rows, LANES), blk),
            pl.BlockSpec((1, LANES, 2 * S5_SW), wblk),
            pl.BlockSpec((1, 2 * S5_SW, LANES), wblk),
            pl.BlockSpec((1, SUBLANES, 2 * S5_SW), wblk),
            extra_spec,
        ],
        out_specs=pl.BlockSpec((rows, LANES), blk),
        out_shape=jax.ShapeDtypeStruct((nrows, S5_WIDTH), F32),
        scratch_shapes=[pltpu.VMEM((rows, 2 * S5_SW), F32), pltpu.VMEM((SUBLANES, 2 * S5_SW), F32)],
        compiler_params=_cparams(("parallel", "arbitrary")),
        name="s5_bwd" if reverse else "s5_fwd",
    )(u, bm, cm, a, extra)


def _out_odd_kernel(y_ref, g_ref, x_ref, mod_ref, gw_ref, gb_ref, wo_ref, fw_ref, o_ref, *, final):
    yy = jax.nn.gelu(y_ref[...]).astype(BF16)
    z = jnp.dot(yy, gw_ref[...], preferred_element_type=F32) + gb_ref[...]
    g = g_ref[...].astype(F32)
    m = z[:, 0:S5_WIDTH] * _sigmoid(z[:, S5_WIDTH:2 * S5_WIDTH]) * _silu(g)
    o = jnp.dot(m.astype(BF16), wo_ref[...], preferred_element_type=F32)
    xn = x_ref[...] + mod_ref[0, 2:3, :] * o
    if final:
        ms = jnp.mean(xn * xn, axis=-1, keepdims=True)
        o_ref[0] = xn * lax.rsqrt(ms + EPS) * fw_ref[...]
    else:
        o_ref[...] = xn


def _out_odd(y, g, xc, mods, gw, gb, wo, fw, nb, final):
    lt = xc.shape[0]
    d = D_MODEL
    nct = CTX_LEN // ROW_TILE
    if final:
        nt = (lt - CTX_LEN) // ROW_TILE
        row = lambda t, b: (t + nct, b)
        mod_index = lambda t, b: (b, 0, 0)
        out_spec = pl.BlockSpec((1, ROW_TILE, d), lambda t, b: (b, t, 0))
        out_shape = jax.ShapeDtypeStruct((nb, lt - CTX_LEN, d), F32)
    else:
        nt = lt // ROW_TILE
        row = lambda t, b: (t, b)
        mod_index = _mod_index
        out_spec = pl.BlockSpec((ROW_TILE, d), row)
        out_shape = jax.ShapeDtypeStruct(xc.shape, F32)
    const = lambda t, b: (0, 0)
    return pl.pallas_call(
        functools.partial(_out_odd_kernel, final=final),
        grid=(nt, nb),
        in_specs=[
            pl.BlockSpec((ROW_TILE, S5_WIDTH), row),
            pl.BlockSpec((ROW_TILE, S5_WIDTH), row),
            pl.BlockSpec((ROW_TILE, d), row),
            pl.BlockSpec((1, 3, d), mod_index),
            pl.BlockSpec(gw.shape, const),
            pl.BlockSpec(gb.shape, const),
            pl.BlockSpec(wo.shape, const),
            pl.BlockSpec(fw.shape, const),
        ],
        out_specs=out_spec,
        out_shape=out_shape,
        compiler_params=_cparams(("parallel", "parallel")),
        name="odd_out_proj_final" if final else "odd_out_proj",
    )(y, g, xc, mods, gw, gb, wo, fw)


def _rope_tables(lt, gain, scale):
    n = lt - CTX_LEN
    rows = n // GRID_W
    row = jnp.repeat(jnp.arange(rows, dtype=F32), GRID_W)
    col = jnp.tile(jnp.arange(GRID_W, dtype=F32), rows)
    n_freq = HEAD_DIM // 4
    inv = ROPE_THETA ** (-jnp.arange(n_freq, dtype=F32) / n_freq)
    ang = jnp.concatenate([row[:, None] * inv, col[:, None] * inv], axis=-1)
    half = HEAD_DIM // 2
    cos = jnp.concatenate([jnp.ones((CTX_LEN, half), F32), jnp.cos(ang)], axis=0)
    sin = jnp.concatenate([jnp.zeros((CTX_LEN, half), F32), jnp.sin(ang)], axis=0)
    g_eo = jnp.concatenate([gain[0::2], gain[1::2]])
    g_oe = jnp.concatenate([gain[1::2], gain[0::2]])
    ta = jnp.concatenate([cos, cos], axis=-1) * g_eo * scale
    tb = jnp.concatenate([-sin, sin], axis=-1) * g_oe * scale
    reps = LANES // HEAD_DIM
    return jnp.tile(ta, (1, reps)), jnp.tile(tb, (1, reps))


def _even_weights(w_in, w_out):
    aw, kw, rw = ATT_WIDTH, KV_WIDTH, RNN_WIDTH
    eo = jnp.concatenate([jnp.arange(0, HEAD_DIM, 2), jnp.arange(1, HEAD_DIM, 2)])
    oe = jnp.concatenate([jnp.arange(1, HEAD_DIM, 2), jnp.arange(0, HEAD_DIM, 2)])
    slot_head = jnp.array([h * ATT_GROUP + g for g in range(ATT_GROUP) for h in range(ATT_KV_HEADS)])
    q_cols = (slot_head[:, None] * HEAD_DIM + eo[None, :]).reshape(-1)
    q_cols_sw = (slot_head[:, None] * HEAD_DIM + oe[None, :]).reshape(-1)
    kv_heads = jnp.arange(ATT_KV_HEADS)
    k_cols = aw + (kv_heads[:, None] * HEAD_DIM + eo[None, :]).reshape(-1)
    k_cols_sw = aw + (kv_heads[:, None] * HEAD_DIM + oe[None, :]).reshape(-1)
    v_cols = aw + kw + jnp.arange(kw)
    slot_cols = (slot_head[:, None] * HEAD_DIM + jnp.arange(HEAD_DIM)[None, :]).reshape(-1)
    ga_cols = aw + 2 * kw + slot_cols
    u_cols = 2 * aw + 2 * kw + jnp.arange(rw)
    gr_cols = 2 * aw + 2 * kw + rw + jnp.arange(rw)
    cols = jnp.concatenate([q_cols, q_cols_sw, k_cols, k_cols_sw, v_cols, ga_cols, u_cols, gr_cols])
    w = w_in[:, cols].astype(BF16)
    wa = w_out[:aw][slot_cols].astype(BF16)
    wr = w_out[aw:].astype(BF16)
    return w, wa, wr


def _block_ones(width):
    idx = jnp.arange(width) // HEAD_DIM
    return (idx[:, None] == idx[None, :]).astype(BF16)


def _rnn_gate_weights(wa, ba, wx, bx):
    eye = jnp.eye(RNN_BLOCKS, dtype=F32)

    def dense(w):
        return jnp.einsum('hij,hk->hikj', w, eye).reshape(RNN_WIDTH, RNN_WIDTH)

    wg = jnp.concatenate([dense(wa), dense(wx)], axis=1).astype(BF16)
    bg = jnp.concatenate([ba.reshape(-1), bx.reshape(-1)])[None, :]
    return wg, bg


def _s5_discretize(lam_re, lam_im, log_step, b_re, b_im):
    dt = jnp.exp(log_step)[:, None]
    mag = jnp.exp(lam_re * dt)
    ab_re = mag * jnp.cos(lam_im * dt)
    ab_im = mag * jnp.sin(lam_im * dt)
    den = lam_re * lam_re + lam_im * lam_im
    nr, ni = ab_re - 1.0, ab_im
    f_re = (nr * lam_re + ni * lam_im) / den
    f_im = (ni * lam_re - nr * lam_im) / den
    bb_re = f_re[..., None] * b_re - f_im[..., None] * b_im
    bb_im = f_re[..., None] * b_im + f_im[..., None] * b_re
    return ab_re, ab_im, bb_re, bb_im


def _s5_weights(lam_re, lam_im, log_step, b_re, b_im, c_re, c_im):
    ab_re, ab_im, bb_re, bb_im = _s5_discretize(lam_re.astype(F32), lam_im.astype(F32), log_step.astype(F32),
                                                b_re.astype(F32), b_im.astype(F32))
    eye = jnp.eye(S5_GB, dtype=F32)
    nbk, gb, p, c = S5_NB, S5_GB, S5_STATE, S5_GROUP
    bb = jnp.stack([bb_re, bb_im], 0).reshape(2, nbk, gb, p, c)
    bm = jnp.einsum('akgpi,gh->kgiahp', bb, eye).reshape(nbk, gb * c, 2 * gb * p).astype(BF16)
    cc = jnp.stack([c_re.astype(F32), -c_im.astype(F32)], 0).reshape(2, nbk, gb, c, p)
    cm = jnp.einsum('akgjp,gh->kagphj', cc, eye).reshape(nbk, 2 * gb * p, gb * c).astype(BF16)
    ab = jnp.stack([ab_re, ab_im], 0).reshape(2, nbk, gb, p).transpose(1, 0, 2, 3).reshape(nbk, 1, 2 * gb * p)
    a = jnp.broadcast_to(ab, (nbk, SUBLANES, 2 * gb * p))
    return bm, cm, a


def kernel(x, c, ctx, c_ctx, ada_w, ada_b, ev_w_in, ev_w_out, q_norm_w, k_norm_w, rg_conv_w, rg_conv_b, rg_wa, rg_ba, rg_wx, rg_bx, rg_lambda, od_w_in, s5_lambda_re, s5_lambda_im, s5_log_step, s5_b_re, s5_b_im, s5_c_re, s5_c_im, s5_d, glu_w, glu_b, od_w_out, final_norm_w):
    nb, seq, d = x.shape
    assert nb == SUBLANES and d == D_MODEL and ctx.shape[1] == CTX_LEN
    assert seq % ATT_TK == 0 and seq % (ROW_TILE) == 0
    lt = CTX_LEN + seq

    xc = jnp.concatenate([ctx, x], axis=1).transpose(1, 0, 2).reshape(lt, nb * d)

    cond = jnp.zeros((16, d), F32).at[:nb].set(c).at[nb].set(c_ctx)
    mods = _modulation(cond, ada_w, ada_b).reshape(DEPTH, 16, 3, d)

    ones_q = _block_ones(ATT_WIDTH)
    ones_k = _block_ones(KV_WIDTH)
    fw = final_norm_w.reshape(1, d)
    out = None
    for layer in range(DEPTH):
        j = layer // 2
        ml = mods[layer]
        if layer % 2 == 0:
            w, wa, wr = _even_weights(ev_w_in[j], ev_w_out[j])
            taq, tbq = _rope_tables(lt, q_norm_w[j], HEAD_DIM ** -0.5)
            tak, tbk = _rope_tables(lt, k_norm_w[j], 1.0)
            q, kt, v, ga, u, gr = _in_even(xc, ml, w, taq, tbq, tak, tbk, ones_q, ones_k, nb)
            att = _attention(q, kt, v, nb)
            ur = u.reshape(lt * nb, RNN_WIDTH)
            conv_b = rg_conv_b[j].reshape(1, RNN_WIDTH)
            y = None
            for direction, reverse in enumerate((False, True)):
                wg, bg = _rnn_gate_weights(rg_wa[j, direction], rg_ba[j, direction],
                                           rg_wx[j, direction], rg_bx[j, direction])
                clam = (-LRU_C * jax.nn.softplus(-rg_lambda[j, direction].astype(F32))).reshape(1, RNN_WIDTH)
                y = _rnn_dir(ur, rg_conv_w[j], conv_b, wg, bg, clam, y, reverse)
            y = y.reshape(lt, nb * RNN_WIDTH)
            xc = _out_even(att, ga, y, gr, xc, ml, wa, wr, nb)
        else:
            u, g = _in_odd(xc, ml, od_w_in[j].astype(BF16), nb)
            ur = u.reshape(lt * nb, S5_WIDTH)
            y = s5_d[j].astype(F32).reshape(1, S5_WIDTH)
            for direction, reverse in enumerate((False, True)):
                bm, cm, a = _s5_weights(s5_lambda_re[j, direction], s5_lambda_im[j, direction],
                                        s5_log_step[j, direction], s5_b_re[j, direction],
                                        s5_b_im[j, direction], s5_c_re[j, direction], s5_c_im[j, direction])
                y = _s5_dir(ur, bm, cm, a, y, reverse)
            y = y.reshape(lt, nb * S5_WIDTH)
            final = layer == DEPTH - 1
            res = _out_odd(y, g, xc, ml, glu_w[j].astype(BF16), glu_b[j].reshape(1, -1),
                           od_w_out[j].astype(BF16), fw, nb, final)
            if final:
                out = res
            else:
                xc = res
    return out
```

```python
import functools
import math

import jax
import jax.numpy as jnp
from jax import lax
from jax.experimental import pallas as pl
from jax.experimental.pallas import tpu as pltpu

F32 = jnp.float32
BF16 = jnp.bfloat16

D_MODEL = 1024
DEPTH = 4
CTX_LEN = 256
GRID_W = 64
EPS = 1e-6

ATT_HEADS = 8
ATT_KV_HEADS = 2
HEAD_DIM = 64
ATT_GROUP = ATT_HEADS // ATT_KV_HEADS
ATT_WIDTH = ATT_HEADS * HEAD_DIM
KV_WIDTH = ATT_KV_HEADS * HEAD_DIM
ROPE_THETA = 10000.0

RNN_WIDTH = D_MODEL // 2
RNN_BLOCKS = 8
RNN_BLOCK_DIM = RNN_WIDTH // RNN_BLOCKS
LRU_C = 8.0

S5_WIDTH = D_MODEL
S5_GROUP = 16
S5_GROUPS = S5_WIDTH // S5_GROUP
S5_STATE = 64

SUBLANES = 8
LANES = 128
ROW_TT = 64
ROWS = ROW_TT * SUBLANES
ATT_TQ = 128
ATT_TK = 1024
RNN_TT = 64
S5_TT = 128
S5_GB = LANES // S5_GROUP
S5_NB = S5_GROUPS // S5_GB
S5_SW = S5_GB * S5_STATE
VMEM_LIMIT = 56 * 1024 * 1024


def _cparams(sem):
    return pltpu.CompilerParams(dimension_semantics=sem, vmem_limit_bytes=VMEM_LIMIT)


def _sigmoid(x):
    return jax.nn.sigmoid(x)


def _silu(x):
    return x * jax.nn.sigmoid(x)


def _batch_rows(b):
    return pl.ds(b, ROW_TT, stride=SUBLANES)


def _mod_kernel(c_ref, w_ref, b_ref, o_ref):
    c = c_ref[...]
    sc = _silu(c).astype(BF16)
    o_ref[0] = jnp.dot(sc, w_ref[0].astype(BF16), preferred_element_type=F32) + b_ref[0]


def _modulation(cond, ada_w, ada_b):
    depth, d, d3 = ada_w.shape
    nblk = d3 // d
    return pl.pallas_call(
        _mod_kernel,
        grid=(depth, nblk),
        in_specs=[
            pl.BlockSpec((16, d), lambda l, n: (0, 0)),
            pl.BlockSpec((1, d, d), lambda l, n: (l, 0, n)),
            pl.BlockSpec((1, 1, d), lambda l, n: (l, 0, n)),
        ],
        out_specs=pl.BlockSpec((1, 16, d), lambda l, n: (l, 0, n)),
        out_shape=jax.ShapeDtypeStruct((depth, 16, d3), F32),
        compiler_params=_cparams(("parallel", "parallel")),
        name="ada_mod",
    )(cond, ada_w, ada_b.reshape(depth, 1, d3))


def _per_batch(x, vec):
    r, d = x.shape
    return (x.reshape(r // SUBLANES, SUBLANES, d) * vec[None]).reshape(r, d)


def _norm_mod(x, mod_ref):
    ms = jnp.mean(x * x, axis=-1, keepdims=True)
    xn = x * lax.rsqrt(ms + EPS)
    r, d = x.shape
    x3 = xn.reshape(r // SUBLANES, SUBLANES, d)
    return (x3 * (1.0 + mod_ref[0, 1])[None] + mod_ref[0, 0][None]).reshape(r, d)


def _mod_spec(n_ctx_tiles, offset=0):
    return pl.BlockSpec((1, 3, SUBLANES, D_MODEL),
                        lambda i: (jnp.where(i + offset < n_ctx_tiles, 0, 1), 0, 0, 0))


def _group_mean_sq(y, ones_ref):
    sq = y * y
    hi = sq.astype(BF16)
    lo = (sq - hi.astype(F32)).astype(BF16)
    ones = ones_ref[...]
    s = jnp.dot(hi, ones, preferred_element_type=F32) + jnp.dot(lo, ones, preferred_element_type=F32)
    return s * (1.0 / HEAD_DIM)


def _in_even_kernel(x_ref, mod_ref, w_ref, cos_ref, sin_ref, gq_ref, gk_ref, oq_ref, ok_ref,
                    q_ref, k_ref, v_ref, ga_ref, u_ref, gr_ref, q_scr, kv_scr):
    h = _norm_mod(x_ref[...], mod_ref).astype(BF16)
    y = jnp.dot(h, w_ref[...], preferred_element_type=F32)
    aw, kw = ATT_WIDTH, KV_WIDTH
    cos = cos_ref[...]
    sin = sin_ref[...]
    qp = y[:, 0:aw]
    qs = y[:, aw:2 * aw]
    rq = lax.rsqrt(_group_mean_sq(qp, oq_ref) + EPS)
    ca = gq_ref[0:1, :] * cos
    sa = gq_ref[1:2, :] * sin
    for j in range(aw // LANES):
        sl = slice(LANES * j, LANES * (j + 1))
        q_scr[j] = rq[:, sl] * (qp[:, sl] * ca + qs[:, sl] * sa)
    o = 2 * aw
    kp = y[:, o:o + kw]
    ks = y[:, o + kw:o + 2 * kw]
    rk = lax.rsqrt(_group_mean_sq(kp, ok_ref) + EPS)
    kv_scr[0] = rk * (kp * (gk_ref[0:1, :] * cos) + ks * (gk_ref[1:2, :] * sin))
    o += 2 * kw
    kv_scr[1] = y[:, o:o + kw]
    o += kw
    for b in range(SUBLANES):
        for j in range(aw // LANES):
            q_ref[b, :, LANES * j:LANES * (j + 1)] = q_scr[j, _batch_rows(b), :].astype(BF16)
        k_ref[b] = kv_scr[0, _batch_rows(b), :].astype(BF16)
        v_ref[b] = kv_scr[1, _batch_rows(b), :].astype(BF16)
    ga_ref[...] = y[:, o:o + aw].astype(BF16)
    o += aw
    u_ref[...] = y[:, o:o + RNN_WIDTH]
    o += RNN_WIDTH
    gr_ref[...] = y[:, o:o + RNN_WIDTH].astype(BF16)


def _in_even(xc, mods, w, cos, sin, gq, gk, ones_q, ones_k):
    nrows = xc.shape[0]
    nt = nrows // ROWS
    lt = nrows // SUBLANES
    d = D_MODEL
    aw, kw, rw = ATT_WIDTH, KV_WIDTH, RNN_WIDTH
    row = lambda i: (i, 0)
    const = lambda i: (0, 0)
    per_batch = lambda i: (0, i, 0)
    return pl.pallas_call(
        _in_even_kernel,
        grid=(nt,),
        in_specs=[
            pl.BlockSpec((ROWS, d), row),
            _mod_spec(CTX_LEN // ROW_TT),
            pl.BlockSpec(w.shape, const),
            pl.BlockSpec((ROWS, LANES), row),
            pl.BlockSpec((ROWS, LANES), row),
            pl.BlockSpec(gq.shape, const),
            pl.BlockSpec(gk.shape, const),
            pl.BlockSpec(ones_q.shape, const),
            pl.BlockSpec(ones_k.shape, const),
        ],
        out_specs=[
            pl.BlockSpec((SUBLANES, ROW_TT, aw), per_batch),
            pl.BlockSpec((SUBLANES, ROW_TT, kw), per_batch),
            pl.BlockSpec((SUBLANES, ROW_TT, kw), per_batch),
            pl.BlockSpec((ROWS, aw), row),
            pl.BlockSpec((ROWS, rw), row),
            pl.BlockSpec((ROWS, rw), row),
        ],
        out_shape=[
            jax.ShapeDtypeStruct((SUBLANES, lt, aw), BF16),
            jax.ShapeDtypeStruct((SUBLANES, lt, kw), BF16),
            jax.ShapeDtypeStruct((SUBLANES, lt, kw), BF16),
            jax.ShapeDtypeStruct((nrows, aw), BF16),
            jax.ShapeDtypeStruct((nrows, rw), F32),
            jax.ShapeDtypeStruct((nrows, rw), BF16),
        ],
        scratch_shapes=[pltpu.VMEM((aw // LANES, ROWS, LANES), F32), pltpu.VMEM((2, ROWS, LANES), F32)],
        compiler_params=_cparams(("parallel",)),
        name="even_in_proj",
    )(xc, mods, w, cos, sin, gq, gk, ones_q, ones_k)


def _aligned(start):
    return start if isinstance(start, int) else pl.multiple_of(start, LANES)


def _attn_kernel(q_ref, k_ref, v_ref, o_ref, kt_scr, vx_scr, sa_scr, sb_scr, m_scr, acc_scr, *,
                 n_ctx_tiles, n_x_chunks):
    qt = pl.program_id(1)
    lane = lax.broadcasted_iota(jnp.int32, (1, LANES), 1)
    tq = ATT_TQ
    rows = ATT_GROUP * tq
    lt = v_ref.shape[1]

    @pl.when(qt == 0)
    def _():
        def fill(i, carry):
            rs = pl.ds(pl.multiple_of(i * CTX_LEN, CTX_LEN), CTX_LEN)
            kt_scr[:, rs] = k_ref[0, rs, :].astype(F32).T.astype(BF16)
            vv = v_ref[0, rs, :]
            for h in range(ATT_KV_HEADS):
                vx_scr[h, rs, :] = jnp.where((lane // HEAD_DIM) == h, vv, jnp.ones((), BF16))
            return carry
        lax.fori_loop(0, lt // CTX_LEN, fill, 0)

    for h in range(ATT_KV_HEADS):
        in_head = (lane // HEAD_DIM) == h
        qs = jnp.concatenate(
            [jnp.where(in_head, q_ref[0, :, LANES * g:LANES * (g + 1)], jnp.zeros((), BF16))
             for g in range(ATT_GROUP)], axis=0)

        def row_max(s):
            return jnp.broadcast_to(jnp.max(s, axis=-1, keepdims=True), (rows, LANES))

        def probs(s_ref, m_rep, width):
            return jnp.concatenate(
                [jnp.exp2(s_ref[:, LANES * j:LANES * (j + 1)] - m_rep) for j in range(width // LANES)],
                axis=1).astype(BF16)

        def scores_into(s_ref, start, qs=qs):
            s = jnp.dot(qs, kt_scr[:, pl.ds(_aligned(start), ATT_TK)], preferred_element_type=F32)
            s_ref[...] = s
            return row_max(s)

        def step(s_ref, next_ref, start, m, acc, mloc, h=h):
            m_new = jnp.maximum(m, mloc)
            alpha = jnp.exp2(m - m_new)
            mloc_next = None
            if next_ref is not None:
                mloc_next = scores_into(next_ref, start + ATT_TK)
            p = probs(s_ref, m_new, ATT_TK)
            pv = jnp.dot(p, vx_scr[h, pl.ds(_aligned(start), ATT_TK), :], preferred_element_type=F32)
            return m_new, alpha * acc + pv, mloc_next

        sa_scr[:, 0:CTX_LEN] = jnp.dot(qs, kt_scr[:, 0:CTX_LEN], preferred_element_type=F32)
        m0 = row_max(sa_scr[:, 0:CTX_LEN])
        p0 = probs(sa_scr, m0, CTX_LEN)
        m_scr[...] = m0
        acc_scr[h] = jnp.dot(p0, vx_scr[h, 0:CTX_LEN, :], preferred_element_type=F32)

        @pl.when(qt >= n_ctx_tiles)
        def _(h=h, step=step, scores_into=scores_into):
            n_pairs = (n_x_chunks - 1) // 2
            mloc = scores_into(sa_scr, CTX_LEN)

            def pair(k, carry):
                m, acc, mloc = carry
                c0 = pl.multiple_of(CTX_LEN + 2 * k * ATT_TK, LANES)
                m, acc, mloc = step(sa_scr, sb_scr, c0, m, acc, mloc)
                m, acc, mloc = step(sb_scr, sa_scr, c0 + ATT_TK, m, acc, mloc)
                return m, acc, mloc

            m, acc, mloc = lax.fori_loop(0, n_pairs, pair, (m_scr[...], acc_scr[h], mloc))
            c = CTX_LEN + 2 * n_pairs * ATT_TK
            if n_x_chunks - 2 * n_pairs == 2:
                m, acc, mloc = step(sa_scr, sb_scr, c, m, acc, mloc)
                m, acc, _ = step(sb_scr, None, c + ATT_TK, m, acc, mloc)
            else:
                m, acc, _ = step(sa_scr, None, c, m, acc, mloc)
            acc_scr[h] = acc

    first = lane < HEAD_DIM
    outs = []
    for h in range(ATT_KV_HEADS):
        acc = acc_scr[h]
        outs.append(acc / pltpu.roll(acc, HEAD_DIM, axis=1))
    for g in range(ATT_GROUP):
        rs = slice(tq * g, tq * (g + 1))
        o_ref[0, :, LANES * g:LANES * (g + 1)] = jnp.where(first, outs[0][rs], outs[1][rs])


def _attention(q, k, v):
    nb, lt, _ = q.shape
    nq = lt // ATT_TQ
    rows = ATT_GROUP * ATT_TQ
    kern = functools.partial(_attn_kernel, n_ctx_tiles=CTX_LEN // ATT_TQ,
                             n_x_chunks=(lt - CTX_LEN) // ATT_TK)
    return pl.pallas_call(
        kern,
        grid=(nb, nq),
        in_specs=[
            pl.BlockSpec((1, ATT_TQ, ATT_WIDTH), lambda b, i: (b, i, 0)),
            pl.BlockSpec((1, lt, KV_WIDTH), lambda b, i: (b, 0, 0)),
            pl.BlockSpec((1, lt, KV_WIDTH), lambda b, i: (b, 0, 0)),
        ],
        out_specs=pl.BlockSpec((1, ATT_TQ, ATT_WIDTH), lambda b, i: (b, i, 0)),
        out_shape=jax.ShapeDtypeStruct((nb, lt, ATT_WIDTH), F32),
        scratch_shapes=[
            pltpu.VMEM((KV_WIDTH, lt), BF16),
            pltpu.VMEM((ATT_KV_HEADS, lt, KV_WIDTH), BF16),
            pltpu.VMEM((rows, ATT_TK), F32),
            pltpu.VMEM((rows, ATT_TK), F32),
            pltpu.VMEM((rows, LANES), F32),
            pltpu.VMEM((ATT_KV_HEADS, rows, LANES), F32),
        ],
        compiler_params=_cparams(("arbitrary", "arbitrary")),
        name="attention",
    )(q, k, v)


def _scan_tile(j, reverse, n_ctx_tiles, n_tiles):
    if not reverse:
        return j
    return jnp.where(j < n_ctx_tiles, n_ctx_tiles - 1 - j, n_tiles + n_ctx_tiles - 1 - j)


def _rnn_kernel(*refs, reverse, n_ctx_tiles, n_tiles, lt):
    if reverse:
        (u_ref, prev_ref, next_ref, cw_ref, cb_ref, wg_ref, bg_ref, clam_ref, yin_ref,
         y_ref, a_scr, b_scr, h_scr) = refs
    else:
        (u_ref, prev_ref, next_ref, cw_ref, cb_ref, wg_ref, bg_ref, clam_ref,
         y_ref, a_scr, b_scr, h_scr) = refs
        yin_ref = None
    j = pl.program_id(0)
    tile = _scan_tile(j, reverse, n_ctx_tiles, n_tiles)
    rows = RNN_TT * SUBLANES
    rw = RNN_WIDTH

    @pl.when(j == 0)
    def _():
        h_scr[...] = jnp.zeros_like(h_scr)

    t0 = tile * RNN_TT
    t1 = t0 + RNN_TT
    has_prev = jnp.logical_and(t0 != 0, t0 != CTX_LEN)
    has_next = jnp.logical_and(t1 != CTX_LEN, t1 != lt)
    u = u_ref[...]
    prev = jnp.where(has_prev, prev_ref[...], 0.0)
    nxt = jnp.where(has_next, next_ref[0:SUBLANES, :], 0.0)
    ext = jnp.concatenate([prev, u, nxt], axis=0)
    cw = cw_ref[...]
    s = SUBLANES
    cv = (cw[0:1] * ext[0:rows] + cw[1:2] * ext[s:rows + s] + cw[2:3] * ext[2 * s:rows + 2 * s]
          + cw[3:4] * ext[3 * s:rows + 3 * s] + cb_ref[...])
    g = jnp.dot(cv.astype(BF16), wg_ref[...], preferred_element_type=F32) + bg_ref[...]
    r = _sigmoid(g[:, 0:rw])
    i = _sigmoid(g[:, rw:2 * rw])
    log_a = clam_ref[...] * r
    a_scr[...] = jnp.exp(log_a)
    th = jnp.tanh(log_a)
    b_scr[...] = jnp.sqrt(-2.0 * th / (1.0 - th)) * (i * cv)

    def step(k, h):
        t = (RNN_TT - 1 - k) if reverse else k
        rs = pl.ds(pl.multiple_of(t * SUBLANES, SUBLANES), SUBLANES)
        h = a_scr[rs, :] * h + b_scr[rs, :]
        if reverse:
            y_ref[rs, :] = h + yin_ref[rs, :]
        else:
            y_ref[rs, :] = h
        return h

    h_scr[...] = lax.fori_loop(0, RNN_TT, step, h_scr[...], unroll=8)


def _rnn_dir(u, conv_w, conv_b, wg, bg, clam, yin, reverse):
    nrows = u.shape[0]
    rows = RNN_TT * SUBLANES
    n_tiles = nrows // rows
    n_ctx = CTX_LEN // RNN_TT
    lt = nrows // SUBLANES
    halo = 2 * SUBLANES
    per = rows // halo
    nhalo = nrows // halo
    tile_of = lambda j: _scan_tile(j, reverse, n_ctx, n_tiles)
    rw = RNN_WIDTH
    const = lambda j: (0, 0)
    in_specs = [
        pl.BlockSpec((rows, rw), lambda j: (tile_of(j), 0)),
        pl.BlockSpec((halo, rw), lambda j: (jnp.maximum(tile_of(j) * per - 1, 0), 0)),
        pl.BlockSpec((halo, rw), lambda j: (jnp.minimum((tile_of(j) + 1) * per, nhalo - 1), 0)),
        pl.BlockSpec((4, rw), const),
        pl.BlockSpec((1, rw), const),
        pl.BlockSpec((rw, 2 * rw), const),
        pl.BlockSpec((1, 2 * rw), const),
        pl.BlockSpec((1, rw), const),
    ]
    args = [u, u, u, conv_w, conv_b, wg, bg, clam]
    if reverse:
        in_specs.append(pl.BlockSpec((rows, rw), lambda j: (tile_of(j), 0)))
        args.append(yin)
    kern = functools.partial(_rnn_kernel, reverse=reverse, n_ctx_tiles=n_ctx, n_tiles=n_tiles, lt=lt)
    return pl.pallas_call(
        kern,
        grid=(n_tiles,),
        in_specs=in_specs,
        out_specs=pl.BlockSpec((rows, rw), lambda j: (tile_of(j), 0)),
        out_shape=jax.ShapeDtypeStruct((nrows, rw), F32),
        scratch_shapes=[pltpu.VMEM((rows, rw), F32), pltpu.VMEM((rows, rw), F32),
                        pltpu.VMEM((SUBLANES, rw), F32)],
        compiler_params=_cparams(("arbitrary",)),
        name="rglru_bwd" if reverse else "rglru_fwd",
    )(*args)


def _out_even_kernel(att_ref, ga_ref, y_ref, gr_ref, x_ref, mod_ref, wa_ref, wr_ref, o_ref, a_scr):
    nblk = ATT_WIDTH // LANES
    for b in range(SUBLANES):
        for j in range(nblk):
            a_scr[j, _batch_rows(b), :] = att_ref[b, :, LANES * j:LANES * (j + 1)]
    att = jnp.concatenate([a_scr[j] for j in range(nblk)], axis=1)
    m1 = (att * _silu(ga_ref[...].astype(F32))).astype(BF16)
    m2 = (y_ref[...] * _silu(gr_ref[...].astype(F32))).astype(BF16)
    o = (jnp.dot(m1, wa_ref[...], preferred_element_type=F32)
         + jnp.dot(m2, wr_ref[...], preferred_element_type=F32))
    o_ref[...] = x_ref[...] + _per_batch(o, mod_ref[0, 2])


def _out_even(att, ga, y, gr, xc, mods, wa, wr):
    nrows = xc.shape[0]
    nt = nrows // ROWS
    d = D_MODEL
    row = lambda i: (i, 0)
    const = lambda i: (0, 0)
    return pl.pallas_call(
        _out_even_kernel,
        grid=(nt,),
        in_specs=[
            pl.BlockSpec((SUBLANES, ROW_TT, ATT_WIDTH), lambda i: (0, i, 0)),
            pl.BlockSpec((ROWS, ATT_WIDTH), row),
            pl.BlockSpec((ROWS, RNN_WIDTH), row),
            pl.BlockSpec((ROWS, RNN_WIDTH), row),
            pl.BlockSpec((ROWS, d), row),
            _mod_spec(CTX_LEN // ROW_TT),
            pl.BlockSpec(wa.shape, const),
            pl.BlockSpec(wr.shape, const),
        ],
        out_specs=pl.BlockSpec((ROWS, d), row),
        out_shape=jax.ShapeDtypeStruct(xc.shape, F32),
        scratch_shapes=[pltpu.VMEM((ATT_WIDTH // LANES, ROWS, LANES), F32)],
        compiler_params=_cparams(("parallel",)),
        name="even_out_proj",
    )(att, ga, y, gr, xc, mods, wa, wr)


def _in_odd_kernel(x_ref, mod_ref, w_ref, u_ref, g_ref):
    h = _norm_mod(x_ref[...], mod_ref).astype(BF16)
    y = jnp.dot(h, w_ref[...], preferred_element_type=F32)
    u_ref[...] = y[:, 0:S5_WIDTH]
    g_ref[...] = y[:, S5_WIDTH:2 * S5_WIDTH].astype(BF16)


def _in_odd(xc, mods, w):
    nrows = xc.shape[0]
    nt = nrows // ROWS
    d = D_MODEL
    row = lambda i: (i, 0)
    return pl.pallas_call(
        _in_odd_kernel,
        grid=(nt,),
        in_specs=[
            pl.BlockSpec((ROWS, d), row),
            _mod_spec(CTX_LEN // ROW_TT),
            pl.BlockSpec(w.shape, lambda i: (0, 0)),
        ],
        out_specs=[pl.BlockSpec((ROWS, S5_WIDTH), row), pl.BlockSpec((ROWS, S5_WIDTH), row)],
        out_shape=[jax.ShapeDtypeStruct((nrows, S5_WIDTH), F32),
                   jax.ShapeDtypeStruct((nrows, S5_WIDTH), BF16)],
        compiler_params=_cparams(("parallel",)),
        name="odd_in_proj",
    )(xc, mods, w)


def _s5_kernel(u_ref, bm_ref, cm_ref, a_ref, extra_ref, y_ref, x_scr, h_scr, *, reverse):
    j = pl.program_id(1)

    @pl.when(j == 0)
    def _():
        h_scr[...] = jnp.zeros_like(h_scr)

    u = u_ref[...]
    x_scr[...] = jnp.dot(u.astype(BF16), bm_ref[0], preferred_element_type=F32)
    sw = S5_SW
    ar = a_ref[0, :, 0:sw]
    ai = a_ref[0, :, sw:2 * sw]

    def step(k, carry):
        hr, hi = carry
        t = (S5_TT - 1 - k) if reverse else k
        rs = pl.ds(pl.multiple_of(t * SUBLANES, SUBLANES), SUBLANES)
        nr = ar * hr - ai * hi + x_scr[rs, 0:sw]
        ni = ar * hi + ai * hr + x_scr[rs, sw:2 * sw]
        x_scr[rs, 0:sw] = nr
        x_scr[rs, sw:2 * sw] = ni
        return nr, ni

    hr, hi = lax.fori_loop(0, S5_TT, step, (h_scr[:, 0:sw], h_scr[:, sw:2 * sw]), unroll=4)
    h_scr[:, 0:sw] = hr
    h_scr[:, sw:2 * sw] = hi
    y = jnp.dot(x_scr[...].astype(BF16), cm_ref[0], preferred_element_type=F32)
    if reverse:
        y_ref[...] = y + extra_ref[...]
    else:
        y_ref[...] = y + extra_ref[...] * u


def _s5_dir(u, bm, cm, a, extra, reverse):
    nrows = u.shape[0]
    rows = S5_TT * SUBLANES
    n_tiles = nrows // rows
    n_ctx = CTX_LEN // S5_TT
    tile_of = lambda j: _scan_tile(j, reverse, n_ctx, n_tiles)
    blk = lambda k, j: (tile_of(j), k)
    wblk = lambda k, j: (k, 0, 0)
    if reverse:
        extra_spec = pl.BlockSpec((rows, LANES), blk)
    else:
        extra_spec = pl.BlockSpec((1, LANES), lambda k, j: (0, k))
    return pl.pallas_call(
        functools.partial(_s5_kernel, reverse=reverse),
        grid=(S5_NB, n_tiles),
        in_specs=[
            pl.BlockSpec((rows, LANES), blk),
            pl.BlockSpec((1, LANES, 2 * S5_SW), wblk),
            pl.BlockSpec((1, 2 * S5_SW, LANES), wblk),
            pl.BlockSpec((1, SUBLANES, 2 * S5_SW), wblk),
            extra_spec,
        ],
        out_specs=pl.BlockSpec((rows, LANES), blk),
        out_shape=jax.ShapeDtypeStruct((nrows, S5_WIDTH), F32),
        scratch_shapes=[pltpu.VMEM((rows, 2 * S5_SW), F32), pltpu.VMEM((SUBLANES, 2 * S5_SW), F32)],
        compiler_params=_cparams(("arbitrary", "arbitrary")),
        name="s5_bwd" if reverse else "s5_fwd",
    )(u, bm, cm, a, extra)


def _out_odd_kernel(y_ref, g_ref, x_ref, mod_ref, gw_ref, gb_ref, wo_ref, fw_ref, o_ref, *scr, final):
    yy = jax.nn.gelu(y_ref[...]).astype(BF16)
    z = jnp.dot(yy, gw_ref[...], preferred_element_type=F32) + gb_ref[...]
    g = g_ref[...].astype(F32)
    m = z[:, 0:S5_WIDTH] * _sigmoid(z[:, S5_WIDTH:2 * S5_WIDTH]) * _silu(g)
    o = jnp.dot(m.astype(BF16), wo_ref[...], preferred_element_type=F32)
    xn = x_ref[...] + _per_batch(o, mod_ref[0, 2])
    if final:
        (o_scr,) = scr
        ms = jnp.mean(xn * xn, axis=-1, keepdims=True)
        xo = xn * lax.rsqrt(ms + EPS) * fw_ref[...]
        nblk = D_MODEL // LANES
        for j in range(nblk):
            o_scr[j] = xo[:, LANES * j:LANES * (j + 1)]
        for b in range(SUBLANES):
            for j in range(nblk):
                o_ref[b, :, LANES * j:LANES * (j + 1)] = o_scr[j, _batch_rows(b), :]
    else:
        o_ref[...] = xn


def _out_odd(y, g, xc, mods, gw, gb, wo, fw, final):
    nrows = xc.shape[0]
    d = D_MODEL
    nct = CTX_LEN // ROW_TT
    lt = nrows // SUBLANES
    if final:
        nt = (lt - CTX_LEN) // ROW_TT
        row = lambda i: (i + nct, 0)
        mod_spec = _mod_spec(nct, offset=nct)
        out_spec = pl.BlockSpec((SUBLANES, ROW_TT, d), lambda i: (0, i, 0))
        out_shape = jax.ShapeDtypeStruct((SUBLANES, lt - CTX_LEN, d), F32)
        scratch = [pltpu.VMEM((d // LANES, ROWS, LANES), F32)]
    else:
        nt = lt // ROW_TT
        row = lambda i: (i, 0)
        mod_spec = _mod_spec(nct)
        out_spec = pl.BlockSpec((ROWS, d), row)
        out_shape = jax.ShapeDtypeStruct(xc.shape, F32)
        scratch = []
    const = lambda i: (0, 0)
    return pl.pallas_call(
        functools.partial(_out_odd_kernel, final=final),
        grid=(nt,),
        in_specs=[
            pl.BlockSpec((ROWS, S5_WIDTH), row),
            pl.BlockSpec((ROWS, S5_WIDTH), row),
            pl.BlockSpec((ROWS, d), row),
            mod_spec,
            pl.BlockSpec(gw.shape, const),
            pl.BlockSpec(gb.shape, const),
            pl.BlockSpec(wo.shape, const),
            pl.BlockSpec(fw.shape, const),
        ],
        out_specs=out_spec,
        out_shape=out_shape,
        scratch_shapes=scratch,
        compiler_params=_cparams(("parallel",)),
        name="odd_out_proj_final" if final else "odd_out_proj",
    )(y, g, xc, mods, gw, gb, wo, fw)


def _rope_tables(lt):
    n = lt - CTX_LEN
    rows = n // GRID_W
    row = jnp.repeat(jnp.arange(rows, dtype=F32), GRID_W)
    col = jnp.tile(jnp.arange(GRID_W, dtype=F32), rows)
    n_freq = HEAD_DIM // 4
    inv = ROPE_THETA ** (-jnp.arange(n_freq, dtype=F32) / n_freq)
    ang = jnp.concatenate([row[:, None] * inv, col[:, None] * inv], axis=-1)
    half = HEAD_DIM // 2
    cos = jnp.concatenate([jnp.ones((CTX_LEN, half), F32), jnp.cos(ang)], axis=0)
    sin = jnp.concatenate([jnp.zeros((CTX_LEN, half), F32), jnp.sin(ang)], axis=0)
    reps = LANES // HEAD_DIM
    ct = jnp.tile(jnp.concatenate([cos, cos], axis=-1), (1, reps))
    st = jnp.tile(jnp.concatenate([-sin, sin], axis=-1), (1, reps))
    return jnp.repeat(ct, SUBLANES, axis=0), jnp.repeat(st, SUBLANES, axis=0)


def _rope_gains(gain, scale):
    g_eo = jnp.concatenate([gain[0::2], gain[1::2]])
    g_oe = jnp.concatenate([gain[1::2], gain[0::2]])
    reps = LANES // HEAD_DIM
    return jnp.stack([jnp.tile(g_eo, reps), jnp.tile(g_oe, reps)]).astype(F32) * scale


def _even_weights(w_in, w_out):
    aw, kw, rw = ATT_WIDTH, KV_WIDTH, RNN_WIDTH
    eo = jnp.concatenate([jnp.arange(0, HEAD_DIM, 2), jnp.arange(1, HEAD_DIM, 2)])
    oe = jnp.concatenate([jnp.arange(1, HEAD_DIM, 2), jnp.arange(0, HEAD_DIM, 2)])
    slot_head = jnp.array([h * ATT_GROUP + g for g in range(ATT_GROUP) for h in range(ATT_KV_HEADS)])
    q_cols = (slot_head[:, None] * HEAD_DIM + eo[None, :]).reshape(-1)
    q_cols_sw = (slot_head[:, None] * HEAD_DIM + oe[None, :]).reshape(-1)
    kv_heads = jnp.arange(ATT_KV_HEADS)
    k_cols = aw + (kv_heads[:, None] * HEAD_DIM + eo[None, :]).reshape(-1)
    k_cols_sw = aw + (kv_heads[:, None] * HEAD_DIM + oe[None, :]).reshape(-1)
    v_cols = aw + kw + jnp.arange(kw)
    slot_cols = (slot_head[:, None] * HEAD_DIM + jnp.arange(HEAD_DIM)[None, :]).reshape(-1)
    ga_cols = aw + 2 * kw + slot_cols
    u_cols = 2 * aw + 2 * kw + jnp.arange(rw)
    gr_cols = 2 * aw + 2 * kw + rw + jnp.arange(rw)
    cols = jnp.concatenate([q_cols, q_cols_sw, k_cols, k_cols_sw, v_cols, ga_cols, u_cols, gr_cols])
    w = w_in[:, cols].astype(BF16)
    wa = w_out[:aw][slot_cols].astype(BF16)
    wr = w_out[aw:].astype(BF16)
    return w, wa, wr


def _block_ones(width):
    idx = jnp.arange(width) // HEAD_DIM
    return (idx[:, None] == idx[None, :]).astype(BF16)


def _rnn_gate_weights(wa, ba, wx, bx):
    eye = jnp.eye(RNN_BLOCKS, dtype=F32)

    def dense(w):
        return jnp.einsum('hij,hk->hikj', w, eye).reshape(RNN_WIDTH, RNN_WIDTH)

    wg = jnp.concatenate([dense(wa), dense(wx)], axis=1).astype(BF16)
    bg = jnp.concatenate([ba.reshape(-1), bx.reshape(-1)])[None, :]
    return wg, bg


def _s5_discretize(lam_re, lam_im, log_step, b_re, b_im):
    dt = jnp.exp(log_step)[:, None]
    mag = jnp.exp(lam_re * dt)
    ab_re = mag * jnp.cos(lam_im * dt)
    ab_im = mag * jnp.sin(lam_im * dt)
    den = lam_re * lam_re + lam_im * lam_im
    nr, ni = ab_re - 1.0, ab_im
    f_re = (nr * lam_re + ni * lam_im) / den
    f_im = (ni * lam_re - nr * lam_im) / den
    bb_re = f_re[..., None] * b_re - f_im[..., None] * b_im
    bb_im = f_re[..., None] * b_im + f_im[..., None] * b_re
    return ab_re, ab_im, bb_re, bb_im


def _s5_weights(lam_re, lam_im, log_step, b_re, b_im, c_re, c_im):
    ab_re, ab_im, bb_re, bb_im = _s5_discretize(lam_re.astype(F32), lam_im.astype(F32), log_step.astype(F32),
                                                b_re.astype(F32), b_im.astype(F32))
    eye = jnp.eye(S5_GB, dtype=F32)
    nbk, gb, p, c = S5_NB, S5_GB, S5_STATE, S5_GROUP
    bb = jnp.stack([bb_re, bb_im], 0).reshape(2, nbk, gb, p, c)
    bm = jnp.einsum('akgpi,gh->kgiahp', bb, eye).reshape(nbk, gb * c, 2 * gb * p).astype(BF16)
    cc = jnp.stack([c_re.astype(F32), -c_im.astype(F32)], 0).reshape(2, nbk, gb, c, p)
    cm = jnp.einsum('akgjp,gh->kagphj', cc, eye).reshape(nbk, 2 * gb * p, gb * c).astype(BF16)
    ab = jnp.stack([ab_re, ab_im], 0).reshape(2, nbk, gb, p).transpose(1, 0, 2, 3).reshape(nbk, 1, 2 * gb * p)
    a = jnp.broadcast_to(ab, (nbk, SUBLANES, 2 * gb * p))
    return bm, cm, a


def kernel(x, c, ctx, c_ctx, ada_w, ada_b, ev_w_in, ev_w_out, q_norm_w, k_norm_w, rg_conv_w, rg_conv_b, rg_wa, rg_ba, rg_wx, rg_bx, rg_lambda, od_w_in, s5_lambda_re, s5_lambda_im, s5_log_step, s5_b_re, s5_b_im, s5_c_re, s5_c_im, s5_d, glu_w, glu_b, od_w_out, final_norm_w):
    nb, seq, d = x.shape
    assert nb == SUBLANES and d == D_MODEL and ctx.shape[1] == CTX_LEN
    assert seq % ATT_TK == 0
    lt = CTX_LEN + seq

    xc = jnp.concatenate([ctx, x], axis=1).transpose(1, 0, 2).reshape(lt * nb, d)

    cond = jnp.zeros((16, d), F32).at[:nb].set(c).at[nb].set(c_ctx)
    mods = _modulation(cond, ada_w, ada_b).reshape(DEPTH, 16, 3, d)
    mods_x = mods[:, :nb].transpose(0, 2, 1, 3)
    mods_c = jnp.broadcast_to(mods[:, nb][:, :, None, :], mods_x.shape)
    mods = jnp.stack([mods_c, mods_x], axis=1)

    cos, sin = _rope_tables(lt)
    ones_q = _block_ones(ATT_WIDTH)
    ones_k = _block_ones(KV_WIDTH)
    fw = final_norm_w.reshape(1, d)
    out = None
    for layer in range(DEPTH):
        j = layer // 2
        ml = mods[layer]
        if layer % 2 == 0:
            w, wa, wr = _even_weights(ev_w_in[j], ev_w_out[j])
            gq = _rope_gains(q_norm_w[j], HEAD_DIM ** -0.5 * math.log2(math.e))
            gk = _rope_gains(k_norm_w[j], 1.0)
            q, k, v, ga, u, gr = _in_even(xc, ml, w, cos, sin, gq, gk, ones_q, ones_k)
            att = _attention(q, k, v)
            conv_b = rg_conv_b[j].reshape(1, RNN_WIDTH)
            y = None
            for direction, reverse in enumerate((False, True)):
                wg, bg = _rnn_gate_weights(rg_wa[j, direction], rg_ba[j, direction],
                                           rg_wx[j, direction], rg_bx[j, direction])
                clam = (-LRU_C * jax.nn.softplus(-rg_lambda[j, direction].astype(F32))).reshape(1, RNN_WIDTH)
                y = _rnn_dir(u, rg_conv_w[j], conv_b, wg, bg, clam, y, reverse)
            xc = _out_even(att, ga, y, gr, xc, ml, wa, wr)
        else:
            u, g = _in_odd(xc, ml, od_w_in[j].astype(BF16))
            y = s5_d[j].astype(F32).reshape(1, S5_WIDTH)
            for direction, reverse in enumerate((False, True)):
                bm, cm, a = _s5_weights(s5_lambda_re[j, direction], s5_lambda_im[j, direction],
                                        s5_log_step[j, direction], s5_b_re[j, direction],
                                        s5_b_im[j, direction], s5_c_re[j, direction], s5_c_im[j, direction])
                y = _s5_dir(u, bm, cm, a, y, reverse)
            final = layer == DEPTH - 1
            res = _out_odd(y, g, xc, ml, glu_w[j].astype(BF16), glu_b[j].reshape(1, -1),
                           od_w_out[j].astype(BF16), fw, final)
            if final:
                out = res
            else:
                xc = res
    return out
```

```python
import functools
import math

import jax
import jax.numpy as jnp
from jax import lax
from jax.experimental import pallas as pl
from jax.experimental.pallas import tpu as pltpu

F32 = jnp.float32
BF16 = jnp.bfloat16

D_MODEL = 1024
DEPTH = 4
CTX_LEN = 256
GRID_W = 64
EPS = 1e-6

ATT_HEADS = 8
ATT_KV_HEADS = 2
HEAD_DIM = 64
ATT_GROUP = ATT_HEADS // ATT_KV_HEADS
ATT_WIDTH = ATT_HEADS * HEAD_DIM
KV_WIDTH = ATT_KV_HEADS * HEAD_DIM
ROPE_THETA = 10000.0

RNN_WIDTH = D_MODEL // 2
RNN_BLOCKS = 8
RNN_BLOCK_DIM = RNN_WIDTH // RNN_BLOCKS
LRU_C = 8.0

S5_WIDTH = D_MODEL
S5_GROUP = 16
S5_GROUPS = S5_WIDTH // S5_GROUP
S5_STATE = 64

SUBLANES = 8
LANES = 128
ROW_TT = 64
ROWS = ROW_TT * SUBLANES
ATT_TQ = 128
ATT_TK = 1024
RNN_TT = 64
S5_TT = 256
S5_TC = 8
S5_GB = LANES // S5_GROUP
S5_NB = S5_GROUPS // S5_GB
S5_SW = S5_GB * S5_STATE
VMEM_LIMIT = 56 * 1024 * 1024


def _cparams(sem):
    return pltpu.CompilerParams(dimension_semantics=sem, vmem_limit_bytes=VMEM_LIMIT)


def _sigmoid(x):
    return jax.nn.sigmoid(x)


def _silu(x):
    return x * jax.nn.sigmoid(x)


def _batch_rows(b):
    return pl.ds(b, ROW_TT, stride=SUBLANES)


def _mod_kernel(c_ref, w_ref, b_ref, o_ref):
    c = c_ref[...]
    sc = _silu(c).astype(BF16)
    o_ref[0] = jnp.dot(sc, w_ref[0].astype(BF16), preferred_element_type=F32) + b_ref[0]


def _modulation(cond, ada_w, ada_b):
    depth, d, d3 = ada_w.shape
    nblk = d3 // d
    return pl.pallas_call(
        _mod_kernel,
        grid=(depth, nblk),
        in_specs=[
            pl.BlockSpec((16, d), lambda l, n: (0, 0)),
            pl.BlockSpec((1, d, d), lambda l, n: (l, 0, n)),
            pl.BlockSpec((1, 1, d), lambda l, n: (l, 0, n)),
        ],
        out_specs=pl.BlockSpec((1, 16, d), lambda l, n: (l, 0, n)),
        out_shape=jax.ShapeDtypeStruct((depth, 16, d3), F32),
        compiler_params=_cparams(("parallel", "parallel")),
        name="ada_mod",
    )(cond, ada_w, ada_b.reshape(depth, 1, d3))


def _per_batch(x, vec):
    r, d = x.shape
    return (x.reshape(r // SUBLANES, SUBLANES, d) * vec[None]).reshape(r, d)


def _norm_mod(x, mod_ref):
    ms = jnp.mean(x * x, axis=-1, keepdims=True)
    xn = x * lax.rsqrt(ms + EPS)
    r, d = x.shape
    x3 = xn.reshape(r // SUBLANES, SUBLANES, d)
    return (x3 * (1.0 + mod_ref[0, 1])[None] + mod_ref[0, 0][None]).reshape(r, d)


def _mod_spec(n_ctx_tiles, offset=0):
    return pl.BlockSpec((1, 3, SUBLANES, D_MODEL),
                        lambda i: (jnp.where(i + offset < n_ctx_tiles, 0, 1), 0, 0, 0))


def _group_mean_sq(y, ones_ref):
    sq = y * y
    hi = sq.astype(BF16)
    lo = (sq - hi.astype(F32)).astype(BF16)
    ones = ones_ref[...]
    s = jnp.dot(hi, ones, preferred_element_type=F32) + jnp.dot(lo, ones, preferred_element_type=F32)
    return s * (1.0 / HEAD_DIM)


def _in_even_kernel(x_ref, mod_ref, w_ref, cos_ref, sin_ref, gq_ref, gk_ref, oq_ref, ok_ref,
                    q_ref, k_ref, v_ref, ga_ref, u_ref, gr_ref, q_scr, kv_scr):
    h = _norm_mod(x_ref[...], mod_ref).astype(BF16)
    y = jnp.dot(h, w_ref[...], preferred_element_type=F32)
    aw, kw = ATT_WIDTH, KV_WIDTH
    cos = cos_ref[...]
    sin = sin_ref[...]
    qp = y[:, 0:aw]
    qs = y[:, aw:2 * aw]
    rq = lax.rsqrt(_group_mean_sq(qp, oq_ref) + EPS)
    ca = gq_ref[0:1, :] * cos
    sa = gq_ref[1:2, :] * sin
    for j in range(aw // LANES):
        sl = slice(LANES * j, LANES * (j + 1))
        q_scr[j] = rq[:, sl] * (qp[:, sl] * ca + qs[:, sl] * sa)
    o = 2 * aw
    kp = y[:, o:o + kw]
    ks = y[:, o + kw:o + 2 * kw]
    rk = lax.rsqrt(_group_mean_sq(kp, ok_ref) + EPS)
    kv_scr[0] = rk * (kp * (gk_ref[0:1, :] * cos) + ks * (gk_ref[1:2, :] * sin))
    o += 2 * kw
    kv_scr[1] = y[:, o:o + kw]
    o += kw
    for b in range(SUBLANES):
        for j in range(aw // LANES):
            q_ref[b, :, LANES * j:LANES * (j + 1)] = q_scr[j, _batch_rows(b), :].astype(BF16)
        k_ref[b] = kv_scr[0, _batch_rows(b), :].astype(BF16)
        v_ref[b] = kv_scr[1, _batch_rows(b), :].astype(BF16)
    ga_ref[...] = y[:, o:o + aw].astype(BF16)
    o += aw
    u_ref[...] = y[:, o:o + RNN_WIDTH]
    o += RNN_WIDTH
    gr_ref[...] = y[:, o:o + RNN_WIDTH].astype(BF16)


def _in_even(xc, mods, w, cos, sin, gq, gk, ones_q, ones_k):
    nrows = xc.shape[0]
    nt = nrows // ROWS
    lt = nrows // SUBLANES
    d = D_MODEL
    aw, kw, rw = ATT_WIDTH, KV_WIDTH, RNN_WIDTH
    row = lambda i: (i, 0)
    const = lambda i: (0, 0)
    per_batch = lambda i: (0, i, 0)
    return pl.pallas_call(
        _in_even_kernel,
        grid=(nt,),
        in_specs=[
            pl.BlockSpec((ROWS, d), row),
            _mod_spec(CTX_LEN // ROW_TT),
            pl.BlockSpec(w.shape, const),
            pl.BlockSpec((ROWS, LANES), row),
            pl.BlockSpec((ROWS, LANES), row),
            pl.BlockSpec(gq.shape, const),
            pl.BlockSpec(gk.shape, const),
            pl.BlockSpec(ones_q.shape, const),
            pl.BlockSpec(ones_k.shape, const),
        ],
        out_specs=[
            pl.BlockSpec((SUBLANES, ROW_TT, aw), per_batch),
            pl.BlockSpec((SUBLANES, ROW_TT, kw), per_batch),
            pl.BlockSpec((SUBLANES, ROW_TT, kw), per_batch),
            pl.BlockSpec((ROWS, aw), row),
            pl.BlockSpec((ROWS, rw), row),
            pl.BlockSpec((ROWS, rw), row),
        ],
        out_shape=[
            jax.ShapeDtypeStruct((SUBLANES, lt, aw), BF16),
            jax.ShapeDtypeStruct((SUBLANES, lt, kw), BF16),
            jax.ShapeDtypeStruct((SUBLANES, lt, kw), BF16),
            jax.ShapeDtypeStruct((nrows, aw), BF16),
            jax.ShapeDtypeStruct((nrows, rw), F32),
            jax.ShapeDtypeStruct((nrows, rw), BF16),
        ],
        scratch_shapes=[pltpu.VMEM((aw // LANES, ROWS, LANES), F32), pltpu.VMEM((2, ROWS, LANES), F32)],
        compiler_params=_cparams(("parallel",)),
        name="even_in_proj",
    )(xc, mods, w, cos, sin, gq, gk, ones_q, ones_k)


def _aligned(start):
    return start if isinstance(start, int) else pl.multiple_of(start, LANES)


def _attn_kernel(q_ref, k_ref, v_ref, o_ref, kt_scr, vx_scr, s_scr, p_scr, acc_scr, *,
                 n_ctx_tiles, n_x_chunks):
    qt = pl.program_id(1)
    lane = lax.broadcasted_iota(jnp.int32, (1, LANES), 1)
    tq = ATT_TQ
    rows = ATT_GROUP * tq
    lt = v_ref.shape[1]
    n = n_x_chunks

    @pl.when(qt == 0)
    def _():
        def fill(i, carry):
            rs = pl.ds(pl.multiple_of(i * CTX_LEN, CTX_LEN), CTX_LEN)
            kt_scr[:, rs] = k_ref[0, rs, :].astype(F32).T.astype(BF16)
            vv = v_ref[0, rs, :]
            for h in range(ATT_KV_HEADS):
                vx_scr[h, rs, :] = jnp.where((lane // HEAD_DIM) == h, vv, jnp.ones((), BF16))
            return carry
        lax.fori_loop(0, lt // CTX_LEN, fill, 0)

    def key_range(c):
        return (CTX_LEN + c * ATT_TK, ATT_TK) if c < n else (0, CTX_LEN)

    for h in range(ATT_KV_HEADS):
        in_head = (lane // HEAD_DIM) == h
        qs = jnp.concatenate(
            [jnp.where(in_head, q_ref[0, :, LANES * g:LANES * (g + 1)], jnp.zeros((), BF16))
             for g in range(ATT_GROUP)], axis=0)

        def scores(start, width, slot, qs=qs):
            s = jnp.dot(qs, kt_scr[:, pl.ds(_aligned(start), width)], preferred_element_type=F32)
            s_scr[slot, :, 0:width] = s
            return jnp.broadcast_to(jnp.max(s, axis=-1, keepdims=True), (rows, LANES))

        def softmax(width, slot, m, mloc):
            m_new = jnp.maximum(m, mloc)
            for j in range(width // LANES):
                sl = slice(LANES * j, LANES * (j + 1))
                p_scr[slot, :, sl] = jnp.exp2(s_scr[slot, :, sl] - m_new).astype(BF16)
            return m_new, jnp.exp2(m - m_new)

        def weighted_values(start, width, slot, alpha, acc, h=h):
            pv = jnp.dot(p_scr[slot, :, 0:width], vx_scr[h, pl.ds(_aligned(start), width), :],
                         preferred_element_type=F32)
            return alpha * acc + pv

        @pl.when(qt < n_ctx_tiles)
        def _(h=h, scores=scores, softmax=softmax, weighted_values=weighted_values):
            mloc = scores(0, CTX_LEN, 0)
            softmax(CTX_LEN, 0, mloc, mloc)
            acc_scr[h] = weighted_values(0, CTX_LEN, 0, jnp.zeros((rows, LANES), F32),
                                         jnp.zeros((rows, LANES), F32))

        @pl.when(qt >= n_ctx_tiles)
        def _(h=h, scores=scores, softmax=softmax, weighted_values=weighted_values):
            def step(par, pv_rng, sm_width, qk_rng, carry):
                m, acc, alpha, mloc = carry
                mloc_new = scores(qk_rng[0], qk_rng[1], par) if qk_rng is not None else mloc
                acc = weighted_values(pv_rng[0], pv_rng[1], par, alpha, acc)
                if sm_width is not None:
                    m, alpha = softmax(sm_width, 1 - par, m, mloc)
                return m, acc, alpha, mloc_new

            mloc = scores(*key_range(0), 0)
            m, _ = softmax(key_range(0)[1], 0, mloc, mloc)
            mloc = scores(*key_range(1), 1)
            carry = (m, jnp.zeros((rows, LANES), F32), jnp.ones((rows, LANES), F32), mloc)

            for i in range(n + 1):
                carry = step(i % 2, key_range(i),
                             key_range(i + 1)[1] if i + 1 <= n else None,
                             key_range(i + 2) if i + 2 <= n else None, carry)
            acc_scr[h] = carry[1]

    first = lane < HEAD_DIM
    outs = []
    for h in range(ATT_KV_HEADS):
        acc = acc_scr[h]
        outs.append(acc / pltpu.roll(acc, HEAD_DIM, axis=1))
    for g in range(ATT_GROUP):
        rs = slice(tq * g, tq * (g + 1))
        o_ref[0, :, LANES * g:LANES * (g + 1)] = jnp.where(first, outs[0][rs], outs[1][rs])


def _attention(q, k, v):
    nb, lt, _ = q.shape
    nq = lt // ATT_TQ
    rows = ATT_GROUP * ATT_TQ
    kern = functools.partial(_attn_kernel, n_ctx_tiles=CTX_LEN // ATT_TQ,
                             n_x_chunks=(lt - CTX_LEN) // ATT_TK)
    return pl.pallas_call(
        kern,
        grid=(nb, nq),
        in_specs=[
            pl.BlockSpec((1, ATT_TQ, ATT_WIDTH), lambda b, i: (b, i, 0)),
            pl.BlockSpec((1, lt, KV_WIDTH), lambda b, i: (b, 0, 0)),
            pl.BlockSpec((1, lt, KV_WIDTH), lambda b, i: (b, 0, 0)),
        ],
        out_specs=pl.BlockSpec((1, ATT_TQ, ATT_WIDTH), lambda b, i: (b, i, 0)),
        out_shape=jax.ShapeDtypeStruct((nb, lt, ATT_WIDTH), F32),
        scratch_shapes=[
            pltpu.VMEM((KV_WIDTH, lt), BF16),
            pltpu.VMEM((ATT_KV_HEADS, lt, KV_WIDTH), BF16),
            pltpu.VMEM((2, rows, ATT_TK), F32),
            pltpu.VMEM((2, rows, ATT_TK), BF16),
            pltpu.VMEM((ATT_KV_HEADS, rows, LANES), F32),
        ],
        compiler_params=_cparams(("arbitrary", "arbitrary")),
        name="attention",
    )(q, k, v)


def _scan_tile(j, reverse, n_ctx_tiles, n_tiles):
    if not reverse:
        return j
    return jnp.where(j < n_ctx_tiles, n_ctx_tiles - 1 - j, n_tiles + n_ctx_tiles - 1 - j)


def _rnn_kernel(*refs, reverse, n_ctx_tiles, n_tiles, lt):
    if reverse:
        (u_ref, prev_ref, next_ref, cw_ref, cb_ref, wg_ref, bg_ref, clam_ref, yin_ref,
         y_ref, a_scr, b_scr, h_scr) = refs
    else:
        (u_ref, prev_ref, next_ref, cw_ref, cb_ref, wg_ref, bg_ref, clam_ref,
         y_ref, a_scr, b_scr, h_scr) = refs
        yin_ref = None
    j = pl.program_id(0)
    tile = _scan_tile(j, reverse, n_ctx_tiles, n_tiles)
    rows = RNN_TT * SUBLANES
    rw = RNN_WIDTH

    @pl.when(j == 0)
    def _():
        h_scr[...] = jnp.zeros_like(h_scr)

    t0 = tile * RNN_TT
    t1 = t0 + RNN_TT
    has_prev = jnp.logical_and(t0 != 0, t0 != CTX_LEN)
    has_next = jnp.logical_and(t1 != CTX_LEN, t1 != lt)
    u = u_ref[...]
    prev = jnp.where(has_prev, prev_ref[...], 0.0)
    nxt = jnp.where(has_next, next_ref[0:SUBLANES, :], 0.0)
    ext = jnp.concatenate([prev, u, nxt], axis=0)
    cw = cw_ref[...]
    s = SUBLANES
    cv = (cw[0:1] * ext[0:rows] + cw[1:2] * ext[s:rows + s] + cw[2:3] * ext[2 * s:rows + 2 * s]
          + cw[3:4] * ext[3 * s:rows + 3 * s] + cb_ref[...])
    g = jnp.dot(cv.astype(BF16), wg_ref[...], preferred_element_type=F32) + bg_ref[...]
    r = _sigmoid(g[:, 0:rw])
    i = _sigmoid(g[:, rw:2 * rw])
    log_a = clam_ref[...] * r
    a_scr[...] = jnp.exp(log_a)
    th = jnp.tanh(log_a)
    b_scr[...] = jnp.sqrt(-2.0 * th / (1.0 - th)) * (i * cv)

    def step(k, h):
        t = (RNN_TT - 1 - k) if reverse else k
        rs = pl.ds(pl.multiple_of(t * SUBLANES, SUBLANES), SUBLANES)
        h = a_scr[rs, :] * h + b_scr[rs, :]
        if reverse:
            y_ref[rs, :] = h + yin_ref[rs, :]
        else:
            y_ref[rs, :] = h
        return h

    h_scr[...] = lax.fori_loop(0, RNN_TT, step, h_scr[...], unroll=8)


def _rnn_dir(u, conv_w, conv_b, wg, bg, clam, yin, reverse):
    nrows = u.shape[0]
    rows = RNN_TT * SUBLANES
    n_tiles = nrows // rows
    n_ctx = CTX_LEN // RNN_TT
    lt = nrows // SUBLANES
    halo = 2 * SUBLANES
    per = rows // halo
    nhalo = nrows // halo
    tile_of = lambda j: _scan_tile(j, reverse, n_ctx, n_tiles)
    rw = RNN_WIDTH
    const = lambda j: (0, 0)
    in_specs = [
        pl.BlockSpec((rows, rw), lambda j: (tile_of(j), 0)),
        pl.BlockSpec((halo, rw), lambda j: (jnp.maximum(tile_of(j) * per - 1, 0), 0)),
        pl.BlockSpec((halo, rw), lambda j: (jnp.minimum((tile_of(j) + 1) * per, nhalo - 1), 0)),
        pl.BlockSpec((4, rw), const),
        pl.BlockSpec((1, rw), const),
        pl.BlockSpec((rw, 2 * rw), const),
        pl.BlockSpec((1, 2 * rw), const),
        pl.BlockSpec((1, rw), const),
    ]
    args = [u, u, u, conv_w, conv_b, wg, bg, clam]
    if reverse:
        in_specs.append(pl.BlockSpec((rows, rw), lambda j: (tile_of(j), 0)))
        args.append(yin)
    kern = functools.partial(_rnn_kernel, reverse=reverse, n_ctx_tiles=n_ctx, n_tiles=n_tiles, lt=lt)
    return pl.pallas_call(
        kern,
        grid=(n_tiles,),
        in_specs=in_specs,
        out_specs=pl.BlockSpec((rows, rw), lambda j: (tile_of(j), 0)),
        out_shape=jax.ShapeDtypeStruct((nrows, rw), F32),
        scratch_shapes=[pltpu.VMEM((rows, rw), F32), pltpu.VMEM((rows, rw), F32),
                        pltpu.VMEM((SUBLANES, rw), F32)],
        compiler_params=_cparams(("arbitrary",)),
        name="rglru_bwd" if reverse else "rglru_fwd",
    )(*args)


def _out_even_kernel(att_ref, ga_ref, y_ref, gr_ref, x_ref, mod_ref, wa_ref, wr_ref, o_ref, a_scr):
    nblk = ATT_WIDTH // LANES
    for b in range(SUBLANES):
        for j in range(nblk):
            a_scr[j, _batch_rows(b), :] = att_ref[b, :, LANES * j:LANES * (j + 1)]
    att = jnp.concatenate([a_scr[j] for j in range(nblk)], axis=1)
    m1 = (att * _silu(ga_ref[...].astype(F32))).astype(BF16)
    m2 = (y_ref[...] * _silu(gr_ref[...].astype(F32))).astype(BF16)
    o = (jnp.dot(m1, wa_ref[...], preferred_element_type=F32)
         + jnp.dot(m2, wr_ref[...], preferred_element_type=F32))
    o_ref[...] = x_ref[...] + _per_batch(o, mod_ref[0, 2])


def _out_even(att, ga, y, gr, xc, mods, wa, wr):
    nrows = xc.shape[0]
    nt = nrows // ROWS
    d = D_MODEL
    row = lambda i: (i, 0)
    const = lambda i: (0, 0)
    return pl.pallas_call(
        _out_even_kernel,
        grid=(nt,),
        in_specs=[
            pl.BlockSpec((SUBLANES, ROW_TT, ATT_WIDTH), lambda i: (0, i, 0)),
            pl.BlockSpec((ROWS, ATT_WIDTH), row),
            pl.BlockSpec((ROWS, RNN_WIDTH), row),
            pl.BlockSpec((ROWS, RNN_WIDTH), row),
            pl.BlockSpec((ROWS, d), row),
            _mod_spec(CTX_LEN // ROW_TT),
            pl.BlockSpec(wa.shape, const),
            pl.BlockSpec(wr.shape, const),
        ],
        out_specs=pl.BlockSpec((ROWS, d), row),
        out_shape=jax.ShapeDtypeStruct(xc.shape, F32),
        scratch_shapes=[pltpu.VMEM((ATT_WIDTH // LANES, ROWS, LANES), F32)],
        compiler_params=_cparams(("parallel",)),
        name="even_out_proj",
    )(att, ga, y, gr, xc, mods, wa, wr)


def _in_odd_kernel(x_ref, mod_ref, w_ref, u_ref, g_ref):
    h = _norm_mod(x_ref[...], mod_ref).astype(BF16)
    y = jnp.dot(h, w_ref[...], preferred_element_type=F32)
    u_ref[...] = y[:, 0:S5_WIDTH]
    g_ref[...] = y[:, S5_WIDTH:2 * S5_WIDTH].astype(BF16)


def _in_odd(xc, mods, w):
    nrows = xc.shape[0]
    nt = nrows // ROWS
    d = D_MODEL
    row = lambda i: (i, 0)
    return pl.pallas_call(
        _in_odd_kernel,
        grid=(nt,),
        in_specs=[
            pl.BlockSpec((ROWS, d), row),
            _mod_spec(CTX_LEN // ROW_TT),
            pl.BlockSpec(w.shape, lambda i: (0, 0)),
        ],
        out_specs=[pl.BlockSpec((ROWS, S5_WIDTH), row), pl.BlockSpec((ROWS, S5_WIDTH), row)],
        out_shape=[jax.ShapeDtypeStruct((nrows, S5_WIDTH), F32),
                   jax.ShapeDtypeStruct((nrows, S5_WIDTH), BF16)],
        compiler_params=_cparams(("parallel",)),
        name="odd_in_proj",
    )(xc, mods, w)


def _s5_kernel(u_ref, ms_ref, mio_ref, a_ref, extra_ref, y_ref, hp_scr, h_scr, *, reverse):
    j = pl.program_id(1)
    nc = S5_TT // S5_TC
    rows = nc * SUBLANES
    sw = S5_SW

    @pl.when(j == 0)
    def _():
        h_scr[...] = jnp.zeros_like(h_scr)

    def chunk_major(ref):
        return jnp.concatenate(
            [ref[:, SUBLANES * t:SUBLANES * (t + 1), :].reshape(rows, LANES) for t in range(S5_TC)], axis=1)

    lhs32 = chunk_major(u_ref)
    lhs = lhs32.astype(BF16)
    local = jnp.dot(lhs, ms_ref[0], preferred_element_type=F32)
    ar = a_ref[0, :, 0:sw]
    ai = a_ref[0, :, sw:2 * sw]
    hr = h_scr[:, 0:sw]
    hi = h_scr[:, sw:2 * sw]
    for c in (range(nc - 1, -1, -1) if reverse else range(nc)):
        rs = slice(SUBLANES * c, SUBLANES * (c + 1))
        hp_scr[rs, 0:sw] = hr
        hp_scr[rs, sw:2 * sw] = hi
        hr, hi = (ar * hr - ai * hi + local[rs, 0:sw], ar * hi + ai * hr + local[rs, sw:2 * sw])
    h_scr[:, 0:sw] = hr
    h_scr[:, sw:2 * sw] = hi
    both = jnp.concatenate([lhs, hp_scr[...].astype(BF16)], axis=1)
    y = jnp.dot(both, mio_ref[0], preferred_element_type=F32)
    if reverse:
        y = y + chunk_major(extra_ref)
    else:
        y = y + extra_ref[...] * lhs32
    for t in range(S5_TC):
        y_ref[:, SUBLANES * t:SUBLANES * (t + 1), :] = (
            y[:, LANES * t:LANES * (t + 1)].reshape(nc, SUBLANES, LANES))


def _s5_dir(u3, ms, mio, a, extra, reverse):
    nchunks, crow, width = u3.shape
    nc = S5_TT // S5_TC
    n_tiles = nchunks // nc
    n_ctx = CTX_LEN // S5_TT
    tile_of = lambda j: _scan_tile(j, reverse, n_ctx, n_tiles)
    blk = lambda k, j: (tile_of(j), 0, k)
    wblk = lambda k, j: (k, 0, 0)
    cw = S5_TC * LANES
    if reverse:
        extra_spec = pl.BlockSpec((nc, crow, LANES), blk)
    else:
        extra_spec = pl.BlockSpec((1, cw), lambda k, j: (0, k))
    return pl.pallas_call(
        functools.partial(_s5_kernel, reverse=reverse),
        grid=(S5_NB, n_tiles),
        in_specs=[
            pl.BlockSpec((nc, crow, LANES), blk),
            pl.BlockSpec((1, cw, 2 * S5_SW), wblk),
            pl.BlockSpec((1, cw + 2 * S5_SW, cw), wblk),
            pl.BlockSpec((1, SUBLANES, 2 * S5_SW), wblk),
            extra_spec,
        ],
        out_specs=pl.BlockSpec((nc, crow, LANES), blk),
        out_shape=jax.ShapeDtypeStruct(u3.shape, F32),
        scratch_shapes=[pltpu.VMEM((nc * SUBLANES, 2 * S5_SW), F32), pltpu.VMEM((SUBLANES, 2 * S5_SW), F32)],
        compiler_params=_cparams(("arbitrary", "arbitrary")),
        name="s5_bwd" if reverse else "s5_fwd",
    )(u3, ms, mio, a, extra)


def _out_odd_kernel(y_ref, g_ref, x_ref, mod_ref, gw_ref, gb_ref, wo_ref, fw_ref, o_ref, *scr, final):
    yy = jax.nn.gelu(y_ref[...]).astype(BF16)
    z = jnp.dot(yy, gw_ref[...], preferred_element_type=F32) + gb_ref[...]
    g = g_ref[...].astype(F32)
    m = z[:, 0:S5_WIDTH] * _sigmoid(z[:, S5_WIDTH:2 * S5_WIDTH]) * _silu(g)
    o = jnp.dot(m.astype(BF16), wo_ref[...], preferred_element_type=F32)
    xn = x_ref[...] + _per_batch(o, mod_ref[0, 2])
    if final:
        (o_scr,) = scr
        ms = jnp.mean(xn * xn, axis=-1, keepdims=True)
        xo = xn * lax.rsqrt(ms + EPS) * fw_ref[...]
        nblk = D_MODEL // LANES
        for j in range(nblk):
            o_scr[j] = xo[:, LANES * j:LANES * (j + 1)]
        for b in range(SUBLANES):
            for j in range(nblk):
                o_ref[b, :, LANES * j:LANES * (j + 1)] = o_scr[j, _batch_rows(b), :]
    else:
        o_ref[...] = xn


def _out_odd(y, g, xc, mods, gw, gb, wo, fw, final):
    nrows = xc.shape[0]
    d = D_MODEL
    nct = CTX_LEN // ROW_TT
    lt = nrows // SUBLANES
    if final:
        nt = (lt - CTX_LEN) // ROW_TT
        row = lambda i: (i + nct, 0)
        mod_spec = _mod_spec(nct, offset=nct)
        out_spec = pl.BlockSpec((SUBLANES, ROW_TT, d), lambda i: (0, i, 0))
        out_shape = jax.ShapeDtypeStruct((SUBLANES, lt - CTX_LEN, d), F32)
        scratch = [pltpu.VMEM((d // LANES, ROWS, LANES), F32)]
    else:
        nt = lt // ROW_TT
        row = lambda i: (i, 0)
        mod_spec = _mod_spec(nct)
        out_spec = pl.BlockSpec((ROWS, d), row)
        out_shape = jax.ShapeDtypeStruct(xc.shape, F32)
        scratch = []
    const = lambda i: (0, 0)
    return pl.pallas_call(
        functools.partial(_out_odd_kernel, final=final),
        grid=(nt,),
        in_specs=[
            pl.BlockSpec((ROWS, S5_WIDTH), row),
            pl.BlockSpec((ROWS, S5_WIDTH), row),
            pl.BlockSpec((ROWS, d), row),
            mod_spec,
            pl.BlockSpec(gw.shape, const),
            pl.BlockSpec(gb.shape, const),
            pl.BlockSpec(wo.shape, const),
            pl.BlockSpec(fw.shape, const),
        ],
        out_specs=out_spec,
        out_shape=out_shape,
        scratch_shapes=scratch,
        compiler_params=_cparams(("parallel",)),
        name="odd_out_proj_final" if final else "odd_out_proj",
    )(y, g, xc, mods, gw, gb, wo, fw)


def _rope_tables(lt):
    n = lt - CTX_LEN
    rows = n // GRID_W
    row = jnp.repeat(jnp.arange(rows, dtype=F32), GRID_W)
    col = jnp.tile(jnp.arange(GRID_W, dtype=F32), rows)
    n_freq = HEAD_DIM // 4
    inv = ROPE_THETA ** (-jnp.arange(n_freq, dtype=F32) / n_freq)
    ang = jnp.concatenate([row[:, None] * inv, col[:, None] * inv], axis=-1)
    half = HEAD_DIM // 2
    cos = jnp.concatenate([jnp.ones((CTX_LEN, half), F32), jnp.cos(ang)], axis=0)
    sin = jnp.concatenate([jnp.zeros((CTX_LEN, half), F32), jnp.sin(ang)], axis=0)
    reps = LANES // HEAD_DIM
    ct = jnp.tile(jnp.concatenate([cos, cos], axis=-1), (1, reps))
    st = jnp.tile(jnp.concatenate([-sin, sin], axis=-1), (1, reps))
    return jnp.repeat(ct, SUBLANES, axis=0), jnp.repeat(st, SUBLANES, axis=0)


def _rope_gains(gain, scale):
    g_eo = jnp.concatenate([gain[0::2], gain[1::2]])
    g_oe = jnp.concatenate([gain[1::2], gain[0::2]])
    reps = LANES // HEAD_DIM
    return jnp.stack([jnp.tile(g_eo, reps), jnp.tile(g_oe, reps)]).astype(F32) * scale


def _even_weights(w_in, w_out):
    aw, kw, rw = ATT_WIDTH, KV_WIDTH, RNN_WIDTH
    eo = jnp.concatenate([jnp.arange(0, HEAD_DIM, 2), jnp.arange(1, HEAD_DIM, 2)])
    oe = jnp.concatenate([jnp.arange(1, HEAD_DIM, 2), jnp.arange(0, HEAD_DIM, 2)])
    slot_head = jnp.array([h * ATT_GROUP + g for g in range(ATT_GROUP) for h in range(ATT_KV_HEADS)])
    q_cols = (slot_head[:, None] * HEAD_DIM + eo[None, :]).reshape(-1)
    q_cols_sw = (slot_head[:, None] * HEAD_DIM + oe[None, :]).reshape(-1)
    kv_heads = jnp.arange(ATT_KV_HEADS)
    k_cols = aw + (kv_heads[:, None] * HEAD_DIM + eo[None, :]).reshape(-1)
    k_cols_sw = aw + (kv_heads[:, None] * HEAD_DIM + oe[None, :]).reshape(-1)
    v_cols = aw + kw + jnp.arange(kw)
    slot_cols = (slot_head[:, None] * HEAD_DIM + jnp.arange(HEAD_DIM)[None, :]).reshape(-1)
    ga_cols = aw + 2 * kw + slot_cols
    u_cols = 2 * aw + 2 * kw + jnp.arange(rw)
    gr_cols = 2 * aw + 2 * kw + rw + jnp.arange(rw)
    cols = jnp.concatenate([q_cols, q_cols_sw, k_cols, k_cols_sw, v_cols, ga_cols, u_cols, gr_cols])
    w = w_in[:, cols].astype(BF16)
    wa = w_out[:aw][slot_cols].astype(BF16)
    wr = w_out[aw:].astype(BF16)
    return w, wa, wr


def _block_ones(width):
    idx = jnp.arange(width) // HEAD_DIM
    return (idx[:, None] == idx[None, :]).astype(BF16)


def _rnn_gate_weights(wa, ba, wx, bx):
    eye = jnp.eye(RNN_BLOCKS, dtype=F32)

    def dense(w):
        return jnp.einsum('hij,hk->hikj', w, eye).reshape(RNN_WIDTH, RNN_WIDTH)

    wg = jnp.concatenate([dense(wa), dense(wx)], axis=1).astype(BF16)
    bg = jnp.concatenate([ba.reshape(-1), bx.reshape(-1)])[None, :]
    return wg, bg


def _s5_discretize(lam_re, lam_im, log_step, b_re, b_im):
    dt = jnp.exp(log_step)[:, None]
    mag = jnp.exp(lam_re * dt)
    ab_re = mag * jnp.cos(lam_im * dt)
    ab_im = mag * jnp.sin(lam_im * dt)
    den = lam_re * lam_re + lam_im * lam_im
    nr, ni = ab_re - 1.0, ab_im
    f_re = (nr * lam_re + ni * lam_im) / den
    f_im = (ni * lam_re - nr * lam_im) / den
    bb_re = f_re[..., None] * b_re - f_im[..., None] * b_im
    bb_im = f_re[..., None] * b_im + f_im[..., None] * b_re
    return ab_re, ab_im, bb_re, bb_im


def _s5_weights(lam_re, lam_im, log_step, b_re, b_im, c_re, c_im, reverse):
    hp = lax.Precision.HIGHEST
    f = lambda t: t.astype(F32)
    lam_re, lam_im, log_step, b_re, b_im, c_re, c_im = map(f, (lam_re, lam_im, log_step, b_re, b_im, c_re, c_im))
    tc, nbk, gb, p, c = S5_TC, S5_NB, S5_GB, S5_STATE, S5_GROUP
    _, _, bb_re, bb_im = _s5_discretize(lam_re, lam_im, log_step, b_re, b_im)
    dt = jnp.exp(log_step)[:, None]
    e = jnp.arange(tc + 1, dtype=F32)[:, None, None]
    mag = jnp.exp(e * (lam_re * dt))
    pw_re = mag * jnp.cos(e * (lam_im * dt))
    pw_im = mag * jnp.sin(e * (lam_im * dt))
    ca_re = c_re[None] * pw_re[:, :, None, :] - c_im[None] * pw_im[:, :, None, :]
    ca_im = c_re[None] * pw_im[:, :, None, :] + c_im[None] * pw_re[:, :, None, :]
    kern = (jnp.einsum('tgjp,gpi->tgji', ca_re[:tc], bb_re, precision=hp)
            - jnp.einsum('tgjp,gpi->tgji', ca_im[:tc], bb_im, precision=hp))
    eye = jnp.eye(gb, dtype=F32)
    steps = jnp.arange(tc)
    lag = (steps[:, None] - steps[None, :]) if reverse else (steps[None, :] - steps[:, None])
    kt = jnp.where((lag >= 0)[:, :, None, None, None], kern[jnp.clip(lag, 0, tc - 1)], 0.0)
    kt = kt.reshape(tc, tc, nbk, gb, c, c)
    mi = jnp.einsum('stkgji,gh->ksgithj', kt, eye).reshape(nbk, tc * gb * c, tc * gb * c)
    e_s = steps if reverse else (tc - 1 - steps)
    ab_re = pw_re[e_s][..., None] * bb_re[None] - pw_im[e_s][..., None] * bb_im[None]
    ab_im = pw_re[e_s][..., None] * bb_im[None] + pw_im[e_s][..., None] * bb_re[None]
    ab = jnp.stack([ab_re, ab_im], 0).reshape(2, tc, nbk, gb, p, c)
    ms = jnp.einsum('askgpi,gh->ksgiahp', ab, eye).reshape(nbk, tc * gb * c, 2 * gb * p)
    e_t = (tc - steps) if reverse else (steps + 1)
    co = jnp.stack([ca_re[e_t], -ca_im[e_t]], 0).reshape(2, tc, nbk, gb, c, p)
    mo = jnp.einsum('atkgjp,gh->kagpthj', co, eye).reshape(nbk, 2 * gb * p, tc * gb * c)
    mio = jnp.concatenate([mi, mo], axis=1)
    a = jnp.stack([pw_re[tc], pw_im[tc]], 0).reshape(2, nbk, gb, p).transpose(1, 0, 2, 3).reshape(nbk, 1, 2 * gb * p)
    a = jnp.broadcast_to(a, (nbk, SUBLANES, 2 * gb * p))
    return ms.astype(BF16), mio.astype(BF16), a


def kernel(x, c, ctx, c_ctx, ada_w, ada_b, ev_w_in, ev_w_out, q_norm_w, k_norm_w, rg_conv_w, rg_conv_b, rg_wa, rg_ba, rg_wx, rg_bx, rg_lambda, od_w_in, s5_lambda_re, s5_lambda_im, s5_log_step, s5_b_re, s5_b_im, s5_c_re, s5_c_im, s5_d, glu_w, glu_b, od_w_out, final_norm_w):
    nb, seq, d = x.shape
    assert nb == SUBLANES and d == D_MODEL and ctx.shape[1] == CTX_LEN
    assert seq % ATT_TK == 0
    lt = CTX_LEN + seq

    xc = jnp.concatenate([ctx, x], axis=1).transpose(1, 0, 2).reshape(lt * nb, d)

    cond = jnp.zeros((16, d), F32).at[:nb].set(c).at[nb].set(c_ctx)
    mods = _modulation(cond, ada_w, ada_b).reshape(DEPTH, 16, 3, d)
    mods_x = mods[:, :nb].transpose(0, 2, 1, 3)
    mods_c = jnp.broadcast_to(mods[:, nb][:, :, None, :], mods_x.shape)
    mods = jnp.stack([mods_c, mods_x], axis=1)

    cos, sin = _rope_tables(lt)
    ones_q = _block_ones(ATT_WIDTH)
    ones_k = _block_ones(KV_WIDTH)
    fw = final_norm_w.reshape(1, d)
    out = None
    for layer in range(DEPTH):
        j = layer // 2
        ml = mods[layer]
        if layer % 2 == 0:
            w, wa, wr = _even_weights(ev_w_in[j], ev_w_out[j])
            gq = _rope_gains(q_norm_w[j], HEAD_DIM ** -0.5 * math.log2(math.e))
            gk = _rope_gains(k_norm_w[j], 1.0)
            q, k, v, ga, u, gr = _in_even(xc, ml, w, cos, sin, gq, gk, ones_q, ones_k)
            att = _attention(q, k, v)
            conv_b = rg_conv_b[j].reshape(1, RNN_WIDTH)
            y = None
            for direction, reverse in enumerate((False, True)):
                wg, bg = _rnn_gate_weights(rg_wa[j, direction], rg_ba[j, direction],
                                           rg_wx[j, direction], rg_bx[j, direction])
                clam = (-LRU_C * jax.nn.softplus(-rg_lambda[j, direction].astype(F32))).reshape(1, RNN_WIDTH)
                y = _rnn_dir(u, rg_conv_w[j], conv_b, wg, bg, clam, y, reverse)
            xc = _out_even(att, ga, y, gr, xc, ml, wa, wr)
        else:
            u, g = _in_odd(xc, ml, od_w_in[j].astype(BF16))
            u3 = u.reshape(lt // S5_TC, S5_TC * nb, S5_WIDTH)
            y = jnp.tile(s5_d[j].astype(F32).reshape(S5_NB, 1, LANES), (1, S5_TC, 1)).reshape(1, -1)
            for direction, reverse in enumerate((False, True)):
                ms, mio, a = _s5_weights(s5_lambda_re[j, direction], s5_lambda_im[j, direction],
                                         s5_log_step[j, direction], s5_b_re[j, direction],
                                         s5_b_im[j, direction], s5_c_re[j, direction],
                                         s5_c_im[j, direction], reverse)
                y = _s5_dir(u3, ms, mio, a, y, reverse)
            y = y.reshape(lt * nb, S5_WIDTH)
            final = layer == DEPTH - 1
            res = _out_odd(y, g, xc, ml, glu_w[j].astype(BF16), glu_b[j].reshape(1, -1),
                           od_w_out[j].astype(BF16), fw, final)
            if final:
                out = res
            else:
                xc = res
    return out
```

```python
import functools
import math

import jax
import jax.numpy as jnp
from jax import lax
from jax.experimental import pallas as pl
from jax.experimental.pallas import tpu as pltpu

F32 = jnp.float32
BF16 = jnp.bfloat16

D_MODEL = 1024
DEPTH = 4
CTX_LEN = 256
GRID_W = 64
EPS = 1e-6

ATT_HEADS = 8
ATT_KV_HEADS = 2
HEAD_DIM = 64
ATT_GROUP = ATT_HEADS // ATT_KV_HEADS
ATT_WIDTH = ATT_HEADS * HEAD_DIM
KV_WIDTH = ATT_KV_HEADS * HEAD_DIM
ROPE_THETA = 10000.0

RNN_WIDTH = D_MODEL // 2
RNN_BLOCKS = 8
RNN_BLOCK_DIM = RNN_WIDTH // RNN_BLOCKS
LRU_C = 8.0

S5_WIDTH = D_MODEL
S5_GROUP = 16
S5_GROUPS = S5_WIDTH // S5_GROUP
S5_STATE = 64

SUBLANES = 8
LANES = 128
ROW_TT = 64
ROWS = ROW_TT * SUBLANES
ATT_TQ = 128
ATT_TK = 1024
RNN_TT = 64
S5_TT = 256
S5_TC = 8
S5_GB = LANES // S5_GROUP
S5_NB = S5_GROUPS // S5_GB
S5_SW = S5_GB * S5_STATE
VMEM_LIMIT = 56 * 1024 * 1024


def _cparams(sem):
    return pltpu.CompilerParams(dimension_semantics=sem, vmem_limit_bytes=VMEM_LIMIT)


def _sigmoid(x):
    return 0.5 * jnp.tanh(0.5 * x) + 0.5


def _silu(x):
    return x * _sigmoid(x)


def _batch_rows(b):
    return pl.ds(b, ROW_TT, stride=SUBLANES)


def _mod_kernel(c_ref, w_ref, b_ref, o_ref):
    c = c_ref[...]
    sc = _silu(c).astype(BF16)
    o_ref[0] = jnp.dot(sc, w_ref[0].astype(BF16), preferred_element_type=F32) + b_ref[0]


def _modulation(cond, ada_w, ada_b):
    depth, d, d3 = ada_w.shape
    nblk = d3 // d
    return pl.pallas_call(
        _mod_kernel,
        grid=(depth, nblk),
        in_specs=[
            pl.BlockSpec((16, d), lambda l, n: (0, 0)),
            pl.BlockSpec((1, d, d), lambda l, n: (l, 0, n)),
            pl.BlockSpec((1, 1, d), lambda l, n: (l, 0, n)),
        ],
        out_specs=pl.BlockSpec((1, 16, d), lambda l, n: (l, 0, n)),
        out_shape=jax.ShapeDtypeStruct((depth, 16, d3), F32),
        compiler_params=_cparams(("parallel", "parallel")),
        name="ada_mod",
    )(cond, ada_w, ada_b.reshape(depth, 1, d3))


def _per_batch(x, vec):
    r, d = x.shape
    return (x.reshape(r // SUBLANES, SUBLANES, d) * vec[None]).reshape(r, d)


def _norm_mod(x, mod_ref):
    ms = jnp.mean(x * x, axis=-1, keepdims=True)
    xn = x * lax.rsqrt(ms + EPS)
    r, d = x.shape
    x3 = xn.reshape(r // SUBLANES, SUBLANES, d)
    return (x3 * (1.0 + mod_ref[0, 1])[None] + mod_ref[0, 0][None]).reshape(r, d)


def _mod_spec(n_ctx_tiles, offset=0):
    return pl.BlockSpec((1, 3, SUBLANES, D_MODEL),
                        lambda i: (jnp.where(i + offset < n_ctx_tiles, 0, 1), 0, 0, 0))


def _group_mean_sq(y, ones_ref):
    sq = y * y
    hi = sq.astype(BF16)
    lo = (sq - hi.astype(F32)).astype(BF16)
    ones = ones_ref[...]
    s = jnp.dot(hi, ones, preferred_element_type=F32) + jnp.dot(lo, ones, preferred_element_type=F32)
    return s * (1.0 / HEAD_DIM)


def _in_even_kernel(x_ref, mod_ref, w_ref, cos_ref, sin_ref, gq_ref, gk_ref, oq_ref, ok_ref,
                    q_ref, k_ref, v_ref, ga_ref, u_ref, gr_ref, q_scr, kv_scr):
    h = _norm_mod(x_ref[...], mod_ref).astype(BF16)
    y = jnp.dot(h, w_ref[...], preferred_element_type=F32)
    aw, kw = ATT_WIDTH, KV_WIDTH
    cos = cos_ref[...]
    sin = sin_ref[...]
    qp = y[:, 0:aw]
    qs = y[:, aw:2 * aw]
    rq = lax.rsqrt(_group_mean_sq(qp, oq_ref) + EPS)
    ca = gq_ref[0:1, :] * cos
    sa = gq_ref[1:2, :] * sin
    for j in range(aw // LANES):
        sl = slice(LANES * j, LANES * (j + 1))
        q_scr[j] = rq[:, sl] * (qp[:, sl] * ca + qs[:, sl] * sa)
    o = 2 * aw
    kp = y[:, o:o + kw]
    ks = y[:, o + kw:o + 2 * kw]
    rk = lax.rsqrt(_group_mean_sq(kp, ok_ref) + EPS)
    kv_scr[0] = rk * (kp * (gk_ref[0:1, :] * cos) + ks * (gk_ref[1:2, :] * sin))
    o += 2 * kw
    kv_scr[1] = y[:, o:o + kw]
    o += kw
    for b in range(SUBLANES):
        for j in range(aw // LANES):
            q_ref[b, :, LANES * j:LANES * (j + 1)] = q_scr[j, _batch_rows(b), :].astype(BF16)
        k_ref[b] = kv_scr[0, _batch_rows(b), :].astype(BF16)
        v_ref[b] = kv_scr[1, _batch_rows(b), :].astype(BF16)
    ga_ref[...] = y[:, o:o + aw].astype(BF16)
    o += aw
    u_ref[...] = y[:, o:o + RNN_WIDTH]
    o += RNN_WIDTH
    gr_ref[...] = y[:, o:o + RNN_WIDTH].astype(BF16)


def _in_even(xc, mods, w, cos, sin, gq, gk, ones_q, ones_k):
    nrows = xc.shape[0]
    nt = nrows // ROWS
    lt = nrows // SUBLANES
    d = D_MODEL
    aw, kw, rw = ATT_WIDTH, KV_WIDTH, RNN_WIDTH
    row = lambda i: (i, 0)
    const = lambda i: (0, 0)
    per_batch = lambda i: (0, i, 0)
    return pl.pallas_call(
        _in_even_kernel,
        grid=(nt,),
        in_specs=[
            pl.BlockSpec((ROWS, d), row),
            _mod_spec(CTX_LEN // ROW_TT),
            pl.BlockSpec(w.shape, const),
            pl.BlockSpec((ROWS, LANES), row),
            pl.BlockSpec((ROWS, LANES), row),
            pl.BlockSpec(gq.shape, const),
            pl.BlockSpec(gk.shape, const),
            pl.BlockSpec(ones_q.shape, const),
            pl.BlockSpec(ones_k.shape, const),
        ],
        out_specs=[
            pl.BlockSpec((SUBLANES, ROW_TT, aw), per_batch),
            pl.BlockSpec((SUBLANES, ROW_TT, kw), per_batch),
            pl.BlockSpec((SUBLANES, ROW_TT, kw), per_batch),
            pl.BlockSpec((ROWS, aw), row),
            pl.BlockSpec((ROWS, rw), row),
            pl.BlockSpec((ROWS, rw), row),
        ],
        out_shape=[
            jax.ShapeDtypeStruct((SUBLANES, lt, aw), BF16),
            jax.ShapeDtypeStruct((SUBLANES, lt, kw), BF16),
            jax.ShapeDtypeStruct((SUBLANES, lt, kw), BF16),
            jax.ShapeDtypeStruct((nrows, aw), BF16),
            jax.ShapeDtypeStruct((nrows, rw), F32),
            jax.ShapeDtypeStruct((nrows, rw), BF16),
        ],
        scratch_shapes=[pltpu.VMEM((aw // LANES, ROWS, LANES), F32), pltpu.VMEM((2, ROWS, LANES), F32)],
        compiler_params=_cparams(("parallel",)),
        name="even_in_proj",
    )(xc, mods, w, cos, sin, gq, gk, ones_q, ones_k)


def _aligned(start):
    return start if isinstance(start, int) else pl.multiple_of(start, LANES)


def _attn_kernel(q_ref, k_ref, v_ref, o_ref, kt_scr, vx_scr, s_scr, p_scr, acc_scr, *,
                 n_ctx_tiles, n_x_chunks):
    qt = pl.program_id(1)
    lane = lax.broadcasted_iota(jnp.int32, (1, LANES), 1)
    tq = ATT_TQ
    rows = ATT_GROUP * tq
    lt = v_ref.shape[1]
    n = n_x_chunks

    @pl.when(qt == 0)
    def _():
        def fill(i, carry):
            rs = pl.ds(pl.multiple_of(i * CTX_LEN, CTX_LEN), CTX_LEN)
            kt_scr[:, rs] = k_ref[0, rs, :].astype(F32).T.astype(BF16)
            vv = v_ref[0, rs, :]
            for h in range(ATT_KV_HEADS):
                vx_scr[h, rs, :] = jnp.where((lane // HEAD_DIM) == h, vv, jnp.ones((), BF16))
            return carry
        lax.fori_loop(0, lt // CTX_LEN, fill, 0)

    def key_range(c):
        return (CTX_LEN + c * ATT_TK, ATT_TK) if c < n else (0, CTX_LEN)

    for h in range(ATT_KV_HEADS):
        in_head = (lane // HEAD_DIM) == h
        qs = jnp.concatenate(
            [jnp.where(in_head, q_ref[0, :, LANES * g:LANES * (g + 1)], jnp.zeros((), BF16))
             for g in range(ATT_GROUP)], axis=0)

        def scores(start, width, slot, qs=qs):
            s = jnp.dot(qs, kt_scr[:, pl.ds(_aligned(start), width)], preferred_element_type=F32)
            s_scr[slot, :, 0:width] = s
            return jnp.broadcast_to(jnp.max(s, axis=-1, keepdims=True), (rows, LANES))

        def softmax(width, slot, m, mloc):
            m_new = jnp.maximum(m, mloc)
            for j in range(width // LANES):
                sl = slice(LANES * j, LANES * (j + 1))
                p_scr[slot, :, sl] = jnp.exp2(s_scr[slot, :, sl] - m_new).astype(BF16)
            return m_new, jnp.exp2(m - m_new)

        def weighted_values(start, width, slot, alpha, acc, h=h):
            pv = jnp.dot(p_scr[slot, :, 0:width], vx_scr[h, pl.ds(_aligned(start), width), :],
                         preferred_element_type=F32)
            return alpha * acc + pv

        @pl.when(qt < n_ctx_tiles)
        def _(h=h, scores=scores, softmax=softmax, weighted_values=weighted_values):
            mloc = scores(0, CTX_LEN, 0)
            softmax(CTX_LEN, 0, mloc, mloc)
            acc_scr[h] = weighted_values(0, CTX_LEN, 0, jnp.zeros((rows, LANES), F32),
                                         jnp.zeros((rows, LANES), F32))

        @pl.when(qt >= n_ctx_tiles)
        def _(h=h, scores=scores, softmax=softmax, weighted_values=weighted_values):
            def step(par, pv_rng, sm_width, qk_rng, carry):
                m, acc, alpha, mloc = carry
                mloc_new = scores(qk_rng[0], qk_rng[1], par) if qk_rng is not None else mloc
                acc = weighted_values(pv_rng[0], pv_rng[1], par, alpha, acc)
                if sm_width is not None:
                    m, alpha = softmax(sm_width, 1 - par, m, mloc)
                return m, acc, alpha, mloc_new

            mloc = scores(*key_range(0), 0)
            m, _ = softmax(key_range(0)[1], 0, mloc, mloc)
            mloc = scores(*key_range(1), 1)
            carry = (m, jnp.zeros((rows, LANES), F32), jnp.ones((rows, LANES), F32), mloc)

            for i in range(n + 1):
                carry = step(i % 2, key_range(i),
                             key_range(i + 1)[1] if i + 1 <= n else None,
                             key_range(i + 2) if i + 2 <= n else None, carry)
            acc_scr[h] = carry[1]

    first = lane < HEAD_DIM
    outs = []
    for h in range(ATT_KV_HEADS):
        acc = acc_scr[h]
        outs.append(acc / pltpu.roll(acc, HEAD_DIM, axis=1))
    for g in range(ATT_GROUP):
        rs = slice(tq * g, tq * (g + 1))
        o_ref[0, :, LANES * g:LANES * (g + 1)] = jnp.where(first, outs[0][rs], outs[1][rs])


def _attention(q, k, v):
    nb, lt, _ = q.shape
    nq = lt // ATT_TQ
    rows = ATT_GROUP * ATT_TQ
    kern = functools.partial(_attn_kernel, n_ctx_tiles=CTX_LEN // ATT_TQ,
                             n_x_chunks=(lt - CTX_LEN) // ATT_TK)
    return pl.pallas_call(
        kern,
        grid=(nb, nq),
        in_specs=[
            pl.BlockSpec((1, ATT_TQ, ATT_WIDTH), lambda b, i: (b, i, 0)),
            pl.BlockSpec((1, lt, KV_WIDTH), lambda b, i: (b, 0, 0)),
            pl.BlockSpec((1, lt, KV_WIDTH), lambda b, i: (b, 0, 0)),
        ],
        out_specs=pl.BlockSpec((1, ATT_TQ, ATT_WIDTH), lambda b, i: (b, i, 0)),
        out_shape=jax.ShapeDtypeStruct((nb, lt, ATT_WIDTH), F32),
        scratch_shapes=[
            pltpu.VMEM((KV_WIDTH, lt), BF16),
            pltpu.VMEM((ATT_KV_HEADS, lt, KV_WIDTH), BF16),
            pltpu.VMEM((2, rows, ATT_TK), F32),
            pltpu.VMEM((2, rows, ATT_TK), BF16),
            pltpu.VMEM((ATT_KV_HEADS, rows, LANES), F32),
        ],
        compiler_params=_cparams(("arbitrary", "arbitrary")),
        name="attention",
    )(q, k, v)


def _scan_tile(j, reverse, n_ctx_tiles, n_tiles):
    if not reverse:
        return j
    return jnp.where(j < n_ctx_tiles, n_ctx_tiles - 1 - j, n_tiles + n_ctx_tiles - 1 - j)


def _rnn_kernel(*refs, reverse, n_ctx_tiles, n_tiles, lt):
    if reverse:
        (u_ref, prev_ref, next_ref, cw_ref, cb_ref, wg_ref, bg_ref, clam_ref, yin_ref,
         y_ref, a_scr, b_scr, h_scr) = refs
    else:
        (u_ref, prev_ref, next_ref, cw_ref, cb_ref, wg_ref, bg_ref, clam_ref,
         y_ref, a_scr, b_scr, h_scr) = refs
        yin_ref = None
    j = pl.program_id(0)
    tile = _scan_tile(j, reverse, n_ctx_tiles, n_tiles)
    rows = RNN_TT * SUBLANES
    rw = RNN_WIDTH

    @pl.when(j == 0)
    def _():
        h_scr[...] = jnp.zeros_like(h_scr)

    t0 = tile * RNN_TT
    t1 = t0 + RNN_TT
    has_prev = jnp.logical_and(t0 != 0, t0 != CTX_LEN)
    has_next = jnp.logical_and(t1 != CTX_LEN, t1 != lt)
    u = u_ref[...]
    prev = jnp.where(has_prev, prev_ref[...], 0.0)
    nxt = jnp.where(has_next, next_ref[0:SUBLANES, :], 0.0)
    ext = jnp.concatenate([prev, u, nxt], axis=0)
    cw = cw_ref[...]
    s = SUBLANES
    cv = (cw[0:1] * ext[0:rows] + cw[1:2] * ext[s:rows + s] + cw[2:3] * ext[2 * s:rows + 2 * s]
          + cw[3:4] * ext[3 * s:rows + 3 * s] + cb_ref[...])
    g = jnp.dot(cv.astype(BF16), wg_ref[...], preferred_element_type=F32) + bg_ref[...]
    r = _sigmoid(g[:, 0:rw])
    i = _sigmoid(g[:, rw:2 * rw])
    log_a = clam_ref[...] * r
    a_scr[...] = jnp.exp(log_a)
    th = jnp.tanh(log_a)
    b_scr[...] = jnp.sqrt(-2.0 * th / (1.0 - th)) * (i * cv)

    def step(k, h):
        t = (RNN_TT - 1 - k) if reverse else k
        rs = pl.ds(pl.multiple_of(t * SUBLANES, SUBLANES), SUBLANES)
        h = a_scr[rs, :] * h + b_scr[rs, :]
        if reverse:
            y_ref[rs, :] = h + yin_ref[rs, :]
        else:
            y_ref[rs, :] = h
        return h

    h_scr[...] = lax.fori_loop(0, RNN_TT, step, h_scr[...], unroll=8)


def _rnn_dir(u, conv_w, conv_b, wg, bg, clam, yin, reverse):
    nrows = u.shape[0]
    rows = RNN_TT * SUBLANES
    n_tiles = nrows // rows
    n_ctx = CTX_LEN // RNN_TT
    lt = nrows // SUBLANES
    halo = 2 * SUBLANES
    per = rows // halo
    nhalo = nrows // halo
    tile_of = lambda j: _scan_tile(j, reverse, n_ctx, n_tiles)
    rw = RNN_WIDTH
    const = lambda j: (0, 0)
    in_specs = [
        pl.BlockSpec((rows, rw), lambda j: (tile_of(j), 0)),
        pl.BlockSpec((halo, rw), lambda j: (jnp.maximum(tile_of(j) * per - 1, 0), 0)),
        pl.BlockSpec((halo, rw), lambda j: (jnp.minimum((tile_of(j) + 1) * per, nhalo - 1), 0)),
        pl.BlockSpec((4, rw), const),
        pl.BlockSpec((1, rw), const),
        pl.BlockSpec((rw, 2 * rw), const),
        pl.BlockSpec((1, 2 * rw), const),
        pl.BlockSpec((1, rw), const),
    ]
    args = [u, u, u, conv_w, conv_b, wg, bg, clam]
    if reverse:
        in_specs.append(pl.BlockSpec((rows, rw), lambda j: (tile_of(j), 0)))
        args.append(yin)
    kern = functools.partial(_rnn_kernel, reverse=reverse, n_ctx_tiles=n_ctx, n_tiles=n_tiles, lt=lt)
    return pl.pallas_call(
        kern,
        grid=(n_tiles,),
        in_specs=in_specs,
        out_specs=pl.BlockSpec((rows, rw), lambda j: (tile_of(j), 0)),
        out_shape=jax.ShapeDtypeStruct((nrows, rw), F32),
        scratch_shapes=[pltpu.VMEM((rows, rw), F32), pltpu.VMEM((rows, rw), F32),
                        pltpu.VMEM((SUBLANES, rw), F32)],
        compiler_params=_cparams(("arbitrary",)),
        name="rglru_bwd" if reverse else "rglru_fwd",
    )(*args)


def _out_even_kernel(att_ref, ga_ref, y_ref, gr_ref, x_ref, mod_ref, wa_ref, wr_ref, o_ref, a_scr):
    nblk = ATT_WIDTH // LANES
    for b in range(SUBLANES):
        for j in range(nblk):
            a_scr[j, _batch_rows(b), :] = att_ref[b, :, LANES * j:LANES * (j + 1)]
    att = jnp.concatenate([a_scr[j] for j in range(nblk)], axis=1)
    m1 = (att * _silu(ga_ref[...].astype(F32))).astype(BF16)
    m2 = (y_ref[...] * _silu(gr_ref[...].astype(F32))).astype(BF16)
    o = (jnp.dot(m1, wa_ref[...], preferred_element_type=F32)
         + jnp.dot(m2, wr_ref[...], preferred_element_type=F32))
    o_ref[...] = x_ref[...] + _per_batch(o, mod_ref[0, 2])


def _out_even(att, ga, y, gr, xc, mods, wa, wr):
    nrows = xc.shape[0]
    nt = nrows // ROWS
    d = D_MODEL
    row = lambda i: (i, 0)
    const = lambda i: (0, 0)
    return pl.pallas_call(
        _out_even_kernel,
        grid=(nt,),
        in_specs=[
            pl.BlockSpec((SUBLANES, ROW_TT, ATT_WIDTH), lambda i: (0, i, 0)),
            pl.BlockSpec((ROWS, ATT_WIDTH), row),
            pl.BlockSpec((ROWS, RNN_WIDTH), row),
            pl.BlockSpec((ROWS, RNN_WIDTH), row),
            pl.BlockSpec((ROWS, d), row),
            _mod_spec(CTX_LEN // ROW_TT),
            pl.BlockSpec(wa.shape, const),
            pl.BlockSpec(wr.shape, const),
        ],
        out_specs=pl.BlockSpec((ROWS, d), row),
        out_shape=jax.ShapeDtypeStruct(xc.shape, F32),
        scratch_shapes=[pltpu.VMEM((ATT_WIDTH // LANES, ROWS, LANES), F32)],
        compiler_params=_cparams(("parallel",)),
        name="even_out_proj",
    )(att, ga, y, gr, xc, mods, wa, wr)


def _in_odd_kernel(x_ref, mod_ref, w_ref, u_ref, g_ref):
    h = _norm_mod(x_ref[...], mod_ref).astype(BF16)
    y = jnp.dot(h, w_ref[...], preferred_element_type=F32)
    u_ref[...] = y[:, 0:S5_WIDTH]
    g_ref[...] = y[:, S5_WIDTH:2 * S5_WIDTH].astype(BF16)


def _in_odd(xc, mods, w):
    nrows = xc.shape[0]
    nt = nrows // ROWS
    d = D_MODEL
    row = lambda i: (i, 0)
    return pl.pallas_call(
        _in_odd_kernel,
        grid=(nt,),
        in_specs=[
            pl.BlockSpec((ROWS, d), row),
            _mod_spec(CTX_LEN // ROW_TT),
            pl.BlockSpec(w.shape, lambda i: (0, 0)),
        ],
        out_specs=[pl.BlockSpec((ROWS, S5_WIDTH), row), pl.BlockSpec((ROWS, S5_WIDTH), row)],
        out_shape=[jax.ShapeDtypeStruct((nrows, S5_WIDTH), F32),
                   jax.ShapeDtypeStruct((nrows, S5_WIDTH), BF16)],
        compiler_params=_cparams(("parallel",)),
        name="odd_in_proj",
    )(xc, mods, w)


def _s5_kernel(u_ref, msc_ref, mic_ref, moc_ref, etj_ref, eap_ref, a_ref, extra_ref, y_ref,
               ms_scr, mio_scr, hp_scr, h_scr, *, reverse):
    j = pl.program_id(1)
    nc = S5_TT // S5_TC
    rows = nc * SUBLANES
    sw = S5_SW

    @pl.when(j == 0)
    def _():
        h_scr[...] = jnp.zeros_like(h_scr)
        cw = S5_TC * LANES
        r = lax.broadcasted_iota(jnp.int32, (cw, 1), 0)
        c = lax.broadcasted_iota(jnp.int32, (1, cw), 1)
        row_group_ch = (r // S5_GROUP) % S5_GB
        row_group_st = (r // S5_STATE) % S5_GB
        col_group_ch = (c // S5_GROUP) % S5_GB
        col_group_st = (c // S5_STATE) % S5_GB

        def expand(compact_ref, e_ref, keep):
            full = jnp.dot(compact_ref[0], e_ref[...], preferred_element_type=F32)
            return jnp.where(keep, full, 0.0).astype(BF16)

        ms_scr[...] = expand(msc_ref, eap_ref, row_group_ch == col_group_st)
        mio_scr[0:cw, :] = expand(mic_ref, etj_ref, row_group_ch == col_group_ch)
        mio_scr[cw:2 * cw, :] = expand(moc_ref, etj_ref, row_group_st == col_group_ch)

    def chunk_major(ref):
        return jnp.concatenate(
            [ref[:, SUBLANES * t:SUBLANES * (t + 1), :].reshape(rows, LANES) for t in range(S5_TC)], axis=1)

    lhs32 = chunk_major(u_ref)
    lhs = lhs32.astype(BF16)
    local = jnp.dot(lhs, ms_scr[...], preferred_element_type=F32)
    ar = a_ref[0, :, 0:sw]
    ai = a_ref[0, :, sw:2 * sw]
    hr = h_scr[:, 0:sw]
    hi = h_scr[:, sw:2 * sw]
    for c in (range(nc - 1, -1, -1) if reverse else range(nc)):
        rs = slice(SUBLANES * c, SUBLANES * (c + 1))
        hp_scr[rs, 0:sw] = hr
        hp_scr[rs, sw:2 * sw] = hi
        hr, hi = (ar * hr - ai * hi + local[rs, 0:sw], ar * hi + ai * hr + local[rs, sw:2 * sw])
    h_scr[:, 0:sw] = hr
    h_scr[:, sw:2 * sw] = hi
    both = jnp.concatenate([lhs, hp_scr[...].astype(BF16)], axis=1)
    y = jnp.dot(both, mio_scr[...], preferred_element_type=F32)
    if reverse:
        y = y + chunk_major(extra_ref)
    else:
        y = y + extra_ref[...] * lhs32
    for t in range(S5_TC):
        y_ref[:, SUBLANES * t:SUBLANES * (t + 1), :] = (
            y[:, LANES * t:LANES * (t + 1)].reshape(nc, SUBLANES, LANES))


def _s5_dir(u3, msc, mic, moc, etj, eap, a, extra, reverse):
    nchunks, crow, width = u3.shape
    nc = S5_TT // S5_TC
    n_tiles = nchunks // nc
    n_ctx = CTX_LEN // S5_TT
    tile_of = lambda j: _scan_tile(j, reverse, n_ctx, n_tiles)
    blk = lambda k, j: (tile_of(j), 0, k)
    wblk = lambda k, j: (k, 0, 0)
    cw = S5_TC * LANES
    if reverse:
        extra_spec = pl.BlockSpec((nc, crow, LANES), blk)
    else:
        extra_spec = pl.BlockSpec((1, cw), lambda k, j: (0, k))
    return pl.pallas_call(
        functools.partial(_s5_kernel, reverse=reverse),
        grid=(S5_NB, n_tiles),
        in_specs=[
            pl.BlockSpec((nc, crow, LANES), blk),
            pl.BlockSpec((1, cw, LANES), wblk),
            pl.BlockSpec((1, cw, LANES), wblk),
            pl.BlockSpec((1, cw, LANES), wblk),
            pl.BlockSpec(etj.shape, lambda k, j: (0, 0)),
            pl.BlockSpec(eap.shape, lambda k, j: (0, 0)),
            pl.BlockSpec((1, SUBLANES, 2 * S5_SW), wblk),
            extra_spec,
        ],
        out_specs=pl.BlockSpec((nc, crow, LANES), blk),
        out_shape=jax.ShapeDtypeStruct(u3.shape, F32),
        scratch_shapes=[pltpu.VMEM((cw, 2 * S5_SW), BF16), pltpu.VMEM((cw + 2 * S5_SW, cw), BF16),
                        pltpu.VMEM((nc * SUBLANES, 2 * S5_SW), F32), pltpu.VMEM((SUBLANES, 2 * S5_SW), F32)],
        compiler_params=_cparams(("arbitrary", "arbitrary")),
        name="s5_bwd" if reverse else "s5_fwd",
    )(u3, msc, mic, moc, etj, eap, a, extra)


def _out_odd_kernel(y_ref, g_ref, x_ref, mod_ref, gw_ref, gb_ref, wo_ref, fw_ref, o_ref, *scr, final):
    yy = jax.nn.gelu(y_ref[...]).astype(BF16)
    z = jnp.dot(yy, gw_ref[...], preferred_element_type=F32) + gb_ref[...]
    g = g_ref[...].astype(F32)
    m = z[:, 0:S5_WIDTH] * _sigmoid(z[:, S5_WIDTH:2 * S5_WIDTH]) * _silu(g)
    o = jnp.dot(m.astype(BF16), wo_ref[...], preferred_element_type=F32)
    xn = x_ref[...] + _per_batch(o, mod_ref[0, 2])
    if final:
        (o_scr,) = scr
        ms = jnp.mean(xn * xn, axis=-1, keepdims=True)
        xo = xn * lax.rsqrt(ms + EPS) * fw_ref[...]
        nblk = D_MODEL // LANES
        for j in range(nblk):
            o_scr[j] = xo[:, LANES * j:LANES * (j + 1)]
        for b in range(SUBLANES):
            for j in range(nblk):
                o_ref[b, :, LANES * j:LANES * (j + 1)] = o_scr[j, _batch_rows(b), :]
    else:
        o_ref[...] = xn


def _out_odd(y, g, xc, mods, gw, gb, wo, fw, final):
    nrows = xc.shape[0]
    d = D_MODEL
    nct = CTX_LEN // ROW_TT
    lt = nrows // SUBLANES
    if final:
        nt = (lt - CTX_LEN) // ROW_TT
        row = lambda i: (i + nct, 0)
        mod_spec = _mod_spec(nct, offset=nct)
        out_spec = pl.BlockSpec((SUBLANES, ROW_TT, d), lambda i: (0, i, 0))
        out_shape = jax.ShapeDtypeStruct((SUBLANES, lt - CTX_LEN, d), F32)
        scratch = [pltpu.VMEM((d // LANES, ROWS, LANES), F32)]
    else:
        nt = lt // ROW_TT
        row = lambda i: (i, 0)
        mod_spec = _mod_spec(nct)
        out_spec = pl.BlockSpec((ROWS, d), row)
        out_shape = jax.ShapeDtypeStruct(xc.shape, F32)
        scratch = []
    const = lambda i: (0, 0)
    return pl.pallas_call(
        functools.partial(_out_odd_kernel, final=final),
        grid=(nt,),
        in_specs=[
            pl.BlockSpec((ROWS, S5_WIDTH), row),
            pl.BlockSpec((ROWS, S5_WIDTH), row),
            pl.BlockSpec((ROWS, d), row),
            mod_spec,
            pl.BlockSpec(gw.shape, const),
            pl.BlockSpec(gb.shape, const),
            pl.BlockSpec(wo.shape, const),
            pl.BlockSpec(fw.shape, const),
        ],
        out_specs=out_spec,
        out_shape=out_shape,
        scratch_shapes=scratch,
        compiler_params=_cparams(("parallel",)),
        name="odd_out_proj_final" if final else "odd_out_proj",
    )(y, g, xc, mods, gw, gb, wo, fw)


def _rope_tables(lt):
    n = lt - CTX_LEN
    rows = n // GRID_W
    row = jnp.repeat(jnp.arange(rows, dtype=F32), GRID_W)
    col = jnp.tile(jnp.arange(GRID_W, dtype=F32), rows)
    n_freq = HEAD_DIM // 4
    inv = ROPE_THETA ** (-jnp.arange(n_freq, dtype=F32) / n_freq)
    ang = jnp.concatenate([row[:, None] * inv, col[:, None] * inv], axis=-1)
    half = HEAD_DIM // 2
    cos = jnp.concatenate([jnp.ones((CTX_LEN, half), F32), jnp.cos(ang)], axis=0)
    sin = jnp.concatenate([jnp.zeros((CTX_LEN, half), F32), jnp.sin(ang)], axis=0)
    reps = LANES // HEAD_DIM
    ct = jnp.tile(jnp.concatenate([cos, cos], axis=-1), (1, reps))
    st = jnp.tile(jnp.concatenate([-sin, sin], axis=-1), (1, reps))
    ct, st = lax.optimization_barrier((ct, st))
    return jnp.repeat(ct, SUBLANES, axis=0), jnp.repeat(st, SUBLANES, axis=0)


def _rope_gains(gain, scale):
    g_eo = jnp.concatenate([gain[0::2], gain[1::2]])
    g_oe = jnp.concatenate([gain[1::2], gain[0::2]])
    reps = LANES // HEAD_DIM
    return jnp.stack([jnp.tile(g_eo, reps), jnp.tile(g_oe, reps)]).astype(F32) * scale


def _even_weights(w_in, w_out):
    aw, kw, rw = ATT_WIDTH, KV_WIDTH, RNN_WIDTH
    eo = jnp.concatenate([jnp.arange(0, HEAD_DIM, 2), jnp.arange(1, HEAD_DIM, 2)])
    oe = jnp.concatenate([jnp.arange(1, HEAD_DIM, 2), jnp.arange(0, HEAD_DIM, 2)])
    slot_head = jnp.array([h * ATT_GROUP + g for g in range(ATT_GROUP) for h in range(ATT_KV_HEADS)])
    q_cols = (slot_head[:, None] * HEAD_DIM + eo[None, :]).reshape(-1)
    q_cols_sw = (slot_head[:, None] * HEAD_DIM + oe[None, :]).reshape(-1)
    kv_heads = jnp.arange(ATT_KV_HEADS)
    k_cols = aw + (kv_heads[:, None] * HEAD_DIM + eo[None, :]).reshape(-1)
    k_cols_sw = aw + (kv_heads[:, None] * HEAD_DIM + oe[None, :]).reshape(-1)
    v_cols = aw + kw + jnp.arange(kw)
    slot_cols = (slot_head[:, None] * HEAD_DIM + jnp.arange(HEAD_DIM)[None, :]).reshape(-1)
    ga_cols = aw + 2 * kw + slot_cols
    u_cols = 2 * aw + 2 * kw + jnp.arange(rw)
    gr_cols = 2 * aw + 2 * kw + rw + jnp.arange(rw)
    cols = jnp.concatenate([q_cols, q_cols_sw, k_cols, k_cols_sw, v_cols, ga_cols, u_cols, gr_cols])
    w = w_in[:, cols].astype(BF16)
    wa = w_out[:aw][slot_cols].astype(BF16)
    wr = w_out[aw:].astype(BF16)
    return w, wa, wr


def _block_ones(width):
    idx = jnp.arange(width) // HEAD_DIM
    return (idx[:, None] == idx[None, :]).astype(BF16)


def _rnn_gate_weights(wa, ba, wx, bx):
    eye = jnp.eye(RNN_BLOCKS, dtype=F32)

    def dense(w):
        return jnp.einsum('hij,hk->hikj', w, eye).reshape(RNN_WIDTH, RNN_WIDTH)

    wg = jnp.concatenate([dense(wa), dense(wx)], axis=1).astype(BF16)
    bg = jnp.concatenate([ba.reshape(-1), bx.reshape(-1)])[None, :]
    return wg, bg


def _s5_discretize(lam_re, lam_im, log_step, b_re, b_im):
    dt = jnp.exp(log_step)[:, None]
    mag = jnp.exp(lam_re * dt)
    ab_re = mag * jnp.cos(lam_im * dt)
    ab_im = mag * jnp.sin(lam_im * dt)
    den = lam_re * lam_re + lam_im * lam_im
    nr, ni = ab_re - 1.0, ab_im
    f_re = (nr * lam_re + ni * lam_im) / den
    f_im = (ni * lam_re - nr * lam_im) / den
    bb_re = f_re[..., None] * b_re - f_im[..., None] * b_im
    bb_im = f_re[..., None] * b_im + f_im[..., None] * b_re
    return ab_re, ab_im, bb_re, bb_im


def _s5_weights(lam_re, lam_im, log_step, b_re, b_im, c_re, c_im, reverse):
    hp = lax.Precision.HIGHEST
    f = lambda t: t.astype(F32)
    lam_re, lam_im, log_step, b_re, b_im, c_re, c_im = map(f, (lam_re, lam_im, log_step, b_re, b_im, c_re, c_im))
    tc, nbk, gb, p, c = S5_TC, S5_NB, S5_GB, S5_STATE, S5_GROUP
    _, _, bb_re, bb_im = _s5_discretize(lam_re, lam_im, log_step, b_re, b_im)
    dt = jnp.exp(log_step)[:, None]
    e = jnp.arange(tc + 1, dtype=F32)[:, None, None]
    mag = jnp.exp(e * (lam_re * dt))
    pw_re = mag * jnp.cos(e * (lam_im * dt))
    pw_im = mag * jnp.sin(e * (lam_im * dt))
    ca_re = c_re[None] * pw_re[:, :, None, :] - c_im[None] * pw_im[:, :, None, :]
    ca_im = c_re[None] * pw_im[:, :, None, :] + c_im[None] * pw_re[:, :, None, :]
    kern = (jnp.einsum('tgjp,gpi->tgji', ca_re[:tc], bb_re, precision=hp)
            - jnp.einsum('tgjp,gpi->tgji', ca_im[:tc], bb_im, precision=hp))
    steps = jnp.arange(tc)
    lag = (steps[:, None] - steps[None, :]) if reverse else (steps[None, :] - steps[:, None])
    kt = jnp.where((lag >= 0)[:, :, None, None, None], kern[jnp.clip(lag, 0, tc - 1)], 0.0)
    kt = kt.reshape(tc, tc, nbk, gb, c, c)
    mic = kt.transpose(2, 0, 3, 5, 1, 4).reshape(nbk, tc * gb * c, tc * c)
    e_s = steps if reverse else (tc - 1 - steps)
    ab_re = pw_re[e_s][..., None] * bb_re[None] - pw_im[e_s][..., None] * bb_im[None]
    ab_im = pw_re[e_s][..., None] * bb_im[None] + pw_im[e_s][..., None] * bb_re[None]
    ab = jnp.stack([ab_re, ab_im], 0).reshape(2, tc, nbk, gb, p, c)
    msc = ab.transpose(2, 1, 3, 5, 0, 4).reshape(nbk, tc * gb * c, 2 * p)
    e_t = (tc - steps) if reverse else (steps + 1)
    co = jnp.stack([ca_re[e_t], -ca_im[e_t]], 0).reshape(2, tc, nbk, gb, c, p)
    moc = co.transpose(2, 0, 3, 5, 1, 4).reshape(nbk, 2 * gb * p, tc * c)
    a = jnp.stack([pw_re[tc], pw_im[tc]], 0).reshape(2, nbk, gb, p).transpose(1, 0, 2, 3).reshape(nbk, 1, 2 * gb * p)
    a = jnp.broadcast_to(a, (nbk, SUBLANES, 2 * gb * p))
    return msc.astype(BF16), mic.astype(BF16), moc.astype(BF16), a


def _s5_expanders():
    cw = S5_TC * LANES
    col = jnp.arange(cw)
    src_tj = (col // LANES) * S5_GROUP + col % S5_GROUP
    src_ap = (col // S5_SW) * S5_STATE + col % S5_STATE
    etj = (jnp.arange(S5_TC * S5_GROUP)[:, None] == src_tj[None, :]).astype(BF16)
    eap = (jnp.arange(2 * S5_STATE)[:, None] == src_ap[None, :]).astype(BF16)
    return etj, eap


def kernel(x, c, ctx, c_ctx, ada_w, ada_b, ev_w_in, ev_w_out, q_norm_w, k_norm_w, rg_conv_w, rg_conv_b, rg_wa, rg_ba, rg_wx, rg_bx, rg_lambda, od_w_in, s5_lambda_re, s5_lambda_im, s5_log_step, s5_b_re, s5_b_im, s5_c_re, s5_c_im, s5_d, glu_w, glu_b, od_w_out, final_norm_w):
    nb, seq, d = x.shape
    assert nb == SUBLANES and d == D_MODEL and ctx.shape[1] == CTX_LEN
    assert seq % ATT_TK == 0
    lt = CTX_LEN + seq

    xc = jnp.concatenate([ctx, x], axis=1).transpose(1, 0, 2).reshape(lt * nb, d)

    cond = jnp.zeros((16, d), F32).at[:nb].set(c).at[nb].set(c_ctx)
    mods = _modulation(cond, ada_w, ada_b).reshape(DEPTH, 16, 3, d)
    mods_x = mods[:, :nb].transpose(0, 2, 1, 3)
    mods_c = jnp.broadcast_to(mods[:, nb][:, :, None, :], mods_x.shape)
    mods = jnp.stack([mods_c, mods_x], axis=1)

    cos, sin = _rope_tables(lt)
    ones_q = _block_ones(ATT_WIDTH)
    ones_k = _block_ones(KV_WIDTH)
    etj, eap = _s5_expanders()
    fw = final_norm_w.reshape(1, d)
    out = None
    for layer in range(DEPTH):
        j = layer // 2
        ml = mods[layer]
        if layer % 2 == 0:
            w, wa, wr = _even_weights(ev_w_in[j], ev_w_out[j])
            gq = _rope_gains(q_norm_w[j], HEAD_DIM ** -0.5 * math.log2(math.e))
            gk = _rope_gains(k_norm_w[j], 1.0)
            q, k, v, ga, u, gr = _in_even(xc, ml, w, cos, sin, gq, gk, ones_q, ones_k)
            att = _attention(q, k, v)
            conv_b = rg_conv_b[j].reshape(1, RNN_WIDTH)
            y = None
            for direction, reverse in enumerate((False, True)):
                wg, bg = _rnn_gate_weights(rg_wa[j, direction], rg_ba[j, direction],
                                           rg_wx[j, direction], rg_bx[j, direction])
                clam = (-LRU_C * jax.nn.softplus(-rg_lambda[j, direction].astype(F32))).reshape(1, RNN_WIDTH)
                y = _rnn_dir(u, rg_conv_w[j], conv_b, wg, bg, clam, y, reverse)
            xc = _out_even(att, ga, y, gr, xc, ml, wa, wr)
        else:
            u, g = _in_odd(xc, ml, od_w_in[j].astype(BF16))
            u3 = u.reshape(lt // S5_TC, S5_TC * nb, S5_WIDTH)
            y = jnp.tile(s5_d[j].astype(F32).reshape(S5_NB, 1, LANES), (1, S5_TC, 1)).reshape(1, -1)
            for direction, reverse in enumerate((False, True)):
                msc, mic, moc, a = _s5_weights(s5_lambda_re[j, direction], s5_lambda_im[j, direction],
                                               s5_log_step[j, direction], s5_b_re[j, direction],
                                               s5_b_im[j, direction], s5_c_re[j, direction],
                                               s5_c_im[j, direction], reverse)
                y = _s5_dir(u3, msc, mic, moc, etj, eap, a, y, reverse)
            y = y.reshape(lt * nb, S5_WIDTH)
            final = layer == DEPTH - 1
            res = _out_odd(y, g, xc, ml, glu_w[j].astype(BF16), glu_b[j].reshape(1, -1),
                           od_w_out[j].astype(BF16), fw, final)
            if final:
                out = res
            else:
                xc = res
    return out
```

```python
import functools
import math

import jax
import jax.numpy as jnp
from jax import lax
from jax.experimental import pallas as pl
from jax.experimental.pallas import tpu as pltpu

F32 = jnp.float32
BF16 = jnp.bfloat16

D_MODEL = 1024
DEPTH = 4
CTX_LEN = 256
GRID_W = 64
EPS = 1e-6

ATT_HEADS = 8
ATT_KV_HEADS = 2
HEAD_DIM = 64
ATT_GROUP = ATT_HEADS // ATT_KV_HEADS
ATT_WIDTH = ATT_HEADS * HEAD_DIM
KV_WIDTH = ATT_KV_HEADS * HEAD_DIM
ROPE_THETA = 10000.0

RNN_WIDTH = D_MODEL // 2
RNN_BLOCKS = 8
RNN_BLOCK_DIM = RNN_WIDTH // RNN_BLOCKS
LRU_C = 8.0

S5_WIDTH = D_MODEL
S5_GROUP = 16
S5_GROUPS = S5_WIDTH // S5_GROUP
S5_STATE = 64

SUBLANES = 8
LANES = 128
ROW_TT = 64
ROWS = ROW_TT * SUBLANES
ATT_TQ = 128
ATT_TK = 1024
RNN_TT = 64
S5_TT = 256
S5_TC = 8
S5_GB = LANES // S5_GROUP
S5_NB = S5_GROUPS // S5_GB
S5_SW = S5_GB * S5_STATE
VMEM_LIMIT = 56 * 1024 * 1024


def _cparams(sem):
    return pltpu.CompilerParams(dimension_semantics=sem, vmem_limit_bytes=VMEM_LIMIT)


def _sigmoid(x):
    return 0.5 * jnp.tanh(0.5 * x) + 0.5


def _silu(x):
    return x * _sigmoid(x)


def _batch_rows(b):
    return pl.ds(b, ROW_TT, stride=SUBLANES)


def _mod_kernel(c_ref, w_ref, b_ref, o_ref):
    c = c_ref[...]
    sc = _silu(c).astype(BF16)
    o_ref[0] = jnp.dot(sc, w_ref[0].astype(BF16), preferred_element_type=F32) + b_ref[0]


def _modulation(cond, ada_w, ada_b):
    depth, d, d3 = ada_w.shape
    nblk = d3 // d
    return pl.pallas_call(
        _mod_kernel,
        grid=(depth, nblk),
        in_specs=[
            pl.BlockSpec((16, d), lambda l, n: (0, 0)),
            pl.BlockSpec((1, d, d), lambda l, n: (l, 0, n)),
            pl.BlockSpec((1, 1, d), lambda l, n: (l, 0, n)),
        ],
        out_specs=pl.BlockSpec((1, 16, d), lambda l, n: (l, 0, n)),
        out_shape=jax.ShapeDtypeStruct((depth, 16, d3), F32),
        compiler_params=_cparams(("parallel", "parallel")),
        name="ada_mod",
    )(cond, ada_w, ada_b.reshape(depth, 1, d3))


def _to_rows_kernel(c_ref, x_ref, o_ref, scr, *, n_ctx_tiles):
    i = pl.program_id(0)
    nblk = D_MODEL // LANES

    def move(src_ref):
        for b in range(SUBLANES):
            for j in range(nblk):
                scr[j, _batch_rows(b), :] = src_ref[b, :, LANES * j:LANES * (j + 1)]
        for j in range(nblk):
            o_ref[:, LANES * j:LANES * (j + 1)] = scr[j]

    @pl.when(i < n_ctx_tiles)
    def _():
        move(c_ref)

    @pl.when(i >= n_ctx_tiles)
    def _():
        move(x_ref)


def _to_rows(ctx, x):
    nb, seq, d = x.shape
    nct = ctx.shape[1] // ROW_TT
    nt = nct + seq // ROW_TT
    return pl.pallas_call(
        functools.partial(_to_rows_kernel, n_ctx_tiles=nct),
        grid=(nt,),
        in_specs=[
            pl.BlockSpec((nb, ROW_TT, d), lambda i: (0, jnp.minimum(i, nct - 1), 0)),
            pl.BlockSpec((nb, ROW_TT, d), lambda i: (0, jnp.maximum(i - nct, 0), 0)),
        ],
        out_specs=pl.BlockSpec((ROWS, d), lambda i: (i, 0)),
        out_shape=jax.ShapeDtypeStruct((nt * ROWS, d), F32),
        scratch_shapes=[pltpu.VMEM((d // LANES, ROWS, LANES), F32)],
        compiler_params=_cparams(("parallel",)),
        name="to_rows",
    )(ctx, x)


def _per_batch(x, vec):
    r, d = x.shape
    return (x.reshape(r // SUBLANES, SUBLANES, d) * vec[None]).reshape(r, d)


def _norm_mod(x, mod_ref):
    ms = jnp.mean(x * x, axis=-1, keepdims=True)
    xn = x * lax.rsqrt(ms + EPS)
    r, d = x.shape
    x3 = xn.reshape(r // SUBLANES, SUBLANES, d)
    return (x3 * (1.0 + mod_ref[0, 1])[None] + mod_ref[0, 0][None]).reshape(r, d)


def _mod_spec(n_ctx_tiles, offset=0):
    return pl.BlockSpec((1, 3, SUBLANES, D_MODEL),
                        lambda i: (jnp.where(i + offset < n_ctx_tiles, 0, 1), 0, 0, 0))


def _group_mean_sq(y, ones_ref):
    sq = y * y
    hi = sq.astype(BF16)
    lo = (sq - hi.astype(F32)).astype(BF16)
    ones = ones_ref[...]
    s = jnp.dot(hi, ones, preferred_element_type=F32) + jnp.dot(lo, ones, preferred_element_type=F32)
    return s * (1.0 / HEAD_DIM)


def _in_even_kernel(x_ref, mod_ref, w_ref, cos_ref, sin_ref, gq_ref, gk_ref, oq_ref, ok_ref,
                    q_ref, k_ref, v_ref, ga_ref, u_ref, gr_ref, q_scr, kv_scr):
    h = _norm_mod(x_ref[...], mod_ref).astype(BF16)
    y = jnp.dot(h, w_ref[...], preferred_element_type=F32)
    aw, kw = ATT_WIDTH, KV_WIDTH
    cos = cos_ref[...]
    sin = sin_ref[...]
    qp = y[:, 0:aw]
    qs = y[:, aw:2 * aw]
    rq = lax.rsqrt(_group_mean_sq(qp, oq_ref) + EPS)
    ca = gq_ref[0:1, :] * cos
    sa = gq_ref[1:2, :] * sin
    for j in range(aw // LANES):
        sl = slice(LANES * j, LANES * (j + 1))
        q_scr[j] = rq[:, sl] * (qp[:, sl] * ca + qs[:, sl] * sa)
    o = 2 * aw
    kp = y[:, o:o + kw]
    ks = y[:, o + kw:o + 2 * kw]
    rk = lax.rsqrt(_group_mean_sq(kp, ok_ref) + EPS)
    kv_scr[0] = rk * (kp * (gk_ref[0:1, :] * cos) + ks * (gk_ref[1:2, :] * sin))
    o += 2 * kw
    kv_scr[1] = y[:, o:o + kw]
    o += kw
    for b in range(SUBLANES):
        for j in range(aw // LANES):
            q_ref[b, :, LANES * j:LANES * (j + 1)] = q_scr[j, _batch_rows(b), :].astype(BF16)
        k_ref[b] = kv_scr[0, _batch_rows(b), :].astype(BF16)
        v_ref[b] = kv_scr[1, _batch_rows(b), :].astype(BF16)
    ga_ref[...] = y[:, o:o + aw].astype(BF16)
    o += aw
    u_ref[...] = y[:, o:o + RNN_WIDTH]
    o += RNN_WIDTH
    gr_ref[...] = y[:, o:o + RNN_WIDTH].astype(BF16)


def _in_even(xc, mods, w, cos, sin, gq, gk, ones_q, ones_k):
    nrows = xc.shape[0]
    nt = nrows // ROWS
    lt = nrows // SUBLANES
    d = D_MODEL
    aw, kw, rw = ATT_WIDTH, KV_WIDTH, RNN_WIDTH
    row = lambda i: (i, 0)
    const = lambda i: (0, 0)
    per_batch = lambda i: (0, i, 0)
    return pl.pallas_call(
        _in_even_kernel,
        grid=(nt,),
        in_specs=[
            pl.BlockSpec((ROWS, d), row),
            _mod_spec(CTX_LEN // ROW_TT),
            pl.BlockSpec(w.shape, const),
            pl.BlockSpec((ROWS, LANES), row),
            pl.BlockSpec((ROWS, LANES), row),
            pl.BlockSpec(gq.shape, const),
            pl.BlockSpec(gk.shape, const),
            pl.BlockSpec(ones_q.shape, const),
            pl.BlockSpec(ones_k.shape, const),
        ],
        out_specs=[
            pl.BlockSpec((SUBLANES, ROW_TT, aw), per_batch),
            pl.BlockSpec((SUBLANES, ROW_TT, kw), per_batch),
            pl.BlockSpec((SUBLANES, ROW_TT, kw), per_batch),
            pl.BlockSpec((ROWS, aw), row),
            pl.BlockSpec((ROWS, rw), row),
            pl.BlockSpec((ROWS, rw), row),
        ],
        out_shape=[
            jax.ShapeDtypeStruct((SUBLANES, lt, aw), BF16),
            jax.ShapeDtypeStruct((SUBLANES, lt, kw), BF16),
            jax.ShapeDtypeStruct((SUBLANES, lt, kw), BF16),
            jax.ShapeDtypeStruct((nrows, aw), BF16),
            jax.ShapeDtypeStruct((nrows, rw), F32),
            jax.ShapeDtypeStruct((nrows, rw), BF16),
        ],
        scratch_shapes=[pltpu.VMEM((aw // LANES, ROWS, LANES), F32), pltpu.VMEM((2, ROWS, LANES), F32)],
        compiler_params=_cparams(("parallel",)),
        name="even_in_proj",
    )(xc, mods, w, cos, sin, gq, gk, ones_q, ones_k)


def _attn_kernel(q_ref, k_ref, v_ref, o_ref, vt_scr, s_scr, acc_scr, *, n_ctx_tiles, n_x_chunks):
    qt = pl.program_id(1)
    lane = lax.broadcasted_iota(jnp.int32, (1, LANES), 1)
    tq = ATT_TQ
    cols = ATT_GROUP * tq
    lt = v_ref.shape[1]
    n = n_x_chunks
    kt = 4 * LANES

    @pl.when(qt == 0)
    def _():
        def fill(i, carry):
            rs = pl.ds(pl.multiple_of(i * CTX_LEN, CTX_LEN), CTX_LEN)
            vv = v_ref[0, rs, :].astype(F32)
            for h in range(ATT_KV_HEADS):
                vt_scr[h, :, rs] = jnp.where((lane // HEAD_DIM) == h, vv, 1.0).T.astype(BF16)
            return carry
        lax.fori_loop(0, lt // CTX_LEN, fill, 0)

    def key_range(c):
        return (CTX_LEN + c * ATT_TK, ATT_TK) if c < n else (0, CTX_LEN)

    for h in range(ATT_KV_HEADS):
        in_head = (lane // HEAD_DIM) == h
        qs = jnp.concatenate(
            [jnp.where(in_head, q_ref[0, :, LANES * g:LANES * (g + 1)], jnp.zeros((), BF16))
             for g in range(ATT_GROUP)], axis=0)
        qst = qs.astype(F32).T.astype(BF16)

        def scores(rng, slot, qst=qst):
            start, width = rng
            s_scr[slot, 0:width, :] = jnp.dot(k_ref[0, start:start + width, :], qst,
                                              preferred_element_type=F32)
            return jnp.max(s_scr[slot, 0:width, :], axis=0, keepdims=True)

        def consume(rng, slot, m, mloc, acc, h=h):
            start, width = rng
            m_new = jnp.maximum(m, mloc)
            acc = acc * jnp.exp2(m - m_new)
            kw = min(kt, width)
            for t in range(width // kw):
                p = jnp.exp2(s_scr[slot, kw * t:kw * (t + 1), :] - m_new).astype(BF16)
                acc = acc + jnp.dot(vt_scr[h, :, start + kw * t:start + kw * (t + 1)], p,
                                    preferred_element_type=F32)
            return m_new, acc

        zero = jnp.zeros((KV_WIDTH, cols), F32)

        @pl.when(qt < n_ctx_tiles)
        def _(h=h, scores=scores, consume=consume):
            mloc = scores((0, CTX_LEN), 0)
            acc_scr[h] = consume((0, CTX_LEN), 0, mloc, mloc, zero)[1]

        @pl.when(qt >= n_ctx_tiles)
        def _(h=h, scores=scores, consume=consume):
            mloc = scores(key_range(0), 0)
            m, acc = mloc, zero
            for i in range(n + 1):
                mloc_next = scores(key_range(i + 1), (i + 1) % 2) if i + 1 <= n else None
                m, acc = consume(key_range(i), i % 2, m, mloc, acc)
                mloc = mloc_next
            acc_scr[h] = acc

    first = lane < HEAD_DIM
    for g in range(ATT_GROUP):
        cs = slice(tq * g, tq * (g + 1))
        outs = []
        for h in range(ATT_KV_HEADS):
            acc = acc_scr[h, :, cs].T
            outs.append(acc / pltpu.roll(acc, HEAD_DIM, axis=1))
        o_ref[0, :, LANES * g:LANES * (g + 1)] = jnp.where(first, outs[0], outs[1])


def _attention(q, k, v):
    nb, lt, _ = q.shape
    nq = lt // ATT_TQ
    cols = ATT_GROUP * ATT_TQ
    kern = functools.partial(_attn_kernel, n_ctx_tiles=CTX_LEN // ATT_TQ,
                             n_x_chunks=(lt - CTX_LEN) // ATT_TK)
    return pl.pallas_call(
        kern,
        grid=(nb, nq),
        in_specs=[
            pl.BlockSpec((1, ATT_TQ, ATT_WIDTH), lambda b, i: (b, i, 0)),
            pl.BlockSpec((1, lt, KV_WIDTH), lambda b, i: (b, 0, 0)),
            pl.BlockSpec((1, lt, KV_WIDTH), lambda b, i: (b, 0, 0)),
        ],
        out_specs=pl.BlockSpec((1, ATT_TQ, ATT_WIDTH), lambda b, i: (b, i, 0)),
        out_shape=jax.ShapeDtypeStruct((nb, lt, ATT_WIDTH), F32),
        scratch_shapes=[
            pltpu.VMEM((ATT_KV_HEADS, KV_WIDTH, lt), BF16),
            pltpu.VMEM((2, ATT_TK, cols), F32),
            pltpu.VMEM((ATT_KV_HEADS, KV_WIDTH, cols), F32),
        ],
        compiler_params=_cparams(("arbitrary", "arbitrary")),
        name="attention",
    )(q, k, v)


def _scan_tile(j, reverse, n_ctx_tiles, n_tiles):
    if not reverse:
        return j
    return jnp.where(j < n_ctx_tiles, n_ctx_tiles - 1 - j, n_tiles + n_ctx_tiles - 1 - j)


def _rnn_kernel(*refs, reverse, n_ctx_tiles, n_tiles, lt):
    if reverse:
        (u_ref, prev_ref, next_ref, cw_ref, cb_ref, wg_ref, bg_ref, clam_ref, yin_ref,
         y_ref, a_scr, b_scr, h_scr) = refs
    else:
        (u_ref, prev_ref, next_ref, cw_ref, cb_ref, wg_ref, bg_ref, clam_ref,
         y_ref, a_scr, b_scr, h_scr) = refs
        yin_ref = None
    j = pl.program_id(0)
    tile = _scan_tile(j, reverse, n_ctx_tiles, n_tiles)
    rows = RNN_TT * SUBLANES
    rw = RNN_WIDTH

    @pl.when(j == 0)
    def _():
        h_scr[...] = jnp.zeros_like(h_scr)

    t0 = tile * RNN_TT
    t1 = t0 + RNN_TT
    has_prev = jnp.logical_and(t0 != 0, t0 != CTX_LEN)
    has_next = jnp.logical_and(t1 != CTX_LEN, t1 != lt)
    u = u_ref[...]
    prev = jnp.where(has_prev, prev_ref[...], 0.0)
    nxt = jnp.where(has_next, next_ref[0:SUBLANES, :], 0.0)
    ext = jnp.concatenate([prev, u, nxt], axis=0)
    cw = cw_ref[...]
    s = SUBLANES
    cv = (cw[0:1] * ext[0:rows] + cw[1:2] * ext[s:rows + s] + cw[2:3] * ext[2 * s:rows + 2 * s]
          + cw[3:4] * ext[3 * s:rows + 3 * s] + cb_ref[...])
    g = jnp.dot(cv.astype(BF16), wg_ref[...], preferred_element_type=F32) + bg_ref[...]
    r = _sigmoid(g[:, 0:rw])
    i = _sigmoid(g[:, rw:2 * rw])
    log_a = clam_ref[...] * r
    a_scr[...] = jnp.exp(log_a)
    th = jnp.tanh(log_a)
    b_scr[...] = jnp.sqrt(-2.0 * th / (1.0 - th)) * (i * cv)

    def step(k, h):
        t = (RNN_TT - 1 - k) if reverse else k
        rs = pl.ds(pl.multiple_of(t * SUBLANES, SUBLANES), SUBLANES)
        h = a_scr[rs, :] * h + b_scr[rs, :]
        if reverse:
            y_ref[rs, :] = h + yin_ref[rs, :]
        else:
            y_ref[rs, :] = h
        return h

    h_scr[...] = lax.fori_loop(0, RNN_TT, step, h_scr[...], unroll=8)


def _rnn_dir(u, conv_w, conv_b, wg, bg, clam, yin, reverse):
    nrows = u.shape[0]
    rows = RNN_TT * SUBLANES
    n_tiles = nrows // rows
    n_ctx = CTX_LEN // RNN_TT
    lt = nrows // SUBLANES
    halo = 2 * SUBLANES
    per = rows // halo
    nhalo = nrows // halo
    tile_of = lambda j: _scan_tile(j, reverse, n_ctx, n_tiles)
    rw = RNN_WIDTH
    const = lambda j: (0, 0)
    in_specs = [
        pl.BlockSpec((rows, rw), lambda j: (tile_of(j), 0)),
        pl.BlockSpec((halo, rw), lambda j: (jnp.maximum(tile_of(j) * per - 1, 0), 0)),
        pl.BlockSpec((halo, rw), lambda j: (jnp.minimum((tile_of(j) + 1) * per, nhalo - 1), 0)),
        pl.BlockSpec((4, rw), const),
        pl.BlockSpec((1, rw), const),
        pl.BlockSpec((rw, 2 * rw), const),
        pl.BlockSpec((1, 2 * rw), const),
        pl.BlockSpec((1, rw), const),
    ]
    args = [u, u, u, conv_w, conv_b, wg, bg, clam]
    if reverse:
        in_specs.append(pl.BlockSpec((rows, rw), lambda j: (tile_of(j), 0)))
        args.append(yin)
    kern = functools.partial(_rnn_kernel, reverse=reverse, n_ctx_tiles=n_ctx, n_tiles=n_tiles, lt=lt)
    return pl.pallas_call(
        kern,
        grid=(n_tiles,),
        in_specs=in_specs,
        out_specs=pl.BlockSpec((rows, rw), lambda j: (tile_of(j), 0)),
        out_shape=jax.ShapeDtypeStruct((nrows, rw), F32),
        scratch_shapes=[pltpu.VMEM((rows, rw), F32), pltpu.VMEM((rows, rw), F32),
                        pltpu.VMEM((SUBLANES, rw), F32)],
        compiler_params=_cparams(("arbitrary",)),
        name="rglru_bwd" if reverse else "rglru_fwd",
    )(*args)


def _out_even_kernel(att_ref, ga_ref, y_ref, gr_ref, x_ref, mod_ref, wa_ref, wr_ref, o_ref, a_scr):
    nblk = ATT_WIDTH // LANES
    for b in range(SUBLANES):
        for j in range(nblk):
            a_scr[j, _batch_rows(b), :] = att_ref[b, :, LANES * j:LANES * (j + 1)]
    att = jnp.concatenate([a_scr[j] for j in range(nblk)], axis=1)
    m1 = (att * _silu(ga_ref[...].astype(F32))).astype(BF16)
    m2 = (y_ref[...] * _silu(gr_ref[...].astype(F32))).astype(BF16)
    o = (jnp.dot(m1, wa_ref[...], preferred_element_type=F32)
         + jnp.dot(m2, wr_ref[...], preferred_element_type=F32))
    o_ref[...] = x_ref[...] + _per_batch(o, mod_ref[0, 2])


def _out_even(att, ga, y, gr, xc, mods, wa, wr):
    nrows = xc.shape[0]
    nt = nrows // ROWS
    d = D_MODEL
    row = lambda i: (i, 0)
    const = lambda i: (0, 0)
    return pl.pallas_call(
        _out_even_kernel,
        grid=(nt,),
        in_specs=[
            pl.BlockSpec((SUBLANES, ROW_TT, ATT_WIDTH), lambda i: (0, i, 0)),
            pl.BlockSpec((ROWS, ATT_WIDTH), row),
            pl.BlockSpec((ROWS, RNN_WIDTH), row),
            pl.BlockSpec((ROWS, RNN_WIDTH), row),
            pl.BlockSpec((ROWS, d), row),
            _mod_spec(CTX_LEN // ROW_TT),
            pl.BlockSpec(wa.shape, const),
            pl.BlockSpec(wr.shape, const),
        ],
        out_specs=pl.BlockSpec((ROWS, d), row),
        out_shape=jax.ShapeDtypeStruct(xc.shape, F32),
        scratch_shapes=[pltpu.VMEM((ATT_WIDTH // LANES, ROWS, LANES), F32)],
        compiler_params=_cparams(("parallel",)),
        name="even_out_proj",
    )(att, ga, y, gr, xc, mods, wa, wr)


def _in_odd_kernel(x_ref, mod_ref, w_ref, u_ref, g_ref):
    h = _norm_mod(x_ref[...], mod_ref).astype(BF16)
    y = jnp.dot(h, w_ref[...], preferred_element_type=F32)
    u_ref[...] = y[:, 0:S5_WIDTH]
    g_ref[...] = y[:, S5_WIDTH:2 * S5_WIDTH].astype(BF16)


def _in_odd(xc, mods, w):
    nrows = xc.shape[0]
    nt = nrows // ROWS
    d = D_MODEL
    row = lambda i: (i, 0)
    return pl.pallas_call(
        _in_odd_kernel,
        grid=(nt,),
        in_specs=[
            pl.BlockSpec((ROWS, d), row),
            _mod_spec(CTX_LEN // ROW_TT),
            pl.BlockSpec(w.shape, lambda i: (0, 0)),
        ],
        out_specs=[pl.BlockSpec((ROWS, S5_WIDTH), row), pl.BlockSpec((ROWS, S5_WIDTH), row)],
        out_shape=[jax.ShapeDtypeStruct((nrows, S5_WIDTH), F32),
                   jax.ShapeDtypeStruct((nrows, S5_WIDTH), BF16)],
        compiler_params=_cparams(("parallel",)),
        name="odd_in_proj",
    )(xc, mods, w)


def _s5_kernel(u_ref, msc_ref, mic_ref, moc_ref, etj_ref, eap_ref, a_ref, extra_ref, y_ref,
               ms_scr, mio_scr, hp_scr, h_scr, *, reverse):
    j = pl.program_id(1)
    nc = S5_TT // S5_TC
    rows = nc * SUBLANES
    sw = S5_SW

    @pl.when(j == 0)
    def _():
        h_scr[...] = jnp.zeros_like(h_scr)
        cw = S5_TC * LANES
        r = lax.broadcasted_iota(jnp.int32, (cw, 1), 0)
        c = lax.broadcasted_iota(jnp.int32, (1, cw), 1)
        row_group_ch = (r // S5_GROUP) % S5_GB
        row_group_st = (r // S5_STATE) % S5_GB
        col_group_ch = (c // S5_GROUP) % S5_GB
        col_group_st = (c // S5_STATE) % S5_GB

        def expand(compact_ref, e_ref, keep):
            full = jnp.dot(compact_ref[0], e_ref[...], preferred_element_type=F32)
            return jnp.where(keep, full, 0.0).astype(BF16)

        ms_scr[...] = expand(msc_ref, eap_ref, row_group_ch == col_group_st)
        mio_scr[0:cw, :] = expand(mic_ref, etj_ref, row_group_ch == col_group_ch)
        mio_scr[cw:2 * cw, :] = expand(moc_ref, etj_ref, row_group_st == col_group_ch)

    def chunk_major(ref):
        return jnp.concatenate(
            [ref[:, SUBLANES * t:SUBLANES * (t + 1), :].reshape(rows, LANES) for t in range(S5_TC)], axis=1)

    lhs32 = chunk_major(u_ref)
    lhs = lhs32.astype(BF16)
    local = jnp.dot(lhs, ms_scr[...], preferred_element_type=F32)
    ar = a_ref[0, :, 0:sw]
    ai = a_ref[0, :, sw:2 * sw]
    hr = h_scr[:, 0:sw]
    hi = h_scr[:, sw:2 * sw]
    for c in (range(nc - 1, -1, -1) if reverse else range(nc)):
        rs = slice(SUBLANES * c, SUBLANES * (c + 1))
        hp_scr[rs, 0:sw] = hr
        hp_scr[rs, sw:2 * sw] = hi
        hr, hi = (ar * hr - ai * hi + local[rs, 0:sw], ar * hi + ai * hr + local[rs, sw:2 * sw])
    h_scr[:, 0:sw] = hr
    h_scr[:, sw:2 * sw] = hi
    cw = S5_TC * LANES
    tile = 2 * LANES
    hp = hp_scr[...].astype(BF16)
    parts = []
    for b in range(cw // tile):
        cs = slice(tile * b, tile * (b + 1))
        ks = slice(tile * b, cw) if reverse else slice(0, tile * (b + 1))
        parts.append(jnp.dot(lhs[:, ks], mio_scr[ks, cs], preferred_element_type=F32)
                     + jnp.dot(hp, mio_scr[cw:2 * cw, cs], preferred_element_type=F32))
    y = jnp.concatenate(parts, axis=1)
    if reverse:
        y = y + chunk_major(extra_ref)
    else:
        y = y + extra_ref[...] * lhs32
    for t in range(S5_TC):
        y_ref[:, SUBLANES * t:SUBLANES * (t + 1), :] = (
            y[:, LANES * t:LANES * (t + 1)].reshape(nc, SUBLANES, LANES))


def _s5_dir(u3, msc, mic, moc, etj, eap, a, extra, reverse):
    nchunks, crow, width = u3.shape
    nc = S5_TT // S5_TC
    n_tiles = nchunks // nc
    n_ctx = CTX_LEN // S5_TT
    tile_of = lambda j: _scan_tile(j, reverse, n_ctx, n_tiles)
    blk = lambda k, j: (tile_of(j), 0, k)
    wblk = lambda k, j: (k, 0, 0)
    cw = S5_TC * LANES
    if reverse:
        extra_spec = pl.BlockSpec((nc, crow, LANES), blk)
    else:
        extra_spec = pl.BlockSpec((1, cw), lambda k, j: (0, k))
    return pl.pallas_call(
        functools.partial(_s5_kernel, reverse=reverse),
        grid=(S5_NB, n_tiles),
        in_specs=[
            pl.BlockSpec((nc, crow, LANES), blk),
            pl.BlockSpec((1, cw, LANES), wblk),
            pl.BlockSpec((1, cw, LANES), wblk),
            pl.BlockSpec((1, cw, LANES), wblk),
            pl.BlockSpec(etj.shape, lambda k, j: (0, 0)),
            pl.BlockSpec(eap.shape, lambda k, j: (0, 0)),
            pl.BlockSpec((1, SUBLANES, 2 * S5_SW), wblk),
            extra_spec,
        ],
        out_specs=pl.BlockSpec((nc, crow, LANES), blk),
        out_shape=jax.ShapeDtypeStruct(u3.shape, F32),
        scratch_shapes=[pltpu.VMEM((cw, 2 * S5_SW), BF16), pltpu.VMEM((cw + 2 * S5_SW, cw), BF16),
                        pltpu.VMEM((nc * SUBLANES, 2 * S5_SW), F32), pltpu.VMEM((SUBLANES, 2 * S5_SW), F32)],
        compiler_params=_cparams(("arbitrary", "arbitrary")),
        name="s5_bwd" if reverse else "s5_fwd",
    )(u3, msc, mic, moc, etj, eap, a, extra)


def _out_odd_kernel(y_ref, g_ref, x_ref, mod_ref, gw_ref, gb_ref, wo_ref, fw_ref, o_ref, *scr, final):
    yy = jax.nn.gelu(y_ref[...]).astype(BF16)
    z = jnp.dot(yy, gw_ref[...], preferred_element_type=F32) + gb_ref[...]
    g = g_ref[...].astype(F32)
    m = z[:, 0:S5_WIDTH] * _sigmoid(z[:, S5_WIDTH:2 * S5_WIDTH]) * _silu(g)
    o = jnp.dot(m.astype(BF16), wo_ref[...], preferred_element_type=F32)
    xn = x_ref[...] + _per_batch(o, mod_ref[0, 2])
    if final:
        (o_scr,) = scr
        ms = jnp.mean(xn * xn, axis=-1, keepdims=True)
        xo = xn * lax.rsqrt(ms + EPS) * fw_ref[...]
        nblk = D_MODEL // LANES
        for j in range(nblk):
            o_scr[j] = xo[:, LANES * j:LANES * (j + 1)]
        for b in range(SUBLANES):
            for j in range(nblk):
                o_ref[b, :, LANES * j:LANES * (j + 1)] = o_scr[j, _batch_rows(b), :]
    else:
        o_ref[...] = xn


def _out_odd(y, g, xc, mods, gw, gb, wo, fw, final):
    nrows = xc.shape[0]
    d = D_MODEL
    nct = CTX_LEN // ROW_TT
    lt = nrows // SUBLANES
    if final:
        nt = (lt - CTX_LEN) // ROW_TT
        row = lambda i: (i + nct, 0)
        mod_spec = _mod_spec(nct, offset=nct)
        out_spec = pl.BlockSpec((SUBLANES, ROW_TT, d), lambda i: (0, i, 0))
        out_shape = jax.ShapeDtypeStruct((SUBLANES, lt - CTX_LEN, d), F32)
        scratch = [pltpu.VMEM((d // LANES, ROWS, LANES), F32)]
    else:
        nt = lt // ROW_TT
        row = lambda i: (i, 0)
        mod_spec = _mod_spec(nct)
        out_spec = pl.BlockSpec((ROWS, d), row)
        out_shape = jax.ShapeDtypeStruct(xc.shape, F32)
        scratch = []
    const = lambda i: (0, 0)
    return pl.pallas_call(
        functools.partial(_out_odd_kernel, final=final),
        grid=(nt,),
        in_specs=[
            pl.BlockSpec((ROWS, S5_WIDTH), row),
            pl.BlockSpec((ROWS, S5_WIDTH), row),
            pl.BlockSpec((ROWS, d), row),
            mod_spec,
            pl.BlockSpec(gw.shape, const),
            pl.BlockSpec(gb.shape, const),
            pl.BlockSpec(wo.shape, const),
            pl.BlockSpec(fw.shape, const),
        ],
        out_specs=out_spec,
        out_shape=out_shape,
        scratch_shapes=scratch,
        compiler_params=_cparams(("parallel",)),
        name="odd_out_proj_final" if final else "odd_out_proj",
    )(y, g, xc, mods, gw, gb, wo, fw)


def _rope_tables(lt):
    n = lt - CTX_LEN
    rows = n // GRID_W
    row = jnp.repeat(jnp.arange(rows, dtype=F32), GRID_W)
    col = jnp.tile(jnp.arange(GRID_W, dtype=F32), rows)
    n_freq = HEAD_DIM // 4
    inv = ROPE_THETA ** (-jnp.arange(n_freq, dtype=F32) / n_freq)
    ang = jnp.concatenate([row[:, None] * inv, col[:, None] * inv], axis=-1)
    half = HEAD_DIM // 2
    cos = jnp.concatenate([jnp.ones((CTX_LEN, half), F32), jnp.cos(ang)], axis=0)
    sin = jnp.concatenate([jnp.zeros((CTX_LEN, half), F32), jnp.sin(ang)], axis=0)
    reps = LANES // HEAD_DIM
    ct = jnp.tile(jnp.concatenate([cos, cos], axis=-1), (1, reps))
    st = jnp.tile(jnp.concatenate([-sin, sin], axis=-1), (1, reps))
    ct, st = lax.optimization_barrier((ct, st))
    return jnp.repeat(ct, SUBLANES, axis=0), jnp.repeat(st, SUBLANES, axis=0)


def _rope_gains(gain, scale):
    g_eo = jnp.concatenate([gain[0::2], gain[1::2]])
    g_oe = jnp.concatenate([gain[1::2], gain[0::2]])
    reps = LANES // HEAD_DIM
    return jnp.stack([jnp.tile(g_eo, reps), jnp.tile(g_oe, reps)]).astype(F32) * scale


def _even_weights(w_in, w_out):
    aw, kw, rw = ATT_WIDTH, KV_WIDTH, RNN_WIDTH
    eo = jnp.concatenate([jnp.arange(0, HEAD_DIM, 2), jnp.arange(1, HEAD_DIM, 2)])
    oe = jnp.concatenate([jnp.arange(1, HEAD_DIM, 2), jnp.arange(0, HEAD_DIM, 2)])
    slot_head = jnp.array([h * ATT_GROUP + g for g in range(ATT_GROUP) for h in range(ATT_KV_HEADS)])
    q_cols = (slot_head[:, None] * HEAD_DIM + eo[None, :]).reshape(-1)
    q_cols_sw = (slot_head[:, None] * HEAD_DIM + oe[None, :]).reshape(-1)
    kv_heads = jnp.arange(ATT_KV_HEADS)
    k_cols = aw + (kv_heads[:, None] * HEAD_DIM + eo[None, :]).reshape(-1)
    k_cols_sw = aw + (kv_heads[:, None] * HEAD_DIM + oe[None, :]).reshape(-1)
    v_cols = aw + kw + jnp.arange(kw)
    slot_cols = (slot_head[:, None] * HEAD_DIM + jnp.arange(HEAD_DIM)[None, :]).reshape(-1)
    ga_cols = aw + 2 * kw + slot_cols
    u_cols = 2 * aw + 2 * kw + jnp.arange(rw)
    gr_cols = 2 * aw + 2 * kw + rw + jnp.arange(rw)
    cols = jnp.concatenate([q_cols, q_cols_sw, k_cols, k_cols_sw, v_cols, ga_cols, u_cols, gr_cols])
    w = w_in[:, cols].astype(BF16)
    wa = w_out[:aw][slot_cols].astype(BF16)
    wr = w_out[aw:].astype(BF16)
    return w, wa, wr


def _block_ones(width):
    idx = jnp.arange(width) // HEAD_DIM
    return (idx[:, None] == idx[None, :]).astype(BF16)


def _rnn_gate_weights(wa, ba, wx, bx):
    eye = jnp.eye(RNN_BLOCKS, dtype=F32)

    def dense(w):
        return jnp.einsum('hij,hk->hikj', w, eye).reshape(RNN_WIDTH, RNN_WIDTH)

    wg = jnp.concatenate([dense(wa), dense(wx)], axis=1).astype(BF16)
    bg = jnp.concatenate([ba.reshape(-1), bx.reshape(-1)])[None, :]
    return wg, bg


def _s5_discretize(lam_re, lam_im, log_step, b_re, b_im):
    dt = jnp.exp(log_step)[:, None]
    mag = jnp.exp(lam_re * dt)
    ab_re = mag * jnp.cos(lam_im * dt)
    ab_im = mag * jnp.sin(lam_im * dt)
    den = lam_re * lam_re + lam_im * lam_im
    nr, ni = ab_re - 1.0, ab_im
    f_re = (nr * lam_re + ni * lam_im) / den
    f_im = (ni * lam_re - nr * lam_im) / den
    bb_re = f_re[..., None] * b_re - f_im[..., None] * b_im
    bb_im = f_re[..., None] * b_im + f_im[..., None] * b_re
    return ab_re, ab_im, bb_re, bb_im


def _s5_weights(lam_re, lam_im, log_step, b_re, b_im, c_re, c_im, reverse):
    hp = lax.Precision.HIGHEST
    f = lambda t: t.astype(F32)
    lam_re, lam_im, log_step, b_re, b_im, c_re, c_im = map(f, (lam_re, lam_im, log_step, b_re, b_im, c_re, c_im))
    tc, nbk, gb, p, c = S5_TC, S5_NB, S5_GB, S5_STATE, S5_GROUP
    _, _, bb_re, bb_im = _s5_discretize(lam_re, lam_im, log_step, b_re, b_im)
    dt = jnp.exp(log_step)[:, None]
    e = jnp.arange(tc + 1, dtype=F32)[:, None, None]
    mag = jnp.exp(e * (lam_re * dt))
    pw_re = mag * jnp.cos(e * (lam_im * dt))
    pw_im = mag * jnp.sin(e * (lam_im * dt))
    ca_re = c_re[None] * pw_re[:, :, None, :] - c_im[None] * pw_im[:, :, None, :]
    ca_im = c_re[None] * pw_im[:, :, None, :] + c_im[None] * pw_re[:, :, None, :]
    kern = (jnp.einsum('tgjp,gpi->tgji', ca_re[:tc], bb_re, precision=hp)
            - jnp.einsum('tgjp,gpi->tgji', ca_im[:tc], bb_im, precision=hp))
    steps = jnp.arange(tc)
    lag = (steps[:, None] - steps[None, :]) if reverse else (steps[None, :] - steps[:, None])
    kt = jnp.where((lag >= 0)[:, :, None, None, None], kern[jnp.clip(lag, 0, tc - 1)], 0.0)
    kt = kt.reshape(tc, tc, nbk, gb, c, c)
    mic = kt.transpose(2, 0, 3, 5, 1, 4).reshape(nbk, tc * gb * c, tc * c)
    e_s = steps if reverse else (tc - 1 - steps)
    ab_re = pw_re[e_s][..., None] * bb_re[None] - pw_im[e_s][..., None] * bb_im[None]
    ab_im = pw_re[e_s][..., None] * bb_im[None] + pw_im[e_s][..., None] * bb_re[None]
    ab = jnp.stack([ab_re, ab_im], 0).reshape(2, tc, nbk, gb, p, c)
    msc = ab.transpose(2, 1, 3, 5, 0, 4).reshape(nbk, tc * gb * c, 2 * p)
    e_t = (tc - steps) if reverse else (steps + 1)
    co = jnp.stack([ca_re[e_t], -ca_im[e_t]], 0).reshape(2, tc, nbk, gb, c, p)
    moc = co.transpose(2, 0, 3, 5, 1, 4).reshape(nbk, 2 * gb * p, tc * c)
    a = jnp.stack([pw_re[tc], pw_im[tc]], 0).reshape(2, nbk, gb, p).transpose(1, 0, 2, 3).reshape(nbk, 1, 2 * gb * p)
    a = jnp.broadcast_to(a, (nbk, SUBLANES, 2 * gb * p))
    return msc.astype(BF16), mic.astype(BF16), moc.astype(BF16), a


def _s5_expanders():
    cw = S5_TC * LANES
    col = jnp.arange(cw)
    src_tj = (col // LANES) * S5_GROUP + col % S5_GROUP
    src_ap = (col // S5_SW) * S5_STATE + col % S5_STATE
    etj = (jnp.arange(S5_TC * S5_GROUP)[:, None] == src_tj[None, :]).astype(BF16)
    eap = (jnp.arange(2 * S5_STATE)[:, None] == src_ap[None, :]).astype(BF16)
    return etj, eap


def kernel(x, c, ctx, c_ctx, ada_w, ada_b, ev_w_in, ev_w_out, q_norm_w, k_norm_w, rg_conv_w, rg_conv_b, rg_wa, rg_ba, rg_wx, rg_bx, rg_lambda, od_w_in, s5_lambda_re, s5_lambda_im, s5_log_step, s5_b_re, s5_b_im, s5_c_re, s5_c_im, s5_d, glu_w, glu_b, od_w_out, final_norm_w):
    nb, seq, d = x.shape
    assert nb == SUBLANES and d == D_MODEL and ctx.shape[1] == CTX_LEN
    assert seq % ATT_TK == 0
    lt = CTX_LEN + seq

    xc = _to_rows(ctx, x)

    cond = jnp.zeros((16, d), F32).at[:nb].set(c).at[nb].set(c_ctx)
    mods = _modulation(cond, ada_w, ada_b).reshape(DEPTH, 16, 3, d)
    mods_x = mods[:, :nb].transpose(0, 2, 1, 3)
    mods_c = jnp.broadcast_to(mods[:, nb][:, :, None, :], mods_x.shape)
    mods = jnp.stack([mods_c, mods_x], axis=1)

    cos, sin = _rope_tables(lt)
    ones_q = _block_ones(ATT_WIDTH)
    ones_k = _block_ones(KV_WIDTH)
    etj, eap = _s5_expanders()
    fw = final_norm_w.reshape(1, d)
    out = None
    for layer in range(DEPTH):
        j = layer // 2
        ml = mods[layer]
        if layer % 2 == 0:
            w, wa, wr = _even_weights(ev_w_in[j], ev_w_out[j])
            gq = _rope_gains(q_norm_w[j], HEAD_DIM ** -0.5 * math.log2(math.e))
            gk = _rope_gains(k_norm_w[j], 1.0)
            q, k, v, ga, u, gr = _in_even(xc, ml, w, cos, sin, gq, gk, ones_q, ones_k)
            att = _attention(q, k, v)
            conv_b = rg_conv_b[j].reshape(1, RNN_WIDTH)
            y = None
            for direction, reverse in enumerate((False, True)):
                wg, bg = _rnn_gate_weights(rg_wa[j, direction], rg_ba[j, direction],
                                           rg_wx[j, direction], rg_bx[j, direction])
                clam = (-LRU_C * jax.nn.softplus(-rg_lambda[j, direction].astype(F32))).reshape(1, RNN_WIDTH)
                y = _rnn_dir(u, rg_conv_w[j], conv_b, wg, bg, clam, y, reverse)
            xc = _out_even(att, ga, y, gr, xc, ml, wa, wr)
        else:
            u, g = _in_odd(xc, ml, od_w_in[j].astype(BF16))
            u3 = u.reshape(lt // S5_TC, S5_TC * nb, S5_WIDTH)
            y = jnp.tile(s5_d[j].astype(F32).reshape(S5_NB, 1, LANES), (1, S5_TC, 1)).reshape(1, -1)
            for direction, reverse in enumerate((False, True)):
                msc, mic, moc, a = _s5_weights(s5_lambda_re[j, direction], s5_lambda_im[j, direction],
                                               s5_log_step[j, direction], s5_b_re[j, direction],
                                               s5_b_im[j, direction], s5_c_re[j, direction],
                                               s5_c_im[j, direction], reverse)
                y = _s5_dir(u3, msc, mic, moc, etj, eap, a, y, reverse)
            y = y.reshape(lt * nb, S5_WIDTH)
            final = layer == DEPTH - 1
            res = _out_odd(y, g, xc, ml, glu_w[j].astype(BF16), glu_b[j].reshape(1, -1),
                           od_w_out[j].astype(BF16), fw, final)
            if final:
                out = res
            else:
                xc = res
    return out
```

```python
import functools
import math

import jax
import jax.numpy as jnp
from jax import lax
from jax.experimental import pallas as pl
from jax.experimental.pallas import tpu as pltpu

F32 = jnp.float32
BF16 = jnp.bfloat16

D_MODEL = 1024
DEPTH = 4
CTX_LEN = 256
GRID_W = 64
EPS = 1e-6

ATT_HEADS = 8
ATT_KV_HEADS = 2
HEAD_DIM = 64
ATT_GROUP = ATT_HEADS // ATT_KV_HEADS
ATT_WIDTH = ATT_HEADS * HEAD_DIM
KV_WIDTH = ATT_KV_HEADS * HEAD_DIM
ROPE_THETA = 10000.0

RNN_WIDTH = D_MODEL // 2
RNN_BLOCKS = 8
RNN_BLOCK_DIM = RNN_WIDTH // RNN_BLOCKS
LRU_C = 8.0

S5_WIDTH = D_MODEL
S5_GROUP = 16
S5_GROUPS = S5_WIDTH // S5_GROUP
S5_STATE = 64

SUBLANES = 8
LANES = 128
ROW_TT = 64
ROWS = ROW_TT * SUBLANES
ATT_TQ = 256
ATT_TK = 2048
ATT_VROWS = HEAD_DIM + 16
RNN_TT = 64
S5_TT = 256
S5_TC = 8
S5_GB = LANES // S5_GROUP
S5_NB = S5_GROUPS // S5_GB
S5_SW = S5_GB * S5_STATE
VMEM_LIMIT = 56 * 1024 * 1024


def _cparams(sem):
    return pltpu.CompilerParams(dimension_semantics=sem, vmem_limit_bytes=VMEM_LIMIT)


def _sigmoid(x):
    return 0.5 * jnp.tanh(0.5 * x) + 0.5


def _silu(x):
    return x * _sigmoid(x)


def _batch_rows(b):
    return pl.ds(b, ROW_TT, stride=SUBLANES)


def _mod_kernel(c_ref, w_ref, b_ref, o_ref):
    c = c_ref[...]
    sc = _silu(c).astype(BF16)
    o_ref[0] = jnp.dot(sc, w_ref[0].astype(BF16), preferred_element_type=F32) + b_ref[0]


def _modulation(cond, ada_w, ada_b):
    depth, d, d3 = ada_w.shape
    nblk = d3 // d
    return pl.pallas_call(
        _mod_kernel,
        grid=(depth, nblk),
        in_specs=[
            pl.BlockSpec((16, d), lambda l, n: (0, 0)),
            pl.BlockSpec((1, d, d), lambda l, n: (l, 0, n)),
            pl.BlockSpec((1, 1, d), lambda l, n: (l, 0, n)),
        ],
        out_specs=pl.BlockSpec((1, 16, d), lambda l, n: (l, 0, n)),
        out_shape=jax.ShapeDtypeStruct((depth, 16, d3), F32),
        compiler_params=_cparams(("parallel", "parallel")),
        name="ada_mod",
    )(cond, ada_w, ada_b.reshape(depth, 1, d3))


def _to_rows_kernel(c_ref, x_ref, o_ref, scr, *, n_ctx_tiles):
    i = pl.program_id(0)
    nblk = D_MODEL // LANES

    def move(src_ref):
        for b in range(SUBLANES):
            for j in range(nblk):
                scr[j, _batch_rows(b), :] = src_ref[b, :, LANES * j:LANES * (j + 1)]
        for j in range(nblk):
            o_ref[:, LANES * j:LANES * (j + 1)] = scr[j]

    @pl.when(i < n_ctx_tiles)
    def _():
        move(c_ref)

    @pl.when(i >= n_ctx_tiles)
    def _():
        move(x_ref)


def _to_rows(ctx, x):
    nb, seq, d = x.shape
    nct = ctx.shape[1] // ROW_TT
    nt = nct + seq // ROW_TT
    return pl.pallas_call(
        functools.partial(_to_rows_kernel, n_ctx_tiles=nct),
        grid=(nt,),
        in_specs=[
            pl.BlockSpec((nb, ROW_TT, d), lambda i: (0, jnp.minimum(i, nct - 1), 0)),
            pl.BlockSpec((nb, ROW_TT, d), lambda i: (0, jnp.maximum(i - nct, 0), 0)),
        ],
        out_specs=pl.BlockSpec((ROWS, d), lambda i: (i, 0)),
        out_shape=jax.ShapeDtypeStruct((nt * ROWS, d), F32),
        scratch_shapes=[pltpu.VMEM((d // LANES, ROWS, LANES), F32)],
        compiler_params=_cparams(("parallel",)),
        name="to_rows",
    )(ctx, x)


def _per_batch(x, vec):
    r, d = x.shape
    return (x.reshape(r // SUBLANES, SUBLANES, d) * vec[None]).reshape(r, d)


def _norm_mod(x, mod_ref):
    ms = jnp.mean(x * x, axis=-1, keepdims=True)
    xn = x * lax.rsqrt(ms + EPS)
    r, d = x.shape
    x3 = xn.reshape(r // SUBLANES, SUBLANES, d)
    return (x3 * (1.0 + mod_ref[0, 1])[None] + mod_ref[0, 0][None]).reshape(r, d)


def _mod_spec(n_ctx_tiles, offset=0):
    return pl.BlockSpec((1, 3, SUBLANES, D_MODEL),
                        lambda i: (jnp.where(i + offset < n_ctx_tiles, 0, 1), 0, 0, 0))


def _group_mean_sq(y, ones_ref):
    sq = y * y
    hi = sq.astype(BF16)
    lo = (sq - hi.astype(F32)).astype(BF16)
    ones = ones_ref[...]
    s = jnp.dot(hi, ones, preferred_element_type=F32) + jnp.dot(lo, ones, preferred_element_type=F32)
    return s * (1.0 / HEAD_DIM)


def _in_even_kernel(x_ref, mod_ref, w_ref, cos_ref, sin_ref, gq_ref, gk_ref, oq_ref, ok_ref,
                    q_ref, k_ref, v_ref, ga_ref, u_ref, gr_ref, q_scr, kv_scr):
    h = _norm_mod(x_ref[...], mod_ref).astype(BF16)
    y = jnp.dot(h, w_ref[...], preferred_element_type=F32)
    aw, kw = ATT_WIDTH, KV_WIDTH
    cos = cos_ref[...]
    sin = sin_ref[...]
    qp = y[:, 0:aw]
    qs = y[:, aw:2 * aw]
    rq = lax.rsqrt(_group_mean_sq(qp, oq_ref) + EPS)
    ca = gq_ref[0:1, :] * cos
    sa = gq_ref[1:2, :] * sin
    for j in range(aw // LANES):
        sl = slice(LANES * j, LANES * (j + 1))
        q_scr[j] = rq[:, sl] * (qp[:, sl] * ca + qs[:, sl] * sa)
    o = 2 * aw
    kp = y[:, o:o + kw]
    ks = y[:, o + kw:o + 2 * kw]
    rk = lax.rsqrt(_group_mean_sq(kp, ok_ref) + EPS)
    kv_scr[0] = rk * (kp * (gk_ref[0:1, :] * cos) + ks * (gk_ref[1:2, :] * sin))
    o += 2 * kw
    kv_scr[1] = y[:, o:o + kw]
    o += kw
    for b in range(SUBLANES):
        for j in range(aw // LANES):
            q_ref[b, :, LANES * j:LANES * (j + 1)] = q_scr[j, _batch_rows(b), :].astype(BF16)
        k_ref[b] = kv_scr[0, _batch_rows(b), :].astype(BF16)
        v_ref[b] = kv_scr[1, _batch_rows(b), :].astype(BF16)
    ga_ref[...] = y[:, o:o + aw].astype(BF16)
    o += aw
    u_ref[...] = y[:, o:o + RNN_WIDTH]
    o += RNN_WIDTH
    gr_ref[...] = y[:, o:o + RNN_WIDTH].astype(BF16)


def _in_even(xc, mods, w, cos, sin, gq, gk, ones_q, ones_k):
    nrows = xc.shape[0]
    nt = nrows // ROWS
    lt = nrows // SUBLANES
    d = D_MODEL
    aw, kw, rw = ATT_WIDTH, KV_WIDTH, RNN_WIDTH
    row = lambda i: (i, 0)
    const = lambda i: (0, 0)
    per_batch = lambda i: (0, i, 0)
    return pl.pallas_call(
        _in_even_kernel,
        grid=(nt,),
        in_specs=[
            pl.BlockSpec((ROWS, d), row),
            _mod_spec(CTX_LEN // ROW_TT),
            pl.BlockSpec(w.shape, const),
            pl.BlockSpec((ROWS, LANES), row),
            pl.BlockSpec((ROWS, LANES), row),
            pl.BlockSpec(gq.shape, const),
            pl.BlockSpec(gk.shape, const),
            pl.BlockSpec(ones_q.shape, const),
            pl.BlockSpec(ones_k.shape, const),
        ],
        out_specs=[
            pl.BlockSpec((SUBLANES, ROW_TT, aw), per_batch),
            pl.BlockSpec((SUBLANES, ROW_TT, kw), per_batch),
            pl.BlockSpec((SUBLANES, ROW_TT, kw), per_batch),
            pl.BlockSpec((ROWS, aw), row),
            pl.BlockSpec((ROWS, rw), row),
            pl.BlockSpec((ROWS, rw), row),
        ],
        out_shape=[
            jax.ShapeDtypeStruct((SUBLANES, lt, aw), BF16),
            jax.ShapeDtypeStruct((SUBLANES, lt, kw), BF16),
            jax.ShapeDtypeStruct((SUBLANES, lt, kw), BF16),
            jax.ShapeDtypeStruct((nrows, aw), BF16),
            jax.ShapeDtypeStruct((nrows, rw), F32),
            jax.ShapeDtypeStruct((nrows, rw), BF16),
        ],
        scratch_shapes=[pltpu.VMEM((aw // LANES, ROWS, LANES), F32), pltpu.VMEM((2, ROWS, LANES), F32)],
        compiler_params=_cparams(("parallel",)),
        name="even_in_proj",
    )(xc, mods, w, cos, sin, gq, gk, ones_q, ones_k)


def _attn_kernel(q_ref, k_ref, v_ref, o_ref, vt_scr, s_scr, acc_scr, *, n_ctx_tiles, n_x_chunks):
    qt = pl.program_id(1)
    lane = lax.broadcasted_iota(jnp.int32, (1, LANES), 1)
    tq = ATT_TQ
    cols = ATT_GROUP * tq
    lt = v_ref.shape[1]
    n = n_x_chunks
    kt = 4 * LANES

    @pl.when(qt == 0)
    def _():
        def fill(i, carry):
            rs = pl.ds(pl.multiple_of(i * CTX_LEN, CTX_LEN), CTX_LEN)
            vt = v_ref[0, rs, :].astype(F32).T
            for h in range(ATT_KV_HEADS):
                vt_scr[h, 0:HEAD_DIM, rs] = vt[HEAD_DIM * h:HEAD_DIM * (h + 1)].astype(BF16)
                vt_scr[h, HEAD_DIM:ATT_VROWS, rs] = jnp.ones((ATT_VROWS - HEAD_DIM, CTX_LEN), BF16)
            return carry
        lax.fori_loop(0, lt // CTX_LEN, fill, 0)

    def key_range(c):
        return (CTX_LEN + c * ATT_TK, ATT_TK) if c < n else (0, CTX_LEN)

    for h in range(ATT_KV_HEADS):
        in_head = (lane // HEAD_DIM) == h
        qs = jnp.concatenate(
            [jnp.where(in_head, q_ref[0, :, LANES * g:LANES * (g + 1)], jnp.zeros((), BF16))
             for g in range(ATT_GROUP)], axis=0)
        qst = qs.astype(F32).T.astype(BF16)

        def scores(rng, slot, qst=qst):
            start, width = rng
            s_scr[slot, 0:width, :] = jnp.dot(k_ref[0, start:start + width, :], qst,
                                              preferred_element_type=F32)
            return jnp.max(s_scr[slot, 0:width, :], axis=0, keepdims=True)

        def consume(rng, slot, m, mloc, acc, h=h):
            start, width = rng
            m_new = jnp.maximum(m, mloc)
            acc = acc * jnp.exp2(m - m_new)
            kw = min(kt, width)
            for t in range(width // kw):
                p = jnp.exp2(s_scr[slot, kw * t:kw * (t + 1), :] - m_new).astype(BF16)
                acc = acc + jnp.dot(vt_scr[h, :, start + kw * t:start + kw * (t + 1)], p,
                                    preferred_element_type=F32)
            return m_new, acc

        zero = jnp.zeros((ATT_VROWS, cols), F32)

        @pl.when(qt < n_ctx_tiles)
        def _(h=h, scores=scores, consume=consume):
            mloc = scores((0, CTX_LEN), 0)
            acc_scr[h] = consume((0, CTX_LEN), 0, mloc, mloc, zero)[1]

        @pl.when(qt >= n_ctx_tiles)
        def _(h=h, scores=scores, consume=consume):
            mloc = scores(key_range(0), 0)
            m, acc = mloc, zero
            for i in range(n + 1):
                mloc_next = scores(key_range(i + 1), (i + 1) % 2) if i + 1 <= n else None
                m, acc = consume(key_range(i), i % 2, m, mloc, acc)
                mloc = mloc_next
            acc_scr[h] = acc

    first = lane < HEAD_DIM
    pad = jnp.zeros((LANES - ATT_VROWS, tq), F32)
    for g in range(ATT_GROUP):
        cs = slice(tq * g, tq * (g + 1))
        outs = []
        for h in range(ATT_KV_HEADS):
            acc = jnp.concatenate([acc_scr[h, :, cs], pad], axis=0).T
            outs.append(acc / acc[:, HEAD_DIM:HEAD_DIM + 1])
        o_ref[0, :, LANES * g:LANES * (g + 1)] = jnp.where(
            first, outs[0], pltpu.roll(outs[1], HEAD_DIM, axis=1))


def _attention(q, k, v):
    nb, lt, _ = q.shape
    nq = lt // ATT_TQ
    cols = ATT_GROUP * ATT_TQ
    kern = functools.partial(_attn_kernel, n_ctx_tiles=CTX_LEN // ATT_TQ,
                             n_x_chunks=(lt - CTX_LEN) // ATT_TK)
    return pl.pallas_call(
        kern,
        grid=(nb, nq),
        in_specs=[
            pl.BlockSpec((1, ATT_TQ, ATT_WIDTH), lambda b, i: (b, i, 0)),
            pl.BlockSpec((1, lt, KV_WIDTH), lambda b, i: (b, 0, 0)),
            pl.BlockSpec((1, lt, KV_WIDTH), lambda b, i: (b, 0, 0)),
        ],
        out_specs=pl.BlockSpec((1, ATT_TQ, ATT_WIDTH), lambda b, i: (b, i, 0)),
        out_shape=jax.ShapeDtypeStruct((nb, lt, ATT_WIDTH), F32),
        scratch_shapes=[
            pltpu.VMEM((ATT_KV_HEADS, ATT_VROWS, lt), BF16),
            pltpu.VMEM((2, ATT_TK, cols), F32),
            pltpu.VMEM((ATT_KV_HEADS, ATT_VROWS, cols), F32),
        ],
        compiler_params=_cparams(("arbitrary", "arbitrary")),
        name="attention",
    )(q, k, v)


def _scan_tile(j, reverse, n_ctx_tiles, n_tiles):
    if not reverse:
        return j
    return jnp.where(j < n_ctx_tiles, n_ctx_tiles - 1 - j, n_tiles + n_ctx_tiles - 1 - j)


def _rnn_kernel(*refs, reverse, n_ctx_tiles, n_tiles, lt):
    if reverse:
        (u_ref, prev_ref, next_ref, cw_ref, cb_ref, wg_ref, bg_ref, clam_ref, yin_ref,
         y_ref, a_scr, b_scr, h_scr) = refs
    else:
        (u_ref, prev_ref, next_ref, cw_ref, cb_ref, wg_ref, bg_ref, clam_ref,
         y_ref, a_scr, b_scr, h_scr) = refs
        yin_ref = None
    j = pl.program_id(0)
    tile = _scan_tile(j, reverse, n_ctx_tiles, n_tiles)
    rows = RNN_TT * SUBLANES
    rw = RNN_WIDTH

    @pl.when(j == 0)
    def _():
        h_scr[...] = jnp.zeros_like(h_scr)

    t0 = tile * RNN_TT
    t1 = t0 + RNN_TT
    has_prev = jnp.logical_and(t0 != 0, t0 != CTX_LEN)
    has_next = jnp.logical_and(t1 != CTX_LEN, t1 != lt)
    u = u_ref[...]
    prev = jnp.where(has_prev, prev_ref[...], 0.0)
    nxt = jnp.where(has_next, next_ref[0:SUBLANES, :], 0.0)
    ext = jnp.concatenate([prev, u, nxt], axis=0)
    cw = cw_ref[...]
    s = SUBLANES
    cv = (cw[0:1] * ext[0:rows] + cw[1:2] * ext[s:rows + s] + cw[2:3] * ext[2 * s:rows + 2 * s]
          + cw[3:4] * ext[3 * s:rows + 3 * s] + cb_ref[...])
    g = jnp.dot(cv.astype(BF16), wg_ref[...], preferred_element_type=F32) + bg_ref[...]
    r = _sigmoid(g[:, 0:rw])
    i = _sigmoid(g[:, rw:2 * rw])
    log_a = clam_ref[...] * r
    a_scr[...] = jnp.exp(log_a)
    th = jnp.tanh(log_a)
    b_scr[...] = jnp.sqrt(-2.0 * th / (1.0 - th)) * (i * cv)

    def step(k, h):
        t = (RNN_TT - 1 - k) if reverse else k
        rs = pl.ds(pl.multiple_of(t * SUBLANES, SUBLANES), SUBLANES)
        h = a_scr[rs, :] * h + b_scr[rs, :]
        if reverse:
            y_ref[rs, :] = h + yin_ref[rs, :]
        else:
            y_ref[rs, :] = h
        return h

    h_scr[...] = lax.fori_loop(0, RNN_TT, step, h_scr[...], unroll=8)


def _rnn_dir(u, conv_w, conv_b, wg, bg, clam, yin, reverse):
    nrows = u.shape[0]
    rows = RNN_TT * SUBLANES
    n_tiles = nrows // rows
    n_ctx = CTX_LEN // RNN_TT
    lt = nrows // SUBLANES
    halo = 2 * SUBLANES
    per = rows // halo
    nhalo = nrows // halo
    tile_of = lambda j: _scan_tile(j, reverse, n_ctx, n_tiles)
    rw = RNN_WIDTH
    const = lambda j: (0, 0)
    in_specs = [
        pl.BlockSpec((rows, rw), lambda j: (tile_of(j), 0)),
        pl.BlockSpec((halo, rw), lambda j: (jnp.maximum(tile_of(j) * per - 1, 0), 0)),
        pl.BlockSpec((halo, rw), lambda j: (jnp.minimum((tile_of(j) + 1) * per, nhalo - 1), 0)),
        pl.BlockSpec((4, rw), const),
        pl.BlockSpec((1, rw), const),
        pl.BlockSpec((rw, 2 * rw), const),
        pl.BlockSpec((1, 2 * rw), const),
        pl.BlockSpec((1, rw), const),
    ]
    args = [u, u, u, conv_w, conv_b, wg, bg, clam]
    if reverse:
        in_specs.append(pl.BlockSpec((rows, rw), lambda j: (tile_of(j), 0)))
        args.append(yin)
    kern = functools.partial(_rnn_kernel, reverse=reverse, n_ctx_tiles=n_ctx, n_tiles=n_tiles, lt=lt)
    return pl.pallas_call(
        kern,
        grid=(n_tiles,),
        in_specs=in_specs,
        out_specs=pl.BlockSpec((rows, rw), lambda j: (tile_of(j), 0)),
        out_shape=jax.ShapeDtypeStruct((nrows, rw), F32),
        scratch_shapes=[pltpu.VMEM((rows, rw), F32), pltpu.VMEM((rows, rw), F32),
                        pltpu.VMEM((SUBLANES, rw), F32)],
        compiler_params=_cparams(("arbitrary",)),
        name="rglru_bwd" if reverse else "rglru_fwd",
    )(*args)


def _out_even_kernel(att_ref, ga_ref, y_ref, gr_ref, x_ref, mod_ref, wa_ref, wr_ref, o_ref, a_scr):
    nblk = ATT_WIDTH // LANES
    for b in range(SUBLANES):
        for j in range(nblk):
            a_scr[j, _batch_rows(b), :] = att_ref[b, :, LANES * j:LANES * (j + 1)]
    att = jnp.concatenate([a_scr[j] for j in range(nblk)], axis=1)
    m1 = (att * _silu(ga_ref[...].astype(F32))).astype(BF16)
    m2 = (y_ref[...] * _silu(gr_ref[...].astype(F32))).astype(BF16)
    o = (jnp.dot(m1, wa_ref[...], preferred_element_type=F32)
         + jnp.dot(m2, wr_ref[...], preferred_element_type=F32))
    o_ref[...] = x_ref[...] + _per_batch(o, mod_ref[0, 2])


def _out_even(att, ga, y, gr, xc, mods, wa, wr):
    nrows = xc.shape[0]
    nt = nrows // ROWS
    d = D_MODEL
    row = lambda i: (i, 0)
    const = lambda i: (0, 0)
    return pl.pallas_call(
        _out_even_kernel,
        grid=(nt,),
        in_specs=[
            pl.BlockSpec((SUBLANES, ROW_TT, ATT_WIDTH), lambda i: (0, i, 0)),
            pl.BlockSpec((ROWS, ATT_WIDTH), row),
            pl.BlockSpec((ROWS, RNN_WIDTH), row),
            pl.BlockSpec((ROWS, RNN_WIDTH), row),
            pl.BlockSpec((ROWS, d), row),
            _mod_spec(CTX_LEN // ROW_TT),
            pl.BlockSpec(wa.shape, const),
            pl.BlockSpec(wr.shape, const),
        ],
        out_specs=pl.BlockSpec((ROWS, d), row),
        out_shape=jax.ShapeDtypeStruct(xc.shape, F32),
        scratch_shapes=[pltpu.VMEM((ATT_WIDTH // LANES, ROWS, LANES), F32)],
        compiler_params=_cparams(("parallel",)),
        name="even_out_proj",
    )(att, ga, y, gr, xc, mods, wa, wr)


def _in_odd_kernel(x_ref, mod_ref, w_ref, u_ref, g_ref):
    h = _norm_mod(x_ref[...], mod_ref).astype(BF16)
    y = jnp.dot(h, w_ref[...], preferred_element_type=F32)
    u_ref[...] = y[:, 0:S5_WIDTH]
    g_ref[...] = y[:, S5_WIDTH:2 * S5_WIDTH].astype(BF16)


def _in_odd(xc, mods, w):
    nrows = xc.shape[0]
    nt = nrows // ROWS
    d = D_MODEL
    row = lambda i: (i, 0)
    return pl.pallas_call(
        _in_odd_kernel,
        grid=(nt,),
        in_specs=[
            pl.BlockSpec((ROWS, d), row),
            _mod_spec(CTX_LEN // ROW_TT),
            pl.BlockSpec(w.shape, lambda i: (0, 0)),
        ],
        out_specs=[pl.BlockSpec((ROWS, S5_WIDTH), row), pl.BlockSpec((ROWS, S5_WIDTH), row)],
        out_shape=[jax.ShapeDtypeStruct((nrows, S5_WIDTH), F32),
                   jax.ShapeDtypeStruct((nrows, S5_WIDTH), BF16)],
        compiler_params=_cparams(("parallel",)),
        name="odd_in_proj",
    )(xc, mods, w)


def _s5_kernel(u_ref, msc_ref, mic_ref, moc_ref, etj_ref, eap_ref, a_ref, extra_ref, y_ref,
               ms_scr, mio_scr, hp_scr, h_scr, *, reverse):
    j = pl.program_id(1)
    nc = S5_TT // S5_TC
    rows = nc * SUBLANES
    sw = S5_SW

    @pl.when(j == 0)
    def _():
        h_scr[...] = jnp.zeros_like(h_scr)
        cw = S5_TC * LANES
        r = lax.broadcasted_iota(jnp.int32, (cw, 1), 0)
        c = lax.broadcasted_iota(jnp.int32, (1, cw), 1)
        row_group_ch = (r // S5_GROUP) % S5_GB
        row_group_st = (r // S5_STATE) % S5_GB
        col_group_ch = (c // S5_GROUP) % S5_GB
        col_group_st = (c // S5_STATE) % S5_GB

        def expand(compact_ref, e_ref, keep):
            full = jnp.dot(compact_ref[0], e_ref[...], preferred_element_type=F32)
            return jnp.where(keep, full, 0.0).astype(BF16)

        ms_scr[...] = expand(msc_ref, eap_ref, row_group_ch == col_group_st)
        mio_scr[0:cw, :] = expand(mic_ref, etj_ref, row_group_ch == col_group_ch)
        mio_scr[cw:2 * cw, :] = expand(moc_ref, etj_ref, row_group_st == col_group_ch)

    def chunk_major(ref):
        return jnp.concatenate(
            [ref[:, SUBLANES * t:SUBLANES * (t + 1), :].reshape(rows, LANES) for t in range(S5_TC)], axis=1)

    lhs32 = chunk_major(u_ref)
    lhs = lhs32.astype(BF16)
    local = jnp.dot(lhs, ms_scr[...], preferred_element_type=F32)
    ar = a_ref[0, :, 0:sw]
    ai = a_ref[0, :, sw:2 * sw]
    hr = h_scr[:, 0:sw]
    hi = h_scr[:, sw:2 * sw]
    for c in (range(nc - 1, -1, -1) if reverse else range(nc)):
        rs = slice(SUBLANES * c, SUBLANES * (c + 1))
        hp_scr[rs, 0:sw] = hr
        hp_scr[rs, sw:2 * sw] = hi
        hr, hi = (ar * hr - ai * hi + local[rs, 0:sw], ar * hi + ai * hr + local[rs, sw:2 * sw])
    h_scr[:, 0:sw] = hr
    h_scr[:, sw:2 * sw] = hi
    cw = S5_TC * LANES
    tile = 2 * LANES
    hp = hp_scr[...].astype(BF16)
    parts = []
    for b in range(cw // tile):
        cs = slice(tile * b, tile * (b + 1))
        ks = slice(tile * b, cw) if reverse else slice(0, tile * (b + 1))
        parts.append(jnp.dot(lhs[:, ks], mio_scr[ks, cs], preferred_element_type=F32)
                     + jnp.dot(hp, mio_scr[cw:2 * cw, cs], preferred_element_type=F32))
    y = jnp.concatenate(parts, axis=1)
    if reverse:
        y = y + chunk_major(extra_ref)
    else:
        y = y + extra_ref[...] * lhs32
    for t in range(S5_TC):
        y_ref[:, SUBLANES * t:SUBLANES * (t + 1), :] = (
            y[:, LANES * t:LANES * (t + 1)].reshape(nc, SUBLANES, LANES))


def _s5_dir(u3, msc, mic, moc, etj, eap, a, extra, reverse):
    nchunks, crow, width = u3.shape
    nc = S5_TT // S5_TC
    n_tiles = nchunks // nc
    n_ctx = CTX_LEN // S5_TT
    tile_of = lambda j: _scan_tile(j, reverse, n_ctx, n_tiles)
    blk = lambda k, j: (tile_of(j), 0, k)
    wblk = lambda k, j: (k, 0, 0)
    cw = S5_TC * LANES
    if reverse:
        extra_spec = pl.BlockSpec((nc, crow, LANES), blk)
    else:
        extra_spec = pl.BlockSpec((1, cw), lambda k, j: (0, k))
    return pl.pallas_call(
        functools.partial(_s5_kernel, reverse=reverse),
        grid=(S5_NB, n_tiles),
        in_specs=[
            pl.BlockSpec((nc, crow, LANES), blk),
            pl.BlockSpec((1, cw, LANES), wblk),
            pl.BlockSpec((1, cw, LANES), wblk),
            pl.BlockSpec((1, cw, LANES), wblk),
            pl.BlockSpec(etj.shape, lambda k, j: (0, 0)),
            pl.BlockSpec(eap.shape, lambda k, j: (0, 0)),
            pl.BlockSpec((1, SUBLANES, 2 * S5_SW), wblk),
            extra_spec,
        ],
        out_specs=pl.BlockSpec((nc, crow, LANES), blk),
        out_shape=jax.ShapeDtypeStruct(u3.shape, F32),
        scratch_shapes=[pltpu.VMEM((cw, 2 * S5_SW), BF16), pltpu.VMEM((cw + 2 * S5_SW, cw), BF16),
                        pltpu.VMEM((nc * SUBLANES, 2 * S5_SW), F32), pltpu.VMEM((SUBLANES, 2 * S5_SW), F32)],
        compiler_params=_cparams(("arbitrary", "arbitrary")),
        name="s5_bwd" if reverse else "s5_fwd",
    )(u3, msc, mic, moc, etj, eap, a, extra)


def _out_odd_kernel(y_ref, g_ref, x_ref, mod_ref, gw_ref, gb_ref, wo_ref, fw_ref, o_ref, *scr, final):
    yy = jax.nn.gelu(y_ref[...]).astype(BF16)
    z = jnp.dot(yy, gw_ref[...], preferred_element_type=F32) + gb_ref[...]
    g = g_ref[...].astype(F32)
    m = z[:, 0:S5_WIDTH] * _sigmoid(z[:, S5_WIDTH:2 * S5_WIDTH]) * _silu(g)
    o = jnp.dot(m.astype(BF16), wo_ref[...], preferred_element_type=F32)
    xn = x_ref[...] + _per_batch(o, mod_ref[0, 2])
    if final:
        (o_scr,) = scr
        ms = jnp.mean(xn * xn, axis=-1, keepdims=True)
        xo = xn * lax.rsqrt(ms + EPS) * fw_ref[...]
        nblk = D_MODEL // LANES
        for j in range(nblk):
            o_scr[j] = xo[:, LANES * j:LANES * (j + 1)]
        for b in range(SUBLANES):
            for j in range(nblk):
                o_ref[b, :, LANES * j:LANES * (j + 1)] = o_scr[j, _batch_rows(b), :]
    else:
        o_ref[...] = xn


def _out_odd(y, g, xc, mods, gw, gb, wo, fw, final):
    nrows = xc.shape[0]
    d = D_MODEL
    nct = CTX_LEN // ROW_TT
    lt = nrows // SUBLANES
    if final:
        nt = (lt - CTX_LEN) // ROW_TT
        row = lambda i: (i + nct, 0)
        mod_spec = _mod_spec(nct, offset=nct)
        out_spec = pl.BlockSpec((SUBLANES, ROW_TT, d), lambda i: (0, i, 0))
        out_shape = jax.ShapeDtypeStruct((SUBLANES, lt - CTX_LEN, d), F32)
        scratch = [pltpu.VMEM((d // LANES, ROWS, LANES), F32)]
    else:
        nt = lt // ROW_TT
        row = lambda i: (i, 0)
        mod_spec = _mod_spec(nct)
        out_spec = pl.BlockSpec((ROWS, d), row)
        out_shape = jax.ShapeDtypeStruct(xc.shape, F32)
        scratch = []
    const = lambda i: (0, 0)
    return pl.pallas_call(
        functools.partial(_out_odd_kernel, final=final),
        grid=(nt,),
        in_specs=[
            pl.BlockSpec((ROWS, S5_WIDTH), row),
            pl.BlockSpec((ROWS, S5_WIDTH), row),
            pl.BlockSpec((ROWS, d), row),
            mod_spec,
            pl.BlockSpec(gw.shape, const),
            pl.BlockSpec(gb.shape, const),
            pl.BlockSpec(wo.shape, const),
            pl.BlockSpec(fw.shape, const),
        ],
        out_specs=out_spec,
        out_shape=out_shape,
        scratch_shapes=scratch,
        compiler_params=_cparams(("parallel",)),
        name="odd_out_proj_final" if final else "odd_out_proj",
    )(y, g, xc, mods, gw, gb, wo, fw)


def _rope_tables(lt):
    n = lt - CTX_LEN
    rows = n // GRID_W
    row = jnp.repeat(jnp.arange(rows, dtype=F32), GRID_W)
    col = jnp.tile(jnp.arange(GRID_W, dtype=F32), rows)
    n_freq = HEAD_DIM // 4
    inv = ROPE_THETA ** (-jnp.arange(n_freq, dtype=F32) / n_freq)
    ang = jnp.concatenate([row[:, None] * inv, col[:, None] * inv], axis=-1)
    half = HEAD_DIM // 2
    cos = jnp.concatenate([jnp.ones((CTX_LEN, half), F32), jnp.cos(ang)], axis=0)
    sin = jnp.concatenate([jnp.zeros((CTX_LEN, half), F32), jnp.sin(ang)], axis=0)
    reps = LANES // HEAD_DIM
    ct = jnp.tile(jnp.concatenate([cos, cos], axis=-1), (1, reps))
    st = jnp.tile(jnp.concatenate([-sin, sin], axis=-1), (1, reps))
    ct, st = lax.optimization_barrier((ct, st))
    return jnp.repeat(ct, SUBLANES, axis=0), jnp.repeat(st, SUBLANES, axis=0)


def _rope_gains(gain, scale):
    g_eo = jnp.concatenate([gain[0::2], gain[1::2]])
    g_oe = jnp.concatenate([gain[1::2], gain[0::2]])
    reps = LANES // HEAD_DIM
    return jnp.stack([jnp.tile(g_eo, reps), jnp.tile(g_oe, reps)]).astype(F32) * scale


def _even_weights(w_in, w_out):
    aw, kw, rw = ATT_WIDTH, KV_WIDTH, RNN_WIDTH
    eo = jnp.concatenate([jnp.arange(0, HEAD_DIM, 2), jnp.arange(1, HEAD_DIM, 2)])
    oe = jnp.concatenate([jnp.arange(1, HEAD_DIM, 2), jnp.arange(0, HEAD_DIM, 2)])
    slot_head = jnp.array([h * ATT_GROUP + g for g in range(ATT_GROUP) for h in range(ATT_KV_HEADS)])
    q_cols = (slot_head[:, None] * HEAD_DIM + eo[None, :]).reshape(-1)
    q_cols_sw = (slot_head[:, None] * HEAD_DIM + oe[None, :]).reshape(-1)
    kv_heads = jnp.arange(ATT_KV_HEADS)
    k_cols = aw + (kv_heads[:, None] * HEAD_DIM + eo[None, :]).reshape(-1)
    k_cols_sw = aw + (kv_heads[:, None] * HEAD_DIM + oe[None, :]).reshape(-1)
    v_cols = aw + kw + jnp.arange(kw)
    slot_cols = (slot_head[:, None] * HEAD_DIM + jnp.arange(HEAD_DIM)[None, :]).reshape(-1)
    ga_cols = aw + 2 * kw + slot_cols
    u_cols = 2 * aw + 2 * kw + jnp.arange(rw)
    gr_cols = 2 * aw + 2 * kw + rw + jnp.arange(rw)
    cols = jnp.concatenate([q_cols, q_cols_sw, k_cols, k_cols_sw, v_cols, ga_cols, u_cols, gr_cols])
    w = w_in[:, cols].astype(BF16)
    wa = w_out[:aw][slot_cols].astype(BF16)
    wr = w_out[aw:].astype(BF16)
    return w, wa, wr


def _block_ones(width):
    idx = jnp.arange(width) // HEAD_DIM
    return (idx[:, None] == idx[None, :]).astype(BF16)


def _rnn_gate_weights(wa, ba, wx, bx):
    eye = jnp.eye(RNN_BLOCKS, dtype=F32)

    def dense(w):
        return jnp.einsum('hij,hk->hikj', w, eye).reshape(RNN_WIDTH, RNN_WIDTH)

    wg = jnp.concatenate([dense(wa), dense(wx)], axis=1).astype(BF16)
    bg = jnp.concatenate([ba.reshape(-1), bx.reshape(-1)])[None, :]
    return wg, bg


def _s5_discretize(lam_re, lam_im, log_step, b_re, b_im):
    dt = jnp.exp(log_step)[:, None]
    mag = jnp.exp(lam_re * dt)
    ab_re = mag * jnp.cos(lam_im * dt)
    ab_im = mag * jnp.sin(lam_im * dt)
    den = lam_re * lam_re + lam_im * lam_im
    nr, ni = ab_re - 1.0, ab_im
    f_re = (nr * lam_re + ni * lam_im) / den
    f_im = (ni * lam_re - nr * lam_im) / den
    bb_re = f_re[..., None] * b_re - f_im[..., None] * b_im
    bb_im = f_re[..., None] * b_im + f_im[..., None] * b_re
    return ab_re, ab_im, bb_re, bb_im


def _s5_weights(lam_re, lam_im, log_step, b_re, b_im, c_re, c_im, reverse):
    hp = lax.Precision.HIGHEST
    f = lambda t: t.astype(F32)
    lam_re, lam_im, log_step, b_re, b_im, c_re, c_im = map(f, (lam_re, lam_im, log_step, b_re, b_im, c_re, c_im))
    tc, nbk, gb, p, c = S5_TC, S5_NB, S5_GB, S5_STATE, S5_GROUP
    _, _, bb_re, bb_im = _s5_discretize(lam_re, lam_im, log_step, b_re, b_im)
    dt = jnp.exp(log_step)[:, None]
    e = jnp.arange(tc + 1, dtype=F32)[:, None, None]
    mag = jnp.exp(e * (lam_re * dt))
    pw_re = mag * jnp.cos(e * (lam_im * dt))
    pw_im = mag * jnp.sin(e * (lam_im * dt))
    ca_re = c_re[None] * pw_re[:, :, None, :] - c_im[None] * pw_im[:, :, None, :]
    ca_im = c_re[None] * pw_im[:, :, None, :] + c_im[None] * pw_re[:, :, None, :]
    kern = (jnp.einsum('tgjp,gpi->tgji', ca_re[:tc], bb_re, precision=hp)
            - jnp.einsum('tgjp,gpi->tgji', ca_im[:tc], bb_im, precision=hp))
    steps = jnp.arange(tc)
    lag = (steps[:, None] - steps[None, :]) if reverse else (steps[None, :] - steps[:, None])
    kt = jnp.where((lag >= 0)[:, :, None, None, None], kern[jnp.clip(lag, 0, tc - 1)], 0.0)
    kt = kt.reshape(tc, tc, nbk, gb, c, c)
    mic = kt.transpose(2, 0, 3, 5, 1, 4).reshape(nbk, tc * gb * c, tc * c)
    e_s = steps if reverse else (tc - 1 - steps)
    ab_re = pw_re[e_s][..., None] * bb_re[None] - pw_im[e_s][..., None] * bb_im[None]
    ab_im = pw_re[e_s][..., None] * bb_im[None] + pw_im[e_s][..., None] * bb_re[None]
    ab = jnp.stack([ab_re, ab_im], 0).reshape(2, tc, nbk, gb, p, c)
    msc = ab.transpose(2, 1, 3, 5, 0, 4).reshape(nbk, tc * gb * c, 2 * p)
    e_t = (tc - steps) if reverse else (steps + 1)
    co = jnp.stack([ca_re[e_t], -ca_im[e_t]], 0).reshape(2, tc, nbk, gb, c, p)
    moc = co.transpose(2, 0, 3, 5, 1, 4).reshape(nbk, 2 * gb * p, tc * c)
    a = jnp.stack([pw_re[tc], pw_im[tc]], 0).reshape(2, nbk, gb, p).transpose(1, 0, 2, 3).reshape(nbk, 1, 2 * gb * p)
    a = jnp.broadcast_to(a, (nbk, SUBLANES, 2 * gb * p))
    return msc.astype(BF16), mic.astype(BF16), moc.astype(BF16), a


def _s5_expanders():
    cw = S5_TC * LANES
    col = jnp.arange(cw)
    src_tj = (col // LANES) * S5_GROUP + col % S5_GROUP
    src_ap = (col // S5_SW) * S5_STATE + col % S5_STATE
    etj = (jnp.arange(S5_TC * S5_GROUP)[:, None] == src_tj[None, :]).astype(BF16)
    eap = (jnp.arange(2 * S5_STATE)[:, None] == src_ap[None, :]).astype(BF16)
    return etj, eap


def kernel(x, c, ctx, c_ctx, ada_w, ada_b, ev_w_in, ev_w_out, q_norm_w, k_norm_w, rg_conv_w, rg_conv_b, rg_wa, rg_ba, rg_wx, rg_bx, rg_lambda, od_w_in, s5_lambda_re, s5_lambda_im, s5_log_step, s5_b_re, s5_b_im, s5_c_re, s5_c_im, s5_d, glu_w, glu_b, od_w_out, final_norm_w):
    nb, seq, d = x.shape
    assert nb == SUBLANES and d == D_MODEL and ctx.shape[1] == CTX_LEN
    assert seq % ATT_TK == 0
    lt = CTX_LEN + seq

    xc = _to_rows(ctx, x)

    cond = jnp.zeros((16, d), F32).at[:nb].set(c).at[nb].set(c_ctx)
    mods = _modulation(cond, ada_w, ada_b).reshape(DEPTH, 16, 3, d)
    mods_x = mods[:, :nb].transpose(0, 2, 1, 3)
    mods_c = jnp.broadcast_to(mods[:, nb][:, :, None, :], mods_x.shape)
    mods = jnp.stack([mods_c, mods_x], axis=1)

    cos, sin = _rope_tables(lt)
    ones_q = _block_ones(ATT_WIDTH)
    ones_k = _block_ones(KV_WIDTH)
    etj, eap = _s5_expanders()
    fw = final_norm_w.reshape(1, d)
    out = None
    for layer in range(DEPTH):
        j = layer // 2
        ml = mods[layer]
        if layer % 2 == 0:
            w, wa, wr = _even_weights(ev_w_in[j], ev_w_out[j])
            gq = _rope_gains(q_norm_w[j], HEAD_DIM ** -0.5 * math.log2(math.e))
            gk = _rope_gains(k_norm_w[j], 1.0)
            q, k, v, ga, u, gr = _in_even(xc, ml, w, cos, sin, gq, gk, ones_q, ones_k)
            att = _attention(q, k, v)
            conv_b = rg_conv_b[j].reshape(1, RNN_WIDTH)
            y = None
            for direction, reverse in enumerate((False, True)):
                wg, bg = _rnn_gate_weights(rg_wa[j, direction], rg_ba[j, direction],
                                           rg_wx[j, direction], rg_bx[j, direction])
                clam = (-LRU_C * jax.nn.softplus(-rg_lambda[j, direction].astype(F32))).reshape(1, RNN_WIDTH)
                y = _rnn_dir(u, rg_conv_w[j], conv_b, wg, bg, clam, y, reverse)
            xc = _out_even(att, ga, y, gr, xc, ml, wa, wr)
        else:
            u, g = _in_odd(xc, ml, od_w_in[j].astype(BF16))
            u3 = u.reshape(lt // S5_TC, S5_TC * nb, S5_WIDTH)
            y = jnp.tile(s5_d[j].astype(F32).reshape(S5_NB, 1, LANES), (1, S5_TC, 1)).reshape(1, -1)
            for direction, reverse in enumerate((False, True)):
                msc, mic, moc, a = _s5_weights(s5_lambda_re[j, direction], s5_lambda_im[j, direction],
                                               s5_log_step[j, direction], s5_b_re[j, direction],
                                               s5_b_im[j, direction], s5_c_re[j, direction],
                                               s5_c_im[j, direction], reverse)
                y = _s5_dir(u3, msc, mic, moc, etj, eap, a, y, reverse)
            y = y.reshape(lt * nb, S5_WIDTH)
            final = layer == DEPTH - 1
            res = _out_odd(y, g, xc, ml, glu_w[j].astype(BF16), glu_b[j].reshape(1, -1),
                           od_w_out[j].astype(BF16), fw, final)
            if final:
                out = res
            else:
                xc = res
    return out
```

```python
import functools
import math

import jax
import jax.numpy as jnp
from jax import lax
from jax.experimental import pallas as pl
from jax.experimental.pallas import tpu as pltpu

F32 = jnp.float32
BF16 = jnp.bfloat16

D_MODEL = 1024
DEPTH = 4
CTX_LEN = 256
GRID_W = 64
EPS = 1e-6

ATT_HEADS = 8
ATT_KV_HEADS = 2
HEAD_DIM = 64
ATT_GROUP = ATT_HEADS // ATT_KV_HEADS
ATT_WIDTH = ATT_HEADS * HEAD_DIM
KV_WIDTH = ATT_KV_HEADS * HEAD_DIM
ROPE_THETA = 10000.0

RNN_WIDTH = D_MODEL // 2
RNN_BLOCKS = 8
RNN_BLOCK_DIM = RNN_WIDTH // RNN_BLOCKS
LRU_C = 8.0

S5_WIDTH = D_MODEL
S5_GROUP = 16
S5_GROUPS = S5_WIDTH // S5_GROUP
S5_STATE = 64

SUBLANES = 8
LANES = 128
ROW_TT = 64
ROWS = ROW_TT * SUBLANES
ATT_TQ = 256
ATT_TK = 2048
ATT_VROWS = HEAD_DIM + 16
RNN_TT = 64
S5_TT = 256
S5_TC = 8
S5_GB = LANES // S5_GROUP
S5_NB = S5_GROUPS // S5_GB
S5_SW = S5_GB * S5_STATE
VMEM_LIMIT = 56 * 1024 * 1024


def _cparams(sem):
    return pltpu.CompilerParams(dimension_semantics=sem, vmem_limit_bytes=VMEM_LIMIT)


def _sigmoid(x):
    return 0.5 * jnp.tanh(0.5 * x) + 0.5


def _silu(x):
    return x * _sigmoid(x)


def _batch_rows(b):
    return pl.ds(b, ROW_TT, stride=SUBLANES)


def _mod_kernel(c_ref, w_ref, b_ref, o_ref):
    c = c_ref[...]
    sc = _silu(c).astype(BF16)
    o_ref[0] = jnp.dot(sc, w_ref[0].astype(BF16), preferred_element_type=F32) + b_ref[0]


def _modulation(cond, ada_w, ada_b):
    depth, d, d3 = ada_w.shape
    nblk = d3 // d
    return pl.pallas_call(
        _mod_kernel,
        grid=(depth, nblk),
        in_specs=[
            pl.BlockSpec((16, d), lambda l, n: (0, 0)),
            pl.BlockSpec((1, d, d), lambda l, n: (l, 0, n)),
            pl.BlockSpec((1, 1, d), lambda l, n: (l, 0, n)),
        ],
        out_specs=pl.BlockSpec((1, 16, d), lambda l, n: (l, 0, n)),
        out_shape=jax.ShapeDtypeStruct((depth, 16, d3), F32),
        compiler_params=_cparams(("parallel", "parallel")),
        name="ada_mod",
    )(cond, ada_w, ada_b.reshape(depth, 1, d3))


def _to_rows_kernel(c_ref, x_ref, o_ref, scr, *, n_ctx_tiles):
    i = pl.program_id(0)
    nblk = D_MODEL // LANES

    def move(src_ref):
        for b in range(SUBLANES):
            for j in range(nblk):
                scr[j, _batch_rows(b), :] = src_ref[b, :, LANES * j:LANES * (j + 1)]
        for j in range(nblk):
            o_ref[:, LANES * j:LANES * (j + 1)] = scr[j]

    @pl.when(i < n_ctx_tiles)
    def _():
        move(c_ref)

    @pl.when(i >= n_ctx_tiles)
    def _():
        move(x_ref)


def _to_rows(ctx, x):
    nb, seq, d = x.shape
    nct = ctx.shape[1] // ROW_TT
    nt = nct + seq // ROW_TT
    return pl.pallas_call(
        functools.partial(_to_rows_kernel, n_ctx_tiles=nct),
        grid=(nt,),
        in_specs=[
            pl.BlockSpec((nb, ROW_TT, d), lambda i: (0, jnp.minimum(i, nct - 1), 0)),
            pl.BlockSpec((nb, ROW_TT, d), lambda i: (0, jnp.maximum(i - nct, 0), 0)),
        ],
        out_specs=pl.BlockSpec((ROWS, d), lambda i: (i, 0)),
        out_shape=jax.ShapeDtypeStruct((nt * ROWS, d), F32),
        scratch_shapes=[pltpu.VMEM((d // LANES, ROWS, LANES), F32)],
        compiler_params=_cparams(("parallel",)),
        name="to_rows",
    )(ctx, x)


def _per_batch(x, vec):
    r, d = x.shape
    return (x.reshape(r // SUBLANES, SUBLANES, d) * vec[None]).reshape(r, d)


def _norm_mod(x, mod_ref):
    ms = jnp.mean(x * x, axis=-1, keepdims=True)
    xn = x * lax.rsqrt(ms + EPS)
    r, d = x.shape
    x3 = xn.reshape(r // SUBLANES, SUBLANES, d)
    return (x3 * (1.0 + mod_ref[0, 1])[None] + mod_ref[0, 0][None]).reshape(r, d)


def _mod_spec(n_ctx_tiles, offset=0):
    return pl.BlockSpec((1, 3, SUBLANES, D_MODEL),
                        lambda i: (jnp.where(i + offset < n_ctx_tiles, 0, 1), 0, 0, 0))


def _group_mean_sq(y, ones_ref):
    sq = y * y
    hi = sq.astype(BF16)
    lo = (sq - hi.astype(F32)).astype(BF16)
    ones = ones_ref[...]
    s = jnp.dot(hi, ones, preferred_element_type=F32) + jnp.dot(lo, ones, preferred_element_type=F32)
    return s * (1.0 / HEAD_DIM)


def _in_even_kernel(x_ref, mod_ref, w_ref, cos_ref, sin_ref, gq_ref, gk_ref, oq_ref, ok_ref,
                    q_ref, k_ref, v_ref, ga_ref, u_ref, gr_ref, q_scr, kv_scr):
    h = _norm_mod(x_ref[...], mod_ref).astype(BF16)
    y = jnp.dot(h, w_ref[...], preferred_element_type=F32)
    aw, kw = ATT_WIDTH, KV_WIDTH
    cos = cos_ref[...]
    sin = sin_ref[...]
    lane = lax.broadcasted_iota(jnp.int32, (1, LANES), 1)
    lower = (lane % HEAD_DIM) < HEAD_DIM // 2

    def swap(t):
        return jnp.where(lower, pltpu.roll(t, LANES - HEAD_DIM // 2, axis=1), pltpu.roll(t, HEAD_DIM // 2, axis=1))

    qp = y[:, 0:aw]
    rq = lax.rsqrt(_group_mean_sq(qp, oq_ref) + EPS)
    ca = gq_ref[0:1, :] * cos
    sa = gq_ref[1:2, :] * sin
    for j in range(aw // LANES):
        sl = slice(LANES * j, LANES * (j + 1))
        q_scr[j] = rq[:, sl] * (qp[:, sl] * ca + swap(qp[:, sl]) * sa)
    o = aw
    kp = y[:, o:o + kw]
    rk = lax.rsqrt(_group_mean_sq(kp, ok_ref) + EPS)
    kv_scr[0] = rk * (kp * (gk_ref[0:1, :] * cos) + swap(kp) * (gk_ref[1:2, :] * sin))
    o += kw
    kv_scr[1] = y[:, o:o + kw]
    o += kw
    for b in range(SUBLANES):
        for j in range(aw // LANES):
            q_ref[b, :, LANES * j:LANES * (j + 1)] = q_scr[j, _batch_rows(b), :].astype(BF16)
        k_ref[b] = kv_scr[0, _batch_rows(b), :].astype(BF16)
        v_ref[b] = kv_scr[1, _batch_rows(b), :].astype(BF16)
    ga_ref[...] = y[:, o:o + aw].astype(BF16)
    o += aw
    u_ref[...] = y[:, o:o + RNN_WIDTH]
    o += RNN_WIDTH
    gr_ref[...] = y[:, o:o + RNN_WIDTH].astype(BF16)


def _in_even(xc, mods, w, cos, sin, gq, gk, ones_q, ones_k):
    nrows = xc.shape[0]
    nt = nrows // ROWS
    lt = nrows // SUBLANES
    d = D_MODEL
    aw, kw, rw = ATT_WIDTH, KV_WIDTH, RNN_WIDTH
    row = lambda i: (i, 0)
    const = lambda i: (0, 0)
    per_batch = lambda i: (0, i, 0)
    return pl.pallas_call(
        _in_even_kernel,
        grid=(nt,),
        in_specs=[
            pl.BlockSpec((ROWS, d), row),
            _mod_spec(CTX_LEN // ROW_TT),
            pl.BlockSpec(w.shape, const),
            pl.BlockSpec((ROWS, LANES), row),
            pl.BlockSpec((ROWS, LANES), row),
            pl.BlockSpec(gq.shape, const),
            pl.BlockSpec(gk.shape, const),
            pl.BlockSpec(ones_q.shape, const),
            pl.BlockSpec(ones_k.shape, const),
        ],
        out_specs=[
            pl.BlockSpec((SUBLANES, ROW_TT, aw), per_batch),
            pl.BlockSpec((SUBLANES, ROW_TT, kw), per_batch),
            pl.BlockSpec((SUBLANES, ROW_TT, kw), per_batch),
            pl.BlockSpec((ROWS, aw), row),
            pl.BlockSpec((ROWS, rw), row),
            pl.BlockSpec((ROWS, rw), row),
        ],
        out_shape=[
            jax.ShapeDtypeStruct((SUBLANES, lt, aw), BF16),
            jax.ShapeDtypeStruct((SUBLANES, lt, kw), BF16),
            jax.ShapeDtypeStruct((SUBLANES, lt, kw), BF16),
            jax.ShapeDtypeStruct((nrows, aw), BF16),
            jax.ShapeDtypeStruct((nrows, rw), F32),
            jax.ShapeDtypeStruct((nrows, rw), BF16),
        ],
        scratch_shapes=[pltpu.VMEM((aw // LANES, ROWS, LANES), F32), pltpu.VMEM((2, ROWS, LANES), F32)],
        compiler_params=_cparams(("parallel",)),
        name="even_in_proj",
    )(xc, mods, w, cos, sin, gq, gk, ones_q, ones_k)


def _attn_kernel(q_ref, k_ref, v_ref, o_ref, vt_scr, s_scr, acc_scr, *, n_ctx_tiles, n_x_chunks):
    qt = pl.program_id(1)
    lane = lax.broadcasted_iota(jnp.int32, (1, LANES), 1)
    tq = ATT_TQ
    cols = ATT_GROUP * tq
    lt = v_ref.shape[1]
    n = n_x_chunks
    kt = 4 * LANES

    @pl.when(qt == 0)
    def _():
        def fill(i, carry):
            rs = pl.ds(pl.multiple_of(i * CTX_LEN, CTX_LEN), CTX_LEN)
            vt = v_ref[0, rs, :].astype(F32).T
            for h in range(ATT_KV_HEADS):
                vt_scr[h, 0:HEAD_DIM, rs] = vt[HEAD_DIM * h:HEAD_DIM * (h + 1)].astype(BF16)
                vt_scr[h, HEAD_DIM:ATT_VROWS, rs] = jnp.ones((ATT_VROWS - HEAD_DIM, CTX_LEN), BF16)
            return carry
        lax.fori_loop(0, lt // CTX_LEN, fill, 0)

    def key_range(c):
        return (CTX_LEN + c * ATT_TK, ATT_TK) if c < n else (0, CTX_LEN)

    for h in range(ATT_KV_HEADS):
        in_head = (lane // HEAD_DIM) == h
        qs = jnp.concatenate(
            [jnp.where(in_head, q_ref[0, :, LANES * g:LANES * (g + 1)], jnp.zeros((), BF16))
             for g in range(ATT_GROUP)], axis=0)
        qst = qs.astype(F32).T.astype(BF16)

        def scores(rng, slot, qst=qst):
            start, width = rng
            s_scr[slot, 0:width, :] = jnp.dot(k_ref[0, start:start + width, :], qst,
                                              preferred_element_type=F32)
            return jnp.max(s_scr[slot, 0:width, :], axis=0, keepdims=True)

        def consume(rng, slot, m, mloc, acc, h=h):
            start, width = rng
            m_new = jnp.maximum(m, mloc)
            acc = acc * jnp.exp2(m - m_new)
            kw = min(kt, width)
            for t in range(width // kw):
                p = jnp.exp2(s_scr[slot, kw * t:kw * (t + 1), :] - m_new).astype(BF16)
                acc = acc + jnp.dot(vt_scr[h, :, start + kw * t:start + kw * (t + 1)], p,
                                    preferred_element_type=F32)
            return m_new, acc

        zero = jnp.zeros((ATT_VROWS, cols), F32)

        @pl.when(qt < n_ctx_tiles)
        def _(h=h, scores=scores, consume=consume):
            mloc = scores((0, CTX_LEN), 0)
            acc_scr[h] = consume((0, CTX_LEN), 0, mloc, mloc, zero)[1]

        @pl.when(qt >= n_ctx_tiles)
        def _(h=h, scores=scores, consume=consume):
            mloc = scores(key_range(0), 0)
            m, acc = mloc, zero
            for i in range(n + 1):
                mloc_next = scores(key_range(i + 1), (i + 1) % 2) if i + 1 <= n else None
                m, acc = consume(key_range(i), i % 2, m, mloc, acc)
                mloc = mloc_next
            acc_scr[h] = acc

    first = lane < HEAD_DIM
    pad = jnp.zeros((LANES - ATT_VROWS, tq), F32)
    for g in range(ATT_GROUP):
        cs = slice(tq * g, tq * (g + 1))
        outs = []
        for h in range(ATT_KV_HEADS):
            acc = jnp.concatenate([acc_scr[h, :, cs], pad], axis=0).T
            outs.append(acc / acc[:, HEAD_DIM:HEAD_DIM + 1])
        o_ref[0, :, LANES * g:LANES * (g + 1)] = jnp.where(
            first, outs[0], pltpu.roll(outs[1], HEAD_DIM, axis=1))


def _attention(q, k, v):
    nb, lt, _ = q.shape
    nq = lt // ATT_TQ
    cols = ATT_GROUP * ATT_TQ
    kern = functools.partial(_attn_kernel, n_ctx_tiles=CTX_LEN // ATT_TQ,
                             n_x_chunks=(lt - CTX_LEN) // ATT_TK)
    return pl.pallas_call(
        kern,
        grid=(nb, nq),
        in_specs=[
            pl.BlockSpec((1, ATT_TQ, ATT_WIDTH), lambda b, i: (b, i, 0)),
            pl.BlockSpec((1, lt, KV_WIDTH), lambda b, i: (b, 0, 0)),
            pl.BlockSpec((1, lt, KV_WIDTH), lambda b, i: (b, 0, 0)),
        ],
        out_specs=pl.BlockSpec((1, ATT_TQ, ATT_WIDTH), lambda b, i: (b, i, 0)),
        out_shape=jax.ShapeDtypeStruct((nb, lt, ATT_WIDTH), F32),
        scratch_shapes=[
            pltpu.VMEM((ATT_KV_HEADS, ATT_VROWS, lt), BF16),
            pltpu.VMEM((2, ATT_TK, cols), F32),
            pltpu.VMEM((ATT_KV_HEADS, ATT_VROWS, cols), F32),
        ],
        compiler_params=_cparams(("arbitrary", "arbitrary")),
        name="attention",
    )(q, k, v)


def _scan_tile(j, reverse, n_ctx_tiles, n_tiles):
    if not reverse:
        return j
    return jnp.where(j < n_ctx_tiles, n_ctx_tiles - 1 - j, n_tiles + n_ctx_tiles - 1 - j)


def _rnn_kernel(*refs, reverse, n_ctx_tiles, n_tiles, lt):
    if reverse:
        (u_ref, prev_ref, next_ref, cw_ref, cb_ref, wg_ref, bg_ref, clam_ref, yin_ref,
         y_ref, a_scr, b_scr, h_scr) = refs
    else:
        (u_ref, prev_ref, next_ref, cw_ref, cb_ref, wg_ref, bg_ref, clam_ref,
         y_ref, a_scr, b_scr, h_scr) = refs
        yin_ref = None
    j = pl.program_id(0)
    tile = _scan_tile(j, reverse, n_ctx_tiles, n_tiles)
    rows = RNN_TT * SUBLANES
    rw = RNN_WIDTH

    @pl.when(j == 0)
    def _():
        h_scr[...] = jnp.zeros_like(h_scr)

    t0 = tile * RNN_TT
    t1 = t0 + RNN_TT
    has_prev = jnp.logical_and(t0 != 0, t0 != CTX_LEN)
    has_next = jnp.logical_and(t1 != CTX_LEN, t1 != lt)
    u = u_ref[...]
    prev = jnp.where(has_prev, prev_ref[...], 0.0)
    nxt = jnp.where(has_next, next_ref[0:SUBLANES, :], 0.0)
    ext = jnp.concatenate([prev, u, nxt], axis=0)
    cw = cw_ref[...]
    s = SUBLANES
    cv = (cw[0:1] * ext[0:rows] + cw[1:2] * ext[s:rows + s] + cw[2:3] * ext[2 * s:rows + 2 * s]
          + cw[3:4] * ext[3 * s:rows + 3 * s] + cb_ref[...])
    g = jnp.dot(cv.astype(BF16), wg_ref[...], preferred_element_type=F32) + bg_ref[...]
    r = _sigmoid(g[:, 0:rw])
    i = _sigmoid(g[:, rw:2 * rw])
    log_a = clam_ref[...] * r
    a_scr[...] = jnp.exp(log_a)
    th = jnp.tanh(log_a)
    b_scr[...] = jnp.sqrt(-2.0 * th / (1.0 - th)) * (i * cv)

    def step(k, h):
        t = (RNN_TT - 1 - k) if reverse else k
        rs = pl.ds(pl.multiple_of(t * SUBLANES, SUBLANES), SUBLANES)
        h = a_scr[rs, :] * h + b_scr[rs, :]
        if reverse:
            y_ref[rs, :] = h + yin_ref[rs, :]
        else:
            y_ref[rs, :] = h
        return h

    h_scr[...] = lax.fori_loop(0, RNN_TT, step, h_scr[...], unroll=8)


def _rnn_dir(u, conv_w, conv_b, wg, bg, clam, yin, reverse):
    nrows = u.shape[0]
    rows = RNN_TT * SUBLANES
    n_tiles = nrows // rows
    n_ctx = CTX_LEN // RNN_TT
    lt = nrows // SUBLANES
    halo = 2 * SUBLANES
    per = rows // halo
    nhalo = nrows // halo
    tile_of = lambda j: _scan_tile(j, reverse, n_ctx, n_tiles)
    rw = RNN_WIDTH
    const = lambda j: (0, 0)
    in_specs = [
        pl.BlockSpec((rows, rw), lambda j: (tile_of(j), 0)),
        pl.BlockSpec((halo, rw), lambda j: (jnp.maximum(tile_of(j) * per - 1, 0), 0)),
        pl.BlockSpec((halo, rw), lambda j: (jnp.minimum((tile_of(j) + 1) * per, nhalo - 1), 0)),
        pl.BlockSpec((4, rw), const),
        pl.BlockSpec((1, rw), const),
        pl.BlockSpec((rw, 2 * rw), const),
        pl.BlockSpec((1, 2 * rw), const),
        pl.BlockSpec((1, rw), const),
    ]
    args = [u, u, u, conv_w, conv_b, wg, bg, clam]
    if reverse:
        in_specs.append(pl.BlockSpec((rows, rw), lambda j: (tile_of(j), 0)))
        args.append(yin)
    kern = functools.partial(_rnn_kernel, reverse=reverse, n_ctx_tiles=n_ctx, n_tiles=n_tiles, lt=lt)
    return pl.pallas_call(
        kern,
        grid=(n_tiles,),
        in_specs=in_specs,
        out_specs=pl.BlockSpec((rows, rw), lambda j: (tile_of(j), 0)),
        out_shape=jax.ShapeDtypeStruct((nrows, rw), F32),
        scratch_shapes=[pltpu.VMEM((rows, rw), F32), pltpu.VMEM((rows, rw), F32),
                        pltpu.VMEM((SUBLANES, rw), F32)],
        compiler_params=_cparams(("arbitrary",)),
        name="rglru_bwd" if reverse else "rglru_fwd",
    )(*args)


def _out_even_in_odd_kernel(att_ref, ga_ref, y_ref, gr_ref, x_ref, mod_ref, wa_ref, wr_ref,
                            nmod_ref, wn_ref, o_ref, u_ref, g_ref, a_scr):
    nblk = ATT_WIDTH // LANES
    for b in range(SUBLANES):
        for j in range(nblk):
            a_scr[j, _batch_rows(b), :] = att_ref[b, :, LANES * j:LANES * (j + 1)]
    att = jnp.concatenate([a_scr[j] for j in range(nblk)], axis=1)
    m1 = (att * _silu(ga_ref[...].astype(F32))).astype(BF16)
    m2 = (y_ref[...] * _silu(gr_ref[...].astype(F32))).astype(BF16)
    o = (jnp.dot(m1, wa_ref[...], preferred_element_type=F32)
         + jnp.dot(m2, wr_ref[...], preferred_element_type=F32))
    xn = x_ref[...] + _per_batch(o, mod_ref[0, 2])
    o_ref[...] = xn
    h = _norm_mod(xn, nmod_ref).astype(BF16)
    z = jnp.dot(h, wn_ref[...], preferred_element_type=F32)
    u_ref[...] = z[:, 0:S5_WIDTH]
    g_ref[...] = z[:, S5_WIDTH:2 * S5_WIDTH].astype(BF16)


def _out_even_in_odd(att, ga, y, gr, xc, mods, wa, wr, next_mods, wn):
    nrows = xc.shape[0]
    nt = nrows // ROWS
    d = D_MODEL
    row = lambda i: (i, 0)
    const = lambda i: (0, 0)
    return pl.pallas_call(
        _out_even_in_odd_kernel,
        grid=(nt,),
        in_specs=[
            pl.BlockSpec((SUBLANES, ROW_TT, ATT_WIDTH), lambda i: (0, i, 0)),
            pl.BlockSpec((ROWS, ATT_WIDTH), row),
            pl.BlockSpec((ROWS, RNN_WIDTH), row),
            pl.BlockSpec((ROWS, RNN_WIDTH), row),
            pl.BlockSpec((ROWS, d), row),
            _mod_spec(CTX_LEN // ROW_TT),
            pl.BlockSpec(wa.shape, const),
            pl.BlockSpec(wr.shape, const),
            _mod_spec(CTX_LEN // ROW_TT),
            pl.BlockSpec(wn.shape, const),
        ],
        out_specs=[pl.BlockSpec((ROWS, d), row), pl.BlockSpec((ROWS, S5_WIDTH), row),
                   pl.BlockSpec((ROWS, S5_WIDTH), row)],
        out_shape=[jax.ShapeDtypeStruct(xc.shape, F32),
                   jax.ShapeDtypeStruct((nrows, S5_WIDTH), F32),
                   jax.ShapeDtypeStruct((nrows, S5_WIDTH), BF16)],
        scratch_shapes=[pltpu.VMEM((ATT_WIDTH // LANES, ROWS, LANES), F32)],
        compiler_params=_cparams(("parallel",)),
        name="even_out_odd_in_proj",
    )(att, ga, y, gr, xc, mods, wa, wr, next_mods, wn)


def _s5_kernel(u_ref, msc_ref, mic_ref, moc_ref, etj_ref, eap_ref, a_ref, extra_ref, y_ref,
               ms_scr, mio_scr, hp_scr, h_scr, *, reverse):
    j = pl.program_id(1)
    nc = S5_TT // S5_TC
    rows = nc * SUBLANES
    sw = S5_SW

    @pl.when(j == 0)
    def _():
        h_scr[...] = jnp.zeros_like(h_scr)
        cw = S5_TC * LANES
        r = lax.broadcasted_iota(jnp.int32, (cw, 1), 0)
        c = lax.broadcasted_iota(jnp.int32, (1, cw), 1)
        row_group_ch = (r // S5_GROUP) % S5_GB
        row_group_st = (r // S5_STATE) % S5_GB
        col_group_ch = (c // S5_GROUP) % S5_GB
        col_group_st = (c // S5_STATE) % S5_GB

        def expand(compact_ref, e_ref, keep):
            full = jnp.dot(compact_ref[0], e_ref[...], preferred_element_type=F32)
            return jnp.where(keep, full, 0.0).astype(BF16)

        ms_scr[...] = expand(msc_ref, eap_ref, row_group_ch == col_group_st)
        mio_scr[0:cw, :] = expand(mic_ref, etj_ref, row_group_ch == col_group_ch)
        mio_scr[cw:2 * cw, :] = expand(moc_ref, etj_ref, row_group_st == col_group_ch)

    def chunk_major(ref):
        return jnp.concatenate(
            [ref[:, SUBLANES * t:SUBLANES * (t + 1), :].reshape(rows, LANES) for t in range(S5_TC)], axis=1)

    lhs32 = chunk_major(u_ref)
    lhs = lhs32.astype(BF16)
    local = jnp.dot(lhs, ms_scr[...], preferred_element_type=F32)
    ar = a_ref[0, :, 0:sw]
    ai = a_ref[0, :, sw:2 * sw]
    hr = h_scr[:, 0:sw]
    hi = h_scr[:, sw:2 * sw]
    for c in (range(nc - 1, -1, -1) if reverse else range(nc)):
        rs = slice(SUBLANES * c, SUBLANES * (c + 1))
        hp_scr[rs, 0:sw] = hr
        hp_scr[rs, sw:2 * sw] = hi
        hr, hi = (ar * hr - ai * hi + local[rs, 0:sw], ar * hi + ai * hr + local[rs, sw:2 * sw])
    h_scr[:, 0:sw] = hr
    h_scr[:, sw:2 * sw] = hi
    cw = S5_TC * LANES
    tile = 2 * LANES
    hp = hp_scr[...].astype(BF16)
    parts = []
    for b in range(cw // tile):
        cs = slice(tile * b, tile * (b + 1))
        ks = slice(tile * b, cw) if reverse else slice(0, tile * (b + 1))
        parts.append(jnp.dot(lhs[:, ks], mio_scr[ks, cs], preferred_element_type=F32)
                     + jnp.dot(hp, mio_scr[cw:2 * cw, cs], preferred_element_type=F32))
    y = jnp.concatenate(parts, axis=1)
    if reverse:
        y = y + chunk_major(extra_ref)
    else:
        y = y + extra_ref[...] * lhs32
    for t in range(S5_TC):
        y_ref[:, SUBLANES * t:SUBLANES * (t + 1), :] = (
            y[:, LANES * t:LANES * (t + 1)].reshape(nc, SUBLANES, LANES))


def _s5_dir(u3, msc, mic, moc, etj, eap, a, extra, reverse):
    nchunks, crow, width = u3.shape
    nc = S5_TT // S5_TC
    n_tiles = nchunks // nc
    n_ctx = CTX_LEN // S5_TT
    tile_of = lambda j: _scan_tile(j, reverse, n_ctx, n_tiles)
    blk = lambda k, j: (tile_of(j), 0, k)
    wblk = lambda k, j: (k, 0, 0)
    cw = S5_TC * LANES
    if reverse:
        extra_spec = pl.BlockSpec((nc, crow, LANES), blk)
    else:
        extra_spec = pl.BlockSpec((1, cw), lambda k, j: (0, k))
    return pl.pallas_call(
        functools.partial(_s5_kernel, reverse=reverse),
        grid=(S5_NB, n_tiles),
        in_specs=[
            pl.BlockSpec((nc, crow, LANES), blk),
            pl.BlockSpec((1, cw, LANES), wblk),
            pl.BlockSpec((1, cw, LANES), wblk),
            pl.BlockSpec((1, cw, LANES), wblk),
            pl.BlockSpec(etj.shape, lambda k, j: (0, 0)),
            pl.BlockSpec(eap.shape, lambda k, j: (0, 0)),
            pl.BlockSpec((1, SUBLANES, 2 * S5_SW), wblk),
            extra_spec,
        ],
        out_specs=pl.BlockSpec((nc, crow, LANES), blk),
        out_shape=jax.ShapeDtypeStruct(u3.shape, F32),
        scratch_shapes=[pltpu.VMEM((cw, 2 * S5_SW), BF16), pltpu.VMEM((cw + 2 * S5_SW, cw), BF16),
                        pltpu.VMEM((nc * SUBLANES, 2 * S5_SW), F32), pltpu.VMEM((SUBLANES, 2 * S5_SW), F32)],
        compiler_params=_cparams(("arbitrary", "arbitrary")),
        name="s5_bwd" if reverse else "s5_fwd",
    )(u3, msc, mic, moc, etj, eap, a, extra)


def _out_odd_kernel(y_ref, g_ref, x_ref, mod_ref, gw_ref, gb_ref, wo_ref, fw_ref, o_ref, *scr, final):
    yy = jax.nn.gelu(y_ref[...]).astype(BF16)
    z = jnp.dot(yy, gw_ref[...], preferred_element_type=F32) + gb_ref[...]
    g = g_ref[...].astype(F32)
    m = z[:, 0:S5_WIDTH] * _sigmoid(z[:, S5_WIDTH:2 * S5_WIDTH]) * _silu(g)
    o = jnp.dot(m.astype(BF16), wo_ref[...], preferred_element_type=F32)
    xn = x_ref[...] + _per_batch(o, mod_ref[0, 2])
    if final:
        (o_scr,) = scr
        ms = jnp.mean(xn * xn, axis=-1, keepdims=True)
        xo = xn * lax.rsqrt(ms + EPS) * fw_ref[...]
        nblk = D_MODEL // LANES
        for j in range(nblk):
            o_scr[j] = xo[:, LANES * j:LANES * (j + 1)]
        for b in range(SUBLANES):
            for j in range(nblk):
                o_ref[b, :, LANES * j:LANES * (j + 1)] = o_scr[j, _batch_rows(b), :]
    else:
        o_ref[...] = xn


def _out_odd(y, g, xc, mods, gw, gb, wo, fw, final):
    nrows = xc.shape[0]
    d = D_MODEL
    nct = CTX_LEN // ROW_TT
    lt = nrows // SUBLANES
    if final:
        nt = (lt - CTX_LEN) // ROW_TT
        row = lambda i: (i + nct, 0)
        mod_spec = _mod_spec(nct, offset=nct)
        out_spec = pl.BlockSpec((SUBLANES, ROW_TT, d), lambda i: (0, i, 0))
        out_shape = jax.ShapeDtypeStruct((SUBLANES, lt - CTX_LEN, d), F32)
        scratch = [pltpu.VMEM((d // LANES, ROWS, LANES), F32)]
    else:
        nt = lt // ROW_TT
        row = lambda i: (i, 0)
        mod_spec = _mod_spec(nct)
        out_spec = pl.BlockSpec((ROWS, d), row)
        out_shape = jax.ShapeDtypeStruct(xc.shape, F32)
        scratch = []
    const = lambda i: (0, 0)
    return pl.pallas_call(
        functools.partial(_out_odd_kernel, final=final),
        grid=(nt,),
        in_specs=[
            pl.BlockSpec((ROWS, S5_WIDTH), row),
            pl.BlockSpec((ROWS, S5_WIDTH), row),
            pl.BlockSpec((ROWS, d), row),
            mod_spec,
            pl.BlockSpec(gw.shape, const),
            pl.BlockSpec(gb.shape, const),
            pl.BlockSpec(wo.shape, const),
            pl.BlockSpec(fw.shape, const),
        ],
        out_specs=out_spec,
        out_shape=out_shape,
        scratch_shapes=scratch,
        compiler_params=_cparams(("parallel",)),
        name="odd_out_proj_final" if final else "odd_out_proj",
    )(y, g, xc, mods, gw, gb, wo, fw)


def _rope_tables(lt):
    n = lt - CTX_LEN
    rows = n // GRID_W
    row = jnp.repeat(jnp.arange(rows, dtype=F32), GRID_W)
    col = jnp.tile(jnp.arange(GRID_W, dtype=F32), rows)
    n_freq = HEAD_DIM // 4
    inv = ROPE_THETA ** (-jnp.arange(n_freq, dtype=F32) / n_freq)
    ang = jnp.concatenate([row[:, None] * inv, col[:, None] * inv], axis=-1)
    half = HEAD_DIM // 2
    cos = jnp.concatenate([jnp.ones((CTX_LEN, half), F32), jnp.cos(ang)], axis=0)
    sin = jnp.concatenate([jnp.zeros((CTX_LEN, half), F32), jnp.sin(ang)], axis=0)
    reps = LANES // HEAD_DIM
    ct = jnp.tile(jnp.concatenate([cos, cos], axis=-1), (1, reps))
    st = jnp.tile(jnp.concatenate([-sin, sin], axis=-1), (1, reps))
    ct, st = lax.optimization_barrier((ct, st))
    return jnp.repeat(ct, SUBLANES, axis=0), jnp.repeat(st, SUBLANES, axis=0)


def _rope_gains(gain, scale):
    g_eo = jnp.concatenate([gain[0::2], gain[1::2]])
    g_oe = jnp.concatenate([gain[1::2], gain[0::2]])
    reps = LANES // HEAD_DIM
    return jnp.stack([jnp.tile(g_eo, reps), jnp.tile(g_oe, reps)]).astype(F32) * scale


def _even_weights(w_in, w_out):
    aw, kw, rw = ATT_WIDTH, KV_WIDTH, RNN_WIDTH
    eo = jnp.concatenate([jnp.arange(0, HEAD_DIM, 2), jnp.arange(1, HEAD_DIM, 2)])
    slot_head = jnp.array([h * ATT_GROUP + g for g in range(ATT_GROUP) for h in range(ATT_KV_HEADS)])
    q_cols = (slot_head[:, None] * HEAD_DIM + eo[None, :]).reshape(-1)
    kv_heads = jnp.arange(ATT_KV_HEADS)
    k_cols = aw + (kv_heads[:, None] * HEAD_DIM + eo[None, :]).reshape(-1)
    v_cols = aw + kw + jnp.arange(kw)
    slot_cols = (slot_head[:, None] * HEAD_DIM + jnp.arange(HEAD_DIM)[None, :]).reshape(-1)
    ga_cols = aw + 2 * kw + slot_cols
    u_cols = 2 * aw + 2 * kw + jnp.arange(rw)
    gr_cols = 2 * aw + 2 * kw + rw + jnp.arange(rw)
    cols = jnp.concatenate([q_cols, k_cols, v_cols, ga_cols, u_cols, gr_cols])
    w = w_in[:, cols].astype(BF16)
    wa = w_out[:aw][slot_cols].astype(BF16)
    wr = w_out[aw:].astype(BF16)
    return w, wa, wr


def _block_ones(width):
    idx = jnp.arange(width) // HEAD_DIM
    return (idx[:, None] == idx[None, :]).astype(BF16)


def _rnn_gate_weights(wa, ba, wx, bx):
    eye = jnp.eye(RNN_BLOCKS, dtype=F32)

    def dense(w):
        return jnp.einsum('hij,hk->hikj', w, eye).reshape(RNN_WIDTH, RNN_WIDTH)

    wg = jnp.concatenate([dense(wa), dense(wx)], axis=1).astype(BF16)
    bg = jnp.concatenate([ba.reshape(-1), bx.reshape(-1)])[None, :]
    return wg, bg


def _s5_discretize(lam_re, lam_im, log_step, b_re, b_im):
    dt = jnp.exp(log_step)[:, None]
    mag = jnp.exp(lam_re * dt)
    ab_re = mag * jnp.cos(lam_im * dt)
    ab_im = mag * jnp.sin(lam_im * dt)
    den = lam_re * lam_re + lam_im * lam_im
    nr, ni = ab_re - 1.0, ab_im
    f_re = (nr * lam_re + ni * lam_im) / den
    f_im = (ni * lam_re - nr * lam_im) / den
    bb_re = f_re[..., None] * b_re - f_im[..., None] * b_im
    bb_im = f_re[..., None] * b_im + f_im[..., None] * b_re
    return ab_re, ab_im, bb_re, bb_im


def _s5_weights(lam_re, lam_im, log_step, b_re, b_im, c_re, c_im, reverse):
    hp = lax.Precision.HIGHEST
    f = lambda t: t.astype(F32)
    lam_re, lam_im, log_step, b_re, b_im, c_re, c_im = map(f, (lam_re, lam_im, log_step, b_re, b_im, c_re, c_im))
    tc, nbk, gb, p, c = S5_TC, S5_NB, S5_GB, S5_STATE, S5_GROUP
    _, _, bb_re, bb_im = _s5_discretize(lam_re, lam_im, log_step, b_re, b_im)
    dt = jnp.exp(log_step)[:, None]
    e = jnp.arange(tc + 1, dtype=F32)[:, None, None]
    mag = jnp.exp(e * (lam_re * dt))
    pw_re = mag * jnp.cos(e * (lam_im * dt))
    pw_im = mag * jnp.sin(e * (lam_im * dt))
    ca_re = c_re[None] * pw_re[:, :, None, :] - c_im[None] * pw_im[:, :, None, :]
    ca_im = c_re[None] * pw_im[:, :, None, :] + c_im[None] * pw_re[:, :, None, :]
    kern = (jnp.einsum('tgjp,gpi->tgji', ca_re[:tc], bb_re, precision=hp)
            - jnp.einsum('tgjp,gpi->tgji', ca_im[:tc], bb_im, precision=hp))
    steps = jnp.arange(tc)
    lag = (steps[:, None] - steps[None, :]) if reverse else (steps[None, :] - steps[:, None])
    kt = jnp.where((lag >= 0)[:, :, None, None, None], kern[jnp.clip(lag, 0, tc - 1)], 0.0)
    kt = kt.reshape(tc, tc, nbk, gb, c, c)
    mic = kt.transpose(2, 0, 3, 5, 1, 4).reshape(nbk, tc * gb * c, tc * c)
    e_s = steps if reverse else (tc - 1 - steps)
    ab_re = pw_re[e_s][..., None] * bb_re[None] - pw_im[e_s][..., None] * bb_im[None]
    ab_im = pw_re[e_s][..., None] * bb_im[None] + pw_im[e_s][..., None] * bb_re[None]
    ab = jnp.stack([ab_re, ab_im], 0).reshape(2, tc, nbk, gb, p, c)
    msc = ab.transpose(2, 1, 3, 5, 0, 4).reshape(nbk, tc * gb * c, 2 * p)
    e_t = (tc - steps) if reverse else (steps + 1)
    co = jnp.stack([ca_re[e_t], -ca_im[e_t]], 0).reshape(2, tc, nbk, gb, c, p)
    moc = co.transpose(2, 0, 3, 5, 1, 4).reshape(nbk, 2 * gb * p, tc * c)
    a = jnp.stack([pw_re[tc], pw_im[tc]], 0).reshape(2, nbk, gb, p).transpose(1, 0, 2, 3).reshape(nbk, 1, 2 * gb * p)
    a = jnp.broadcast_to(a, (nbk, SUBLANES, 2 * gb * p))
    return msc.astype(BF16), mic.astype(BF16), moc.astype(BF16), a


def _s5_expanders():
    cw = S5_TC * LANES
    col = jnp.arange(cw)
    src_tj = (col // LANES) * S5_GROUP + col % S5_GROUP
    src_ap = (col // S5_SW) * S5_STATE + col % S5_STATE
    etj = (jnp.arange(S5_TC * S5_GROUP)[:, None] == src_tj[None, :]).astype(BF16)
    eap = (jnp.arange(2 * S5_STATE)[:, None] == src_ap[None, :]).astype(BF16)
    return etj, eap


def kernel(x, c, ctx, c_ctx, ada_w, ada_b, ev_w_in, ev_w_out, q_norm_w, k_norm_w, rg_conv_w, rg_conv_b, rg_wa, rg_ba, rg_wx, rg_bx, rg_lambda, od_w_in, s5_lambda_re, s5_lambda_im, s5_log_step, s5_b_re, s5_b_im, s5_c_re, s5_c_im, s5_d, glu_w, glu_b, od_w_out, final_norm_w):
    nb, seq, d = x.shape
    assert nb == SUBLANES and d == D_MODEL and ctx.shape[1] == CTX_LEN
    assert seq % ATT_TK == 0 and DEPTH % 2 == 0
    lt = CTX_LEN + seq

    xc = _to_rows(ctx, x)

    cond = jnp.zeros((16, d), F32).at[:nb].set(c).at[nb].set(c_ctx)
    mods = _modulation(cond, ada_w, ada_b).reshape(DEPTH, 16, 3, d)
    mods_x = mods[:, :nb].transpose(0, 2, 1, 3)
    mods_c = jnp.broadcast_to(mods[:, nb][:, :, None, :], mods_x.shape)
    mods = jnp.stack([mods_c, mods_x], axis=1)

    cos, sin = _rope_tables(lt)
    ones_q = _block_ones(ATT_WIDTH)
    ones_k = _block_ones(KV_WIDTH)
    etj, eap = _s5_expanders()
    fw = final_norm_w.reshape(1, d)
    out = None
    for layer in range(DEPTH):
        j = layer // 2
        ml = mods[layer]
        if layer % 2 == 0:
            w, wa, wr = _even_weights(ev_w_in[j], ev_w_out[j])
            gq = _rope_gains(q_norm_w[j], HEAD_DIM ** -0.5 * math.log2(math.e))
            gk = _rope_gains(k_norm_w[j], 1.0)
            q, k, v, ga, u, gr = _in_even(xc, ml, w, cos, sin, gq, gk, ones_q, ones_k)
            att = _attention(q, k, v)
            conv_b = rg_conv_b[j].reshape(1, RNN_WIDTH)
            y = None
            for direction, reverse in enumerate((False, True)):
                wg, bg = _rnn_gate_weights(rg_wa[j, direction], rg_ba[j, direction],
                                           rg_wx[j, direction], rg_bx[j, direction])
                clam = (-LRU_C * jax.nn.softplus(-rg_lambda[j, direction].astype(F32))).reshape(1, RNN_WIDTH)
                y = _rnn_dir(u, rg_conv_w[j], conv_b, wg, bg, clam, y, reverse)
            xc, u_odd, g_odd = _out_even_in_odd(att, ga, y, gr, xc, ml, wa, wr, mods[layer + 1],
                                                od_w_in[j].astype(BF16))
        else:
            u, g = u_odd, g_odd
            u3 = u.reshape(lt // S5_TC, S5_TC * nb, S5_WIDTH)
            y = jnp.tile(s5_d[j].astype(F32).reshape(S5_NB, 1, LANES), (1, S5_TC, 1)).reshape(1, -1)
            for direction, reverse in enumerate((False, True)):
                msc, mic, moc, a = _s5_weights(s5_lambda_re[j, direction], s5_lambda_im[j, direction],
                                               s5_log_step[j, direction], s5_b_re[j, direction],
                                               s5_b_im[j, direction], s5_c_re[j, direction],
                                               s5_c_im[j, direction], reverse)
                y = _s5_dir(u3, msc, mic, moc, etj, eap, a, y, reverse)
            y = y.reshape(lt * nb, S5_WIDTH)
            final = layer == DEPTH - 1
            res = _out_odd(y, g, xc, ml, glu_w[j].astype(BF16), glu_b[j].reshape(1, -1),
                           od_w_out[j].astype(BF16), fw, final)
            if final:
                out = res
            else:
                xc = res
    return out
```

```python
import functools
import math

import jax
import jax.numpy as jnp
from jax import lax
from jax.experimental import pallas as pl
from jax.experimental.pallas import tpu as pltpu

F32 = jnp.float32
BF16 = jnp.bfloat16

D_MODEL = 1024
DEPTH = 4
CTX_LEN = 256
GRID_W = 64
EPS = 1e-6

ATT_HEADS = 8
ATT_KV_HEADS = 2
HEAD_DIM = 64
ATT_GROUP = ATT_HEADS // ATT_KV_HEADS
ATT_WIDTH = ATT_HEADS * HEAD_DIM
KV_WIDTH = ATT_KV_HEADS * HEAD_DIM
ROPE_THETA = 10000.0

RNN_WIDTH = D_MODEL // 2
RNN_BLOCKS = 8
RNN_BLOCK_DIM = RNN_WIDTH // RNN_BLOCKS
LRU_C = 8.0

S5_WIDTH = D_MODEL
S5_GROUP = 16
S5_GROUPS = S5_WIDTH // S5_GROUP
S5_STATE = 64

SUBLANES = 8
LANES = 128
ROW_TT = 128
ROWS = ROW_TT * SUBLANES
ATT_TQ = 256
ATT_TK = 2048
ATT_VROWS = HEAD_DIM + 16
RNN_TT = 64
S5_TT = 256
S5_TC = 8
S5_GB = LANES // S5_GROUP
S5_NB = S5_GROUPS // S5_GB
S5_SW = S5_GB * S5_STATE
VMEM_LIMIT = 56 * 1024 * 1024


def _cparams(sem):
    return pltpu.CompilerParams(dimension_semantics=sem, vmem_limit_bytes=VMEM_LIMIT)


def _sigmoid(x):
    return 0.5 * jnp.tanh(0.5 * x) + 0.5


def _silu(x):
    return x * _sigmoid(x)


def _batch_rows(b):
    return pl.ds(b, ROW_TT, stride=SUBLANES)


def _mod_kernel(c_ref, w_ref, b_ref, o_ref):
    c = c_ref[...]
    sc = _silu(c).astype(BF16)
    o_ref[0] = jnp.dot(sc, w_ref[0].astype(BF16), preferred_element_type=F32) + b_ref[0]


def _modulation(cond, ada_w, ada_b):
    depth, d, d3 = ada_w.shape
    nblk = d3 // d
    return pl.pallas_call(
        _mod_kernel,
        grid=(depth, nblk),
        in_specs=[
            pl.BlockSpec((16, d), lambda l, n: (0, 0)),
            pl.BlockSpec((1, d, d), lambda l, n: (l, 0, n)),
            pl.BlockSpec((1, 1, d), lambda l, n: (l, 0, n)),
        ],
        out_specs=pl.BlockSpec((1, 16, d), lambda l, n: (l, 0, n)),
        out_shape=jax.ShapeDtypeStruct((depth, 16, d3), F32),
        compiler_params=_cparams(("parallel", "parallel")),
        name="ada_mod",
    )(cond, ada_w, ada_b.reshape(depth, 1, d3))


def _to_rows_kernel(c_ref, x_ref, o_ref, scr, *, n_ctx_tiles):
    i = pl.program_id(0)
    nblk = D_MODEL // LANES

    def move(src_ref):
        for b in range(SUBLANES):
            for j in range(nblk):
                scr[j, _batch_rows(b), :] = src_ref[b, :, LANES * j:LANES * (j + 1)]
        for j in range(nblk):
            o_ref[:, LANES * j:LANES * (j + 1)] = scr[j]

    @pl.when(i < n_ctx_tiles)
    def _():
        move(c_ref)

    @pl.when(i >= n_ctx_tiles)
    def _():
        move(x_ref)


def _to_rows(ctx, x):
    nb, seq, d = x.shape
    nct = ctx.shape[1] // ROW_TT
    nt = nct + seq // ROW_TT
    return pl.pallas_call(
        functools.partial(_to_rows_kernel, n_ctx_tiles=nct),
        grid=(nt,),
        in_specs=[
            pl.BlockSpec((nb, ROW_TT, d), lambda i: (0, jnp.minimum(i, nct - 1), 0)),
            pl.BlockSpec((nb, ROW_TT, d), lambda i: (0, jnp.maximum(i - nct, 0), 0)),
        ],
        out_specs=pl.BlockSpec((ROWS, d), lambda i: (i, 0)),
        out_shape=jax.ShapeDtypeStruct((nt * ROWS, d), F32),
        scratch_shapes=[pltpu.VMEM((d // LANES, ROWS, LANES), F32)],
        compiler_params=_cparams(("parallel",)),
        name="to_rows",
    )(ctx, x)


def _per_batch(x, vec):
    r, d = x.shape
    return (x.reshape(r // SUBLANES, SUBLANES, d) * vec[None]).reshape(r, d)


def _norm_mod(x, mod_ref):
    ms = jnp.mean(x * x, axis=-1, keepdims=True)
    xn = x * lax.rsqrt(ms + EPS)
    r, d = x.shape
    x3 = xn.reshape(r // SUBLANES, SUBLANES, d)
    return (x3 * (1.0 + mod_ref[0, 1])[None] + mod_ref[0, 0][None]).reshape(r, d)


def _mod_spec(n_ctx_tiles, offset=0):
    return pl.BlockSpec((1, 3, SUBLANES, D_MODEL),
                        lambda i: (jnp.where(i + offset < n_ctx_tiles, 0, 1), 0, 0, 0))


def _group_mean_sq(y, ones_ref):
    sq = y * y
    hi = sq.astype(BF16)
    lo = (sq - hi.astype(F32)).astype(BF16)
    ones = ones_ref[...]
    s = jnp.dot(hi, ones, preferred_element_type=F32) + jnp.dot(lo, ones, preferred_element_type=F32)
    return s * (1.0 / HEAD_DIM)


def _in_even_kernel(x_ref, mod_ref, w_ref, cos_ref, sin_ref, gq_ref, gk_ref, oq_ref, ok_ref,
                    q_ref, k_ref, v_ref, ga_ref, u_ref, gr_ref, q_scr, kv_scr):
    h = _norm_mod(x_ref[...], mod_ref).astype(BF16)
    y = jnp.dot(h, w_ref[...], preferred_element_type=F32)
    aw, kw = ATT_WIDTH, KV_WIDTH
    cos = cos_ref[...]
    sin = sin_ref[...]
    lane = lax.broadcasted_iota(jnp.int32, (1, LANES), 1)
    lower = (lane % HEAD_DIM) < HEAD_DIM // 2

    def swap(t):
        return jnp.where(lower, pltpu.roll(t, LANES - HEAD_DIM // 2, axis=1), pltpu.roll(t, HEAD_DIM // 2, axis=1))

    def rotation(gain_ref):
        ca = gain_ref[0:1, :] * cos
        sa = gain_ref[1:2, :] * sin
        return lambda t: t * ca + swap(t) * sa

    rotate_q = rotation(gq_ref)
    rotate_k = rotation(gk_ref)

    qp = y[:, 0:aw]
    rq = lax.rsqrt(_group_mean_sq(qp, oq_ref) + EPS)
    for j in range(aw // LANES):
        sl = slice(LANES * j, LANES * (j + 1))
        q_scr[j] = rq[:, sl] * qp[:, sl]
    o = aw
    kp = y[:, o:o + kw]
    kv_scr[0] = lax.rsqrt(_group_mean_sq(kp, ok_ref) + EPS) * kp
    o += kw
    kv_scr[1] = y[:, o:o + kw]
    o += kw
    for b in range(SUBLANES):
        for j in range(aw // LANES):
            q_ref[b, :, LANES * j:LANES * (j + 1)] = rotate_q(q_scr[j, _batch_rows(b), :]).astype(BF16)
        k_ref[b] = rotate_k(kv_scr[0, _batch_rows(b), :]).astype(BF16)
        v_ref[b] = kv_scr[1, _batch_rows(b), :].astype(BF16)
    ga_ref[...] = y[:, o:o + aw].astype(BF16)
    o += aw
    u_ref[...] = y[:, o:o + RNN_WIDTH]
    o += RNN_WIDTH
    gr_ref[...] = y[:, o:o + RNN_WIDTH].astype(BF16)


def _in_even(xc, mods, w, cos, sin, gq, gk, ones_q, ones_k):
    nrows = xc.shape[0]
    nt = nrows // ROWS
    lt = nrows // SUBLANES
    d = D_MODEL
    aw, kw, rw = ATT_WIDTH, KV_WIDTH, RNN_WIDTH
    row = lambda i: (i, 0)
    const = lambda i: (0, 0)
    per_batch = lambda i: (0, i, 0)
    return pl.pallas_call(
        _in_even_kernel,
        grid=(nt,),
        in_specs=[
            pl.BlockSpec((ROWS, d), row),
            _mod_spec(CTX_LEN // ROW_TT),
            pl.BlockSpec(w.shape, const),
            pl.BlockSpec((ROW_TT, LANES), row),
            pl.BlockSpec((ROW_TT, LANES), row),
            pl.BlockSpec(gq.shape, const),
            pl.BlockSpec(gk.shape, const),
            pl.BlockSpec(ones_q.shape, const),
            pl.BlockSpec(ones_k.shape, const),
        ],
        out_specs=[
            pl.BlockSpec((SUBLANES, ROW_TT, aw), per_batch),
            pl.BlockSpec((SUBLANES, ROW_TT, kw), per_batch),
            pl.BlockSpec((SUBLANES, ROW_TT, kw), per_batch),
            pl.BlockSpec((ROWS, aw), row),
            pl.BlockSpec((ROWS, rw), row),
            pl.BlockSpec((ROWS, rw), row),
        ],
        out_shape=[
            jax.ShapeDtypeStruct((SUBLANES, lt, aw), BF16),
            jax.ShapeDtypeStruct((SUBLANES, lt, kw), BF16),
            jax.ShapeDtypeStruct((SUBLANES, lt, kw), BF16),
            jax.ShapeDtypeStruct((nrows, aw), BF16),
            jax.ShapeDtypeStruct((nrows, rw), F32),
            jax.ShapeDtypeStruct((nrows, rw), BF16),
        ],
        scratch_shapes=[pltpu.VMEM((aw // LANES, ROWS, LANES), F32), pltpu.VMEM((2, ROWS, LANES), F32)],
        compiler_params=_cparams(("parallel",)),
        name="even_in_proj",
    )(xc, mods, w, cos, sin, gq, gk, ones_q, ones_k)


def _attn_kernel(q_ref, k_ref, v_ref, o_ref, vt_scr, s_scr, acc_scr, *, n_ctx_tiles, n_x_chunks):
    qt = pl.program_id(1)
    lane = lax.broadcasted_iota(jnp.int32, (1, LANES), 1)
    tq = ATT_TQ
    cols = ATT_GROUP * tq
    lt = v_ref.shape[1]
    n = n_x_chunks
    kt = 4 * LANES

    @pl.when(qt == 0)
    def _():
        def fill(i, carry):
            rs = pl.ds(pl.multiple_of(i * CTX_LEN, CTX_LEN), CTX_LEN)
            vt = v_ref[0, rs, :].astype(F32).T
            for h in range(ATT_KV_HEADS):
                vt_scr[h, 0:HEAD_DIM, rs] = vt[HEAD_DIM * h:HEAD_DIM * (h + 1)].astype(BF16)
                vt_scr[h, HEAD_DIM:ATT_VROWS, rs] = jnp.ones((ATT_VROWS - HEAD_DIM, CTX_LEN), BF16)
            return carry
        lax.fori_loop(0, lt // CTX_LEN, fill, 0)

    def key_range(c):
        return (CTX_LEN + c * ATT_TK, ATT_TK) if c < n else (0, CTX_LEN)

    for h in range(ATT_KV_HEADS):
        in_head = (lane // HEAD_DIM) == h
        qs = jnp.concatenate(
            [jnp.where(in_head, q_ref[0, :, LANES * g:LANES * (g + 1)], jnp.zeros((), BF16))
             for g in range(ATT_GROUP)], axis=0)
        qst = qs.astype(F32).T.astype(BF16)

        def scores(rng, slot, qst=qst):
            start, width = rng
            s_scr[slot, 0:width, :] = jnp.dot(k_ref[0, start:start + width, :], qst,
                                              preferred_element_type=F32)
            return jnp.max(s_scr[slot, 0:width, :], axis=0, keepdims=True)

        def consume(rng, slot, m, mloc, acc, h=h):
            start, width = rng
            m_new = jnp.maximum(m, mloc)
            acc = acc * jnp.exp2(m - m_new)
            kw = min(kt, width)
            for t in range(width // kw):
                p = jnp.exp2(s_scr[slot, kw * t:kw * (t + 1), :] - m_new).astype(BF16)
                acc = acc + jnp.dot(vt_scr[h, :, start + kw * t:start + kw * (t + 1)], p,
                                    preferred_element_type=F32)
            return m_new, acc

        zero = jnp.zeros((ATT_VROWS, cols), F32)

        @pl.when(qt < n_ctx_tiles)
        def _(h=h, scores=scores, consume=consume):
            mloc = scores((0, CTX_LEN), 0)
            acc_scr[h] = consume((0, CTX_LEN), 0, mloc, mloc, zero)[1]

        @pl.when(qt >= n_ctx_tiles)
        def _(h=h, scores=scores, consume=consume):
            mloc = scores(key_range(0), 0)
            m, acc = mloc, zero
            for i in range(n + 1):
                mloc_next = scores(key_range(i + 1), (i + 1) % 2) if i + 1 <= n else None
                m, acc = consume(key_range(i), i % 2, m, mloc, acc)
                mloc = mloc_next
            acc_scr[h] = acc

    first = lane < HEAD_DIM
    pad = jnp.zeros((LANES - ATT_VROWS, tq), F32)
    for g in range(ATT_GROUP):
        cs = slice(tq * g, tq * (g + 1))
        outs = []
        for h in range(ATT_KV_HEADS):
            acc = jnp.concatenate([acc_scr[h, :, cs], pad], axis=0).T
            outs.append(acc / acc[:, HEAD_DIM:HEAD_DIM + 1])
        o_ref[0, :, LANES * g:LANES * (g + 1)] = jnp.where(
            first, outs[0], pltpu.roll(outs[1], HEAD_DIM, axis=1))


def _attention(q, k, v):
    nb, lt, _ = q.shape
    nq = lt // ATT_TQ
    cols = ATT_GROUP * ATT_TQ
    kern = functools.partial(_attn_kernel, n_ctx_tiles=CTX_LEN // ATT_TQ,
                             n_x_chunks=(lt - CTX_LEN) // ATT_TK)
    return pl.pallas_call(
        kern,
        grid=(nb, nq),
        in_specs=[
            pl.BlockSpec((1, ATT_TQ, ATT_WIDTH), lambda b, i: (b, i, 0)),
            pl.BlockSpec((1, lt, KV_WIDTH), lambda b, i: (b, 0, 0)),
            pl.BlockSpec((1, lt, KV_WIDTH), lambda b, i: (b, 0, 0)),
        ],
        out_specs=pl.BlockSpec((1, ATT_TQ, ATT_WIDTH), lambda b, i: (b, i, 0)),
        out_shape=jax.ShapeDtypeStruct((nb, lt, ATT_WIDTH), F32),
        scratch_shapes=[
            pltpu.VMEM((ATT_KV_HEADS, ATT_VROWS, lt), BF16),
            pltpu.VMEM((2, ATT_TK, cols), F32),
            pltpu.VMEM((ATT_KV_HEADS, ATT_VROWS, cols), F32),
        ],
        compiler_params=_cparams(("arbitrary", "arbitrary")),
        name="attention",
    )(q, k, v)


def _scan_tile(j, reverse, n_ctx_tiles, n_tiles):
    if not reverse:
        return j
    return jnp.where(j < n_ctx_tiles, n_ctx_tiles - 1 - j, n_tiles + n_ctx_tiles - 1 - j)


def _rnn_kernel(*refs, reverse, n_ctx_tiles, n_tiles, lt):
    if reverse:
        (u_ref, prev_ref, next_ref, cw_ref, cb_ref, wg_ref, bg_ref, clam_ref, yin_ref,
         y_ref, a_scr, b_scr, h_scr) = refs
    else:
        (u_ref, prev_ref, next_ref, cw_ref, cb_ref, wg_ref, bg_ref, clam_ref,
         y_ref, a_scr, b_scr, h_scr) = refs
        yin_ref = None
    j = pl.program_id(0)
    tile = _scan_tile(j, reverse, n_ctx_tiles, n_tiles)
    rows = RNN_TT * SUBLANES
    rw = RNN_WIDTH

    @pl.when(j == 0)
    def _():
        h_scr[...] = jnp.zeros_like(h_scr)

    t0 = tile * RNN_TT
    t1 = t0 + RNN_TT
    has_prev = jnp.logical_and(t0 != 0, t0 != CTX_LEN)
    has_next = jnp.logical_and(t1 != CTX_LEN, t1 != lt)
    u = u_ref[...]
    prev = jnp.where(has_prev, prev_ref[...], 0.0)
    nxt = jnp.where(has_next, next_ref[0:SUBLANES, :], 0.0)
    ext = jnp.concatenate([prev, u, nxt], axis=0)
    cw = cw_ref[...]
    s = SUBLANES
    cv = (cw[0:1] * ext[0:rows] + cw[1:2] * ext[s:rows + s] + cw[2:3] * ext[2 * s:rows + 2 * s]
          + cw[3:4] * ext[3 * s:rows + 3 * s] + cb_ref[...])
    g = jnp.dot(cv.astype(BF16), wg_ref[...], preferred_element_type=F32) + bg_ref[...]
    r = _sigmoid(g[:, 0:rw])
    i = _sigmoid(g[:, rw:2 * rw])
    log_a = clam_ref[...] * r
    a_scr[...] = jnp.exp(log_a)
    th = jnp.tanh(log_a)
    b_scr[...] = jnp.sqrt(-2.0 * th / (1.0 - th)) * (i * cv)

    def step(k, h):
        t = (RNN_TT - 1 - k) if reverse else k
        rs = pl.ds(pl.multiple_of(t * SUBLANES, SUBLANES), SUBLANES)
        h = a_scr[rs, :] * h + b_scr[rs, :]
        if reverse:
            y_ref[rs, :] = h + yin_ref[rs, :]
        else:
            y_ref[rs, :] = h
        return h

    h_scr[...] = lax.fori_loop(0, RNN_TT, step, h_scr[...], unroll=8)


def _rnn_dir(u, conv_w, conv_b, wg, bg, clam, yin, reverse):
    nrows = u.shape[0]
    rows = RNN_TT * SUBLANES
    n_tiles = nrows // rows
    n_ctx = CTX_LEN // RNN_TT
    lt = nrows // SUBLANES
    halo = 2 * SUBLANES
    per = rows // halo
    nhalo = nrows // halo
    tile_of = lambda j: _scan_tile(j, reverse, n_ctx, n_tiles)
    rw = RNN_WIDTH
    const = lambda j: (0, 0)
    in_specs = [
        pl.BlockSpec((rows, rw), lambda j: (tile_of(j), 0)),
        pl.BlockSpec((halo, rw), lambda j: (jnp.maximum(tile_of(j) * per - 1, 0), 0)),
        pl.BlockSpec((halo, rw), lambda j: (jnp.minimum((tile_of(j) + 1) * per, nhalo - 1), 0)),
        pl.BlockSpec((4, rw), const),
        pl.BlockSpec((1, rw), const),
        pl.BlockSpec((rw, 2 * rw), const),
        pl.BlockSpec((1, 2 * rw), const),
        pl.BlockSpec((1, rw), const),
    ]
    args = [u, u, u, conv_w, conv_b, wg, bg, clam]
    if reverse:
        in_specs.append(pl.BlockSpec((rows, rw), lambda j: (tile_of(j), 0)))
        args.append(yin)
    kern = functools.partial(_rnn_kernel, reverse=reverse, n_ctx_tiles=n_ctx, n_tiles=n_tiles, lt=lt)
    return pl.pallas_call(
        kern,
        grid=(n_tiles,),
        in_specs=in_specs,
        out_specs=pl.BlockSpec((rows, rw), lambda j: (tile_of(j), 0)),
        out_shape=jax.ShapeDtypeStruct((nrows, rw), F32),
        scratch_shapes=[pltpu.VMEM((rows, rw), F32), pltpu.VMEM((rows, rw), F32),
                        pltpu.VMEM((SUBLANES, rw), F32)],
        compiler_params=_cparams(("arbitrary",)),
        name="rglru_bwd" if reverse else "rglru_fwd",
    )(*args)


def _out_even_in_odd_kernel(att_ref, ga_ref, y_ref, gr_ref, x_ref, mod_ref, wa_ref, wr_ref,
                            nmod_ref, wn_ref, o_ref, u_ref, g_ref, a_scr):
    nblk = ATT_WIDTH // LANES
    for b in range(SUBLANES):
        for j in range(nblk):
            a_scr[j, _batch_rows(b), :] = att_ref[b, :, LANES * j:LANES * (j + 1)]
    att = jnp.concatenate([a_scr[j] for j in range(nblk)], axis=1)
    m1 = (att * _silu(ga_ref[...].astype(F32))).astype(BF16)
    m2 = (y_ref[...] * _silu(gr_ref[...].astype(F32))).astype(BF16)
    o = (jnp.dot(m1, wa_ref[...], preferred_element_type=F32)
         + jnp.dot(m2, wr_ref[...], preferred_element_type=F32))
    xn = x_ref[...] + _per_batch(o, mod_ref[0, 2])
    o_ref[...] = xn
    h = _norm_mod(xn, nmod_ref).astype(BF16)
    z = jnp.dot(h, wn_ref[...], preferred_element_type=F32)
    u_ref[...] = z[:, 0:S5_WIDTH]
    g_ref[...] = z[:, S5_WIDTH:2 * S5_WIDTH].astype(BF16)


def _out_even_in_odd(att, ga, y, gr, xc, mods, wa, wr, next_mods, wn):
    nrows = xc.shape[0]
    nt = nrows // ROWS
    d = D_MODEL
    row = lambda i: (i, 0)
    const = lambda i: (0, 0)
    return pl.pallas_call(
        _out_even_in_odd_kernel,
        grid=(nt,),
        in_specs=[
            pl.BlockSpec((SUBLANES, ROW_TT, ATT_WIDTH), lambda i: (0, i, 0)),
            pl.BlockSpec((ROWS, ATT_WIDTH), row),
            pl.BlockSpec((ROWS, RNN_WIDTH), row),
            pl.BlockSpec((ROWS, RNN_WIDTH), row),
            pl.BlockSpec((ROWS, d), row),
            _mod_spec(CTX_LEN // ROW_TT),
            pl.BlockSpec(wa.shape, const),
            pl.BlockSpec(wr.shape, const),
            _mod_spec(CTX_LEN // ROW_TT),
            pl.BlockSpec(wn.shape, const),
        ],
        out_specs=[pl.BlockSpec((ROWS, d), row), pl.BlockSpec((ROWS, S5_WIDTH), row),
                   pl.BlockSpec((ROWS, S5_WIDTH), row)],
        out_shape=[jax.ShapeDtypeStruct(xc.shape, F32),
                   jax.ShapeDtypeStruct((nrows, S5_WIDTH), F32),
                   jax.ShapeDtypeStruct((nrows, S5_WIDTH), BF16)],
        scratch_shapes=[pltpu.VMEM((ATT_WIDTH // LANES, ROWS, LANES), F32)],
        compiler_params=_cparams(("parallel",)),
        name="even_out_odd_in_proj",
    )(att, ga, y, gr, xc, mods, wa, wr, next_mods, wn)


def _s5_kernel(u_ref, msc_ref, mic_ref, moc_ref, etj_ref, eap_ref, a_ref, extra_ref, y_ref,
               ms_scr, mio_scr, hp_scr, h_scr, *, reverse):
    j = pl.program_id(1)
    nc = S5_TT // S5_TC
    rows = nc * SUBLANES
    sw = S5_SW

    @pl.when(j == 0)
    def _():
        h_scr[...] = jnp.zeros_like(h_scr)
        cw = S5_TC * LANES
        r = lax.broadcasted_iota(jnp.int32, (cw, 1), 0)
        c = lax.broadcasted_iota(jnp.int32, (1, cw), 1)
        row_group_ch = (r // S5_GROUP) % S5_GB
        row_group_st = (r // S5_STATE) % S5_GB
        col_group_ch = (c // S5_GROUP) % S5_GB
        col_group_st = (c // S5_STATE) % S5_GB

        def expand(compact_ref, e_ref, keep):
            full = jnp.dot(compact_ref[0], e_ref[...], preferred_element_type=F32)
            return jnp.where(keep, full, 0.0).astype(BF16)

        ms_scr[...] = expand(msc_ref, eap_ref, row_group_ch == col_group_st)
        mio_scr[0:cw, :] = expand(mic_ref, etj_ref, row_group_ch == col_group_ch)
        mio_scr[cw:2 * cw, :] = expand(moc_ref, etj_ref, row_group_st == col_group_ch)

    def chunk_major(ref):
        return jnp.concatenate(
            [ref[:, SUBLANES * t:SUBLANES * (t + 1), :].reshape(rows, LANES) for t in range(S5_TC)], axis=1)

    lhs32 = chunk_major(u_ref)
    lhs = lhs32.astype(BF16)
    local = jnp.dot(lhs, ms_scr[...], preferred_element_type=F32)
    ar = a_ref[0, :, 0:sw]
    ai = a_ref[0, :, sw:2 * sw]
    hr = h_scr[:, 0:sw]
    hi = h_scr[:, sw:2 * sw]
    for c in (range(nc - 1, -1, -1) if reverse else range(nc)):
        rs = slice(SUBLANES * c, SUBLANES * (c + 1))
        hp_scr[rs, 0:sw] = hr
        hp_scr[rs, sw:2 * sw] = hi
        hr, hi = (ar * hr - ai * hi + local[rs, 0:sw], ar * hi + ai * hr + local[rs, sw:2 * sw])
    h_scr[:, 0:sw] = hr
    h_scr[:, sw:2 * sw] = hi
    cw = S5_TC * LANES
    tile = 2 * LANES
    hp = hp_scr[...].astype(BF16)
    parts = []
    for b in range(cw // tile):
        cs = slice(tile * b, tile * (b + 1))
        ks = slice(tile * b, cw) if reverse else slice(0, tile * (b + 1))
        parts.append(jnp.dot(lhs[:, ks], mio_scr[ks, cs], preferred_element_type=F32)
                     + jnp.dot(hp, mio_scr[cw:2 * cw, cs], preferred_element_type=F32))
    y = jnp.concatenate(parts, axis=1)
    if reverse:
        y = y + chunk_major(extra_ref)
    else:
        y = y + extra_ref[...] * lhs32
    for t in range(S5_TC):
        y_ref[:, SUBLANES * t:SUBLANES * (t + 1), :] = (
            y[:, LANES * t:LANES * (t + 1)].reshape(nc, SUBLANES, LANES))


def _s5_dir(u3, msc, mic, moc, etj, eap, a, extra, reverse):
    nchunks, crow, width = u3.shape
    nc = S5_TT // S5_TC
    n_tiles = nchunks // nc
    n_ctx = CTX_LEN // S5_TT
    tile_of = lambda j: _scan_tile(j, reverse, n_ctx, n_tiles)
    blk = lambda k, j: (tile_of(j), 0, k)
    wblk = lambda k, j: (k, 0, 0)
    cw = S5_TC * LANES
    if reverse:
        extra_spec = pl.BlockSpec((nc, crow, LANES), blk)
    else:
        extra_spec = pl.BlockSpec((1, cw), lambda k, j: (0, k))
    return pl.pallas_call(
        functools.partial(_s5_kernel, reverse=reverse),
        grid=(S5_NB, n_tiles),
        in_specs=[
            pl.BlockSpec((nc, crow, LANES), blk),
            pl.BlockSpec((1, cw, LANES), wblk),
            pl.BlockSpec((1, cw, LANES), wblk),
            pl.BlockSpec((1, cw, LANES), wblk),
            pl.BlockSpec(etj.shape, lambda k, j: (0, 0)),
            pl.BlockSpec(eap.shape, lambda k, j: (0, 0)),
            pl.BlockSpec((1, SUBLANES, 2 * S5_SW), wblk),
            extra_spec,
        ],
        out_specs=pl.BlockSpec((nc, crow, LANES), blk),
        out_shape=jax.ShapeDtypeStruct(u3.shape, F32),
        scratch_shapes=[pltpu.VMEM((cw, 2 * S5_SW), BF16), pltpu.VMEM((cw + 2 * S5_SW, cw), BF16),
                        pltpu.VMEM((nc * SUBLANES, 2 * S5_SW), F32), pltpu.VMEM((SUBLANES, 2 * S5_SW), F32)],
        compiler_params=_cparams(("arbitrary", "arbitrary")),
        name="s5_bwd" if reverse else "s5_fwd",
    )(u3, msc, mic, moc, etj, eap, a, extra)


def _out_odd_kernel(y_ref, g_ref, x_ref, mod_ref, gw_ref, gb_ref, wo_ref, fw_ref, o_ref, *scr, final):
    yy = jax.nn.gelu(y_ref[...]).astype(BF16)
    z = jnp.dot(yy, gw_ref[...], preferred_element_type=F32) + gb_ref[...]
    g = g_ref[...].astype(F32)
    m = z[:, 0:S5_WIDTH] * _sigmoid(z[:, S5_WIDTH:2 * S5_WIDTH]) * _silu(g)
    o = jnp.dot(m.astype(BF16), wo_ref[...], preferred_element_type=F32)
    xn = x_ref[...] + _per_batch(o, mod_ref[0, 2])
    if final:
        (o_scr,) = scr
        ms = jnp.mean(xn * xn, axis=-1, keepdims=True)
        xo = xn * lax.rsqrt(ms + EPS) * fw_ref[...]
        nblk = D_MODEL // LANES
        for j in range(nblk):
            o_scr[j] = xo[:, LANES * j:LANES * (j + 1)]
        for b in range(SUBLANES):
            for j in range(nblk):
                o_ref[b, :, LANES * j:LANES * (j + 1)] = o_scr[j, _batch_rows(b), :]
    else:
        o_ref[...] = xn


def _out_odd(y, g, xc, mods, gw, gb, wo, fw, final):
    nrows = xc.shape[0]
    d = D_MODEL
    nct = CTX_LEN // ROW_TT
    lt = nrows // SUBLANES
    if final:
        nt = (lt - CTX_LEN) // ROW_TT
        row = lambda i: (i + nct, 0)
        mod_spec = _mod_spec(nct, offset=nct)
        out_spec = pl.BlockSpec((SUBLANES, ROW_TT, d), lambda i: (0, i, 0))
        out_shape = jax.ShapeDtypeStruct((SUBLANES, lt - CTX_LEN, d), F32)
        scratch = [pltpu.VMEM((d // LANES, ROWS, LANES), F32)]
    else:
        nt = lt // ROW_TT
        row = lambda i: (i, 0)
        mod_spec = _mod_spec(nct)
        out_spec = pl.BlockSpec((ROWS, d), row)
        out_shape = jax.ShapeDtypeStruct(xc.shape, F32)
        scratch = []
    const = lambda i: (0, 0)
    return pl.pallas_call(
        functools.partial(_out_odd_kernel, final=final),
        grid=(nt,),
        in_specs=[
            pl.BlockSpec((ROWS, S5_WIDTH), row),
            pl.BlockSpec((ROWS, S5_WIDTH), row),
            pl.BlockSpec((ROWS, d), row),
            mod_spec,
            pl.BlockSpec(gw.shape, const),
            pl.BlockSpec(gb.shape, const),
            pl.BlockSpec(wo.shape, const),
            pl.BlockSpec(fw.shape, const),
        ],
        out_specs=out_spec,
        out_shape=out_shape,
        scratch_shapes=scratch,
        compiler_params=_cparams(("parallel",)),
        name="odd_out_proj_final" if final else "odd_out_proj",
    )(y, g, xc, mods, gw, gb, wo, fw)


def _rope_tables(lt):
    n = lt - CTX_LEN
    rows = n // GRID_W
    row = jnp.repeat(jnp.arange(rows, dtype=F32), GRID_W)
    col = jnp.tile(jnp.arange(GRID_W, dtype=F32), rows)
    n_freq = HEAD_DIM // 4
    inv = ROPE_THETA ** (-jnp.arange(n_freq, dtype=F32) / n_freq)
    ang = jnp.concatenate([row[:, None] * inv, col[:, None] * inv], axis=-1)
    half = HEAD_DIM // 2
    cos = jnp.concatenate([jnp.ones((CTX_LEN, half), F32), jnp.cos(ang)], axis=0)
    sin = jnp.concatenate([jnp.zeros((CTX_LEN, half), F32), jnp.sin(ang)], axis=0)
    reps = LANES // HEAD_DIM
    ct = jnp.tile(jnp.concatenate([cos, cos], axis=-1), (1, reps))
    st = jnp.tile(jnp.concatenate([-sin, sin], axis=-1), (1, reps))
    return ct, st


def _rope_gains(gain, scale):
    g_eo = jnp.concatenate([gain[0::2], gain[1::2]])
    g_oe = jnp.concatenate([gain[1::2], gain[0::2]])
    reps = LANES // HEAD_DIM
    return jnp.stack([jnp.tile(g_eo, reps), jnp.tile(g_oe, reps)]).astype(F32) * scale


def _even_weights(w_in, w_out):
    aw, kw, rw = ATT_WIDTH, KV_WIDTH, RNN_WIDTH
    eo = jnp.concatenate([jnp.arange(0, HEAD_DIM, 2), jnp.arange(1, HEAD_DIM, 2)])
    slot_head = jnp.array([h * ATT_GROUP + g for g in range(ATT_GROUP) for h in range(ATT_KV_HEADS)])
    q_cols = (slot_head[:, None] * HEAD_DIM + eo[None, :]).reshape(-1)
    kv_heads = jnp.arange(ATT_KV_HEADS)
    k_cols = aw + (kv_heads[:, None] * HEAD_DIM + eo[None, :]).reshape(-1)
    v_cols = aw + kw + jnp.arange(kw)
    slot_cols = (slot_head[:, None] * HEAD_DIM + jnp.arange(HEAD_DIM)[None, :]).reshape(-1)
    ga_cols = aw + 2 * kw + slot_cols
    u_cols = 2 * aw + 2 * kw + jnp.arange(rw)
    gr_cols = 2 * aw + 2 * kw + rw + jnp.arange(rw)
    cols = jnp.concatenate([q_cols, k_cols, v_cols, ga_cols, u_cols, gr_cols])
    w = w_in[:, cols].astype(BF16)
    wa = w_out[:aw][slot_cols].astype(BF16)
    wr = w_out[aw:].astype(BF16)
    return w, wa, wr


def _block_ones(width):
    idx = jnp.arange(width) // HEAD_DIM
    return (idx[:, None] == idx[None, :]).astype(BF16)


def _rnn_gate_weights(wa, ba, wx, bx):
    eye = jnp.eye(RNN_BLOCKS, dtype=F32)

    def dense(w):
        return jnp.einsum('hij,hk->hikj', w, eye).reshape(RNN_WIDTH, RNN_WIDTH)

    wg = jnp.concatenate([dense(wa), dense(wx)], axis=1).astype(BF16)
    bg = jnp.concatenate([ba.reshape(-1), bx.reshape(-1)])[None, :]
    return wg, bg


def _s5_discretize(lam_re, lam_im, log_step, b_re, b_im):
    dt = jnp.exp(log_step)[:, None]
    mag = jnp.exp(lam_re * dt)
    ab_re = mag * jnp.cos(lam_im * dt)
    ab_im = mag * jnp.sin(lam_im * dt)
    den = lam_re * lam_re + lam_im * lam_im
    nr, ni = ab_re - 1.0, ab_im
    f_re = (nr * lam_re + ni * lam_im) / den
    f_im = (ni * lam_re - nr * lam_im) / den
    bb_re = f_re[..., None] * b_re - f_im[..., None] * b_im
    bb_im = f_re[..., None] * b_im + f_im[..., None] * b_re
    return ab_re, ab_im, bb_re, bb_im


def _s5_weights(lam_re, lam_im, log_step, b_re, b_im, c_re, c_im, reverse):
    hp = lax.Precision.HIGHEST
    f = lambda t: t.astype(F32)
    lam_re, lam_im, log_step, b_re, b_im, c_re, c_im = map(f, (lam_re, lam_im, log_step, b_re, b_im, c_re, c_im))
    tc, nbk, gb, p, c = S5_TC, S5_NB, S5_GB, S5_STATE, S5_GROUP
    _, _, bb_re, bb_im = _s5_discretize(lam_re, lam_im, log_step, b_re, b_im)
    dt = jnp.exp(log_step)[:, None]
    e = jnp.arange(tc + 1, dtype=F32)[:, None, None]
    mag = jnp.exp(e * (lam_re * dt))
    pw_re = mag * jnp.cos(e * (lam_im * dt))
    pw_im = mag * jnp.sin(e * (lam_im * dt))
    ca_re = c_re[None] * pw_re[:, :, None, :] - c_im[None] * pw_im[:, :, None, :]
    ca_im = c_re[None] * pw_im[:, :, None, :] + c_im[None] * pw_re[:, :, None, :]
    kern = (jnp.einsum('tgjp,gpi->tgji', ca_re[:tc], bb_re, precision=hp)
            - jnp.einsum('tgjp,gpi->tgji', ca_im[:tc], bb_im, precision=hp))
    steps = jnp.arange(tc)
    lag = (steps[:, None] - steps[None, :]) if reverse else (steps[None, :] - steps[:, None])
    kt = jnp.where((lag >= 0)[:, :, None, None, None], kern[jnp.clip(lag, 0, tc - 1)], 0.0)
    kt = kt.reshape(tc, tc, nbk, gb, c, c)
    mic = kt.transpose(2, 0, 3, 5, 1, 4).reshape(nbk, tc * gb * c, tc * c)
    e_s = steps if reverse else (tc - 1 - steps)
    ab_re = pw_re[e_s][..., None] * bb_re[None] - pw_im[e_s][..., None] * bb_im[None]
    ab_im = pw_re[e_s][..., None] * bb_im[None] + pw_im[e_s][..., None] * bb_re[None]
    ab = jnp.stack([ab_re, ab_im], 0).reshape(2, tc, nbk, gb, p, c)
    msc = ab.transpose(2, 1, 3, 5, 0, 4).reshape(nbk, tc * gb * c, 2 * p)
    e_t = (tc - steps) if reverse else (steps + 1)
    co = jnp.stack([ca_re[e_t], -ca_im[e_t]], 0).reshape(2, tc, nbk, gb, c, p)
    moc = co.transpose(2, 0, 3, 5, 1, 4).reshape(nbk, 2 * gb * p, tc * c)
    a = jnp.stack([pw_re[tc], pw_im[tc]], 0).reshape(2, nbk, gb, p).transpose(1, 0, 2, 3).reshape(nbk, 1, 2 * gb * p)
    a = jnp.broadcast_to(a, (nbk, SUBLANES, 2 * gb * p))
    return msc.astype(BF16), mic.astype(BF16), moc.astype(BF16), a


def _s5_expanders():
    cw = S5_TC * LANES
    col = jnp.arange(cw)
    src_tj = (col // LANES) * S5_GROUP + col % S5_GROUP
    src_ap = (col // S5_SW) * S5_STATE + col % S5_STATE
    etj = (jnp.arange(S5_TC * S5_GROUP)[:, None] == src_tj[None, :]).astype(BF16)
    eap = (jnp.arange(2 * S5_STATE)[:, None] == src_ap[None, :]).astype(BF16)
    return etj, eap


def kernel(x, c, ctx, c_ctx, ada_w, ada_b, ev_w_in, ev_w_out, q_norm_w, k_norm_w, rg_conv_w, rg_conv_b, rg_wa, rg_ba, rg_wx, rg_bx, rg_lambda, od_w_in, s5_lambda_re, s5_lambda_im, s5_log_step, s5_b_re, s5_b_im, s5_c_re, s5_c_im, s5_d, glu_w, glu_b, od_w_out, final_norm_w):
    nb, seq, d = x.shape
    assert nb == SUBLANES and d == D_MODEL and ctx.shape[1] == CTX_LEN
    assert seq % ATT_TK == 0 and DEPTH % 2 == 0
    lt = CTX_LEN + seq

    xc = _to_rows(ctx, x)

    cond = jnp.zeros((16, d), F32).at[:nb].set(c).at[nb].set(c_ctx)
    mods = _modulation(cond, ada_w, ada_b).reshape(DEPTH, 16, 3, d)
    mods_x = mods[:, :nb].transpose(0, 2, 1, 3)
    mods_c = jnp.broadcast_to(mods[:, nb][:, :, None, :], mods_x.shape)
    mods = jnp.stack([mods_c, mods_x], axis=1)

    cos, sin = _rope_tables(lt)
    ones_q = _block_ones(ATT_WIDTH)
    ones_k = _block_ones(KV_WIDTH)
    etj, eap = _s5_expanders()
    fw = final_norm_w.reshape(1, d)
    out = None
    for layer in range(DEPTH):
        j = layer // 2
        ml = mods[layer]
        if layer % 2 == 0:
            w, wa, wr = _even_weights(ev_w_in[j], ev_w_out[j])
            gq = _rope_gains(q_norm_w[j], HEAD_DIM ** -0.5 * math.log2(math.e))
            gk = _rope_gains(k_norm_w[j], 1.0)
            q, k, v, ga, u, gr = _in_even(xc, ml, w, cos, sin, gq, gk, ones_q, ones_k)
            att = _attention(q, k, v)
            conv_b = rg_conv_b[j].reshape(1, RNN_WIDTH)
            y = None
            for direction, reverse in enumerate((False, True)):
                wg, bg = _rnn_gate_weights(rg_wa[j, direction], rg_ba[j, direction],
                                           rg_wx[j, direction], rg_bx[j, direction])
                clam = (-LRU_C * jax.nn.softplus(-rg_lambda[j, direction].astype(F32))).reshape(1, RNN_WIDTH)
                y = _rnn_dir(u, rg_conv_w[j], conv_b, wg, bg, clam, y, reverse)
            xc, u_odd, g_odd = _out_even_in_odd(att, ga, y, gr, xc, ml, wa, wr, mods[layer + 1],
                                                od_w_in[j].astype(BF16))
        else:
            u, g = u_odd, g_odd
            u3 = u.reshape(lt // S5_TC, S5_TC * nb, S5_WIDTH)
            y = jnp.tile(s5_d[j].astype(F32).reshape(S5_NB, 1, LANES), (1, S5_TC, 1)).reshape(1, -1)
            for direction, reverse in enumerate((False, True)):
                msc, mic, moc, a = _s5_weights(s5_lambda_re[j, direction], s5_lambda_im[j, direction],
                                               s5_log_step[j, direction], s5_b_re[j, direction],
                                               s5_b_im[j, direction], s5_c_re[j, direction],
                                               s5_c_im[j, direction], reverse)
                y = _s5_dir(u3, msc, mic, moc, etj, eap, a, y, reverse)
            y = y.reshape(lt * nb, S5_WIDTH)
            final = layer == DEPTH - 1
            res = _out_odd(y, g, xc, ml, glu_w[j].astype(BF16), glu_b[j].reshape(1, -1),
                           od_w_out[j].astype(BF16), fw, final)
            if final:
                out = res
            else:
                xc = res
    return out
```

```python
import functools
import math

import jax
import jax.numpy as jnp
from jax import lax
from jax.experimental import pallas as pl
from jax.experimental.pallas import tpu as pltpu

F32 = jnp.float32
BF16 = jnp.bfloat16

D_MODEL = 1024
DEPTH = 4
CTX_LEN = 256
GRID_W = 64
EPS = 1e-6

ATT_HEADS = 8
ATT_KV_HEADS = 2
HEAD_DIM = 64
ATT_GROUP = ATT_HEADS // ATT_KV_HEADS
ATT_WIDTH = ATT_HEADS * HEAD_DIM
KV_WIDTH = ATT_KV_HEADS * HEAD_DIM
ROPE_THETA = 10000.0

RNN_WIDTH = D_MODEL // 2
RNN_BLOCKS = 8
RNN_BLOCK_DIM = RNN_WIDTH // RNN_BLOCKS
LRU_C = 8.0

S5_WIDTH = D_MODEL
S5_GROUP = 16
S5_GROUPS = S5_WIDTH // S5_GROUP
S5_STATE = 64

SUBLANES = 8
LANES = 128
ROW_TT = 128
ROWS = ROW_TT * SUBLANES
ATT_TQ = 256
ATT_TK = 2048
ATT_VROWS = HEAD_DIM + 16
RNN_TT = 64
S5_TT = 256
S5_TC = 8
S5_GB = LANES // S5_GROUP
S5_NB = S5_GROUPS // S5_GB
S5_SW = S5_GB * S5_STATE
VMEM_LIMIT = 56 * 1024 * 1024


def _cparams(sem):
    return pltpu.CompilerParams(dimension_semantics=sem, vmem_limit_bytes=VMEM_LIMIT)


def _sigmoid(x):
    return 0.5 * jnp.tanh(0.5 * x) + 0.5


def _silu(x):
    return x * _sigmoid(x)


def _batch_rows(b):
    return pl.ds(b, ROW_TT, stride=SUBLANES)


def _mod_kernel(c_ref, w_ref, b_ref, o_ref):
    c = c_ref[...]
    sc = _silu(c).astype(BF16)
    o_ref[0] = jnp.dot(sc, w_ref[0].astype(BF16), preferred_element_type=F32) + b_ref[0]


def _modulation(cond, ada_w, ada_b):
    depth, d, d3 = ada_w.shape
    nblk = d3 // d
    return pl.pallas_call(
        _mod_kernel,
        grid=(depth, nblk),
        in_specs=[
            pl.BlockSpec((16, d), lambda l, n: (0, 0)),
            pl.BlockSpec((1, d, d), lambda l, n: (l, 0, n)),
            pl.BlockSpec((1, 1, d), lambda l, n: (l, 0, n)),
        ],
        out_specs=pl.BlockSpec((1, 16, d), lambda l, n: (l, 0, n)),
        out_shape=jax.ShapeDtypeStruct((depth, 16, d3), F32),
        compiler_params=_cparams(("parallel", "parallel")),
        name="ada_mod",
    )(cond, ada_w, ada_b.reshape(depth, 1, d3))


def _to_rows_kernel(c_ref, x_ref, o_ref, scr, *, n_ctx_tiles):
    i = pl.program_id(0)
    nblk = D_MODEL // LANES

    def move(src_ref):
        for b in range(SUBLANES):
            for j in range(nblk):
                scr[j, _batch_rows(b), :] = src_ref[b, :, LANES * j:LANES * (j + 1)]
        for j in range(nblk):
            o_ref[:, LANES * j:LANES * (j + 1)] = scr[j]

    @pl.when(i < n_ctx_tiles)
    def _():
        move(c_ref)

    @pl.when(i >= n_ctx_tiles)
    def _():
        move(x_ref)


def _per_batch(x, vec):
    r, d = x.shape
    return (x.reshape(r // SUBLANES, SUBLANES, d) * vec[None]).reshape(r, d)


def _norm_mod(x, mod_ref):
    ms = jnp.mean(x * x, axis=-1, keepdims=True)
    xn = x * lax.rsqrt(ms + EPS)
    r, d = x.shape
    x3 = xn.reshape(r // SUBLANES, SUBLANES, d)
    return (x3 * (1.0 + mod_ref[0, 1])[None] + mod_ref[0, 0][None]).reshape(r, d)


def _mod_spec(n_ctx_tiles, offset=0):
    return pl.BlockSpec((1, 3, SUBLANES, D_MODEL),
                        lambda i: (jnp.where(i + offset < n_ctx_tiles, 0, 1), 0, 0, 0))


def _group_mean_sq(y, ones_ref):
    sq = y * y
    hi = sq.astype(BF16)
    lo = (sq - hi.astype(F32)).astype(BF16)
    ones = ones_ref[...]
    s = jnp.dot(hi, ones, preferred_element_type=F32) + jnp.dot(lo, ones, preferred_element_type=F32)
    return s * (1.0 / HEAD_DIM)


def _in_even_kernel(x_ref, mod_ref, w_ref, cos_ref, sin_ref, gq_ref, gk_ref, oq_ref, ok_ref,
                    q_ref, k_ref, v_ref, ga_ref, u_ref, gr_ref, q_scr, kv_scr):
    h = _norm_mod(x_ref[...], mod_ref).astype(BF16)
    y = jnp.dot(h, w_ref[...], preferred_element_type=F32)
    aw, kw = ATT_WIDTH, KV_WIDTH
    cos = cos_ref[...]
    sin = sin_ref[...]
    lane = lax.broadcasted_iota(jnp.int32, (1, LANES), 1)
    lower = (lane % HEAD_DIM) < HEAD_DIM // 2

    def swap(t):
        return jnp.where(lower, pltpu.roll(t, LANES - HEAD_DIM // 2, axis=1), pltpu.roll(t, HEAD_DIM // 2, axis=1))

    def rotation(gain_ref):
        ca = gain_ref[0:1, :] * cos
        sa = gain_ref[1:2, :] * sin
        return lambda t: t * ca + swap(t) * sa

    rotate_q = rotation(gq_ref)
    rotate_k = rotation(gk_ref)

    qp = y[:, 0:aw]
    rq = lax.rsqrt(_group_mean_sq(qp, oq_ref) + EPS)
    for j in range(aw // LANES):
        sl = slice(LANES * j, LANES * (j + 1))
        q_scr[j] = rq[:, sl] * qp[:, sl]
    o = aw
    kp = y[:, o:o + kw]
    kv_scr[0] = lax.rsqrt(_group_mean_sq(kp, ok_ref) + EPS) * kp
    o += kw
    kv_scr[1] = y[:, o:o + kw]
    o += kw
    for b in range(SUBLANES):
        for j in range(aw // LANES):
            q_ref[b, :, LANES * j:LANES * (j + 1)] = rotate_q(q_scr[j, _batch_rows(b), :]).astype(BF16)
        k_ref[b] = rotate_k(kv_scr[0, _batch_rows(b), :]).astype(BF16)
        v_ref[b] = kv_scr[1, _batch_rows(b), :].astype(BF16)
    ga_ref[...] = y[:, o:o + aw].astype(BF16)
    o += aw
    u_ref[...] = y[:, o:o + RNN_WIDTH]
    o += RNN_WIDTH
    gr_ref[...] = y[:, o:o + RNN_WIDTH].astype(BF16)


def _in_even_first_kernel(c_ref, xb_ref, *refs, n_ctx_tiles):
    *main, xrow_ref, q_scr, kv_scr, x_scr = refs
    _to_rows_kernel(c_ref, xb_ref, xrow_ref, x_scr, n_ctx_tiles=n_ctx_tiles)
    _in_even_kernel(xrow_ref, *main, q_scr, kv_scr)


def _in_even(src, mods, w, cos, sin, gq, gk, ones_q, ones_k):
    first = isinstance(src, tuple)
    d = D_MODEL
    if first:
        ctx, x = src
        nb, seq, _ = x.shape
        nct = ctx.shape[1] // ROW_TT
        nrows = (ctx.shape[1] + seq) * nb
    else:
        nrows = src.shape[0]
    nt = nrows // ROWS
    lt = nrows // SUBLANES
    aw, kw, rw = ATT_WIDTH, KV_WIDTH, RNN_WIDTH
    row = lambda i: (i, 0)
    const = lambda i: (0, 0)
    per_batch = lambda i: (0, i, 0)
    if first:
        kern = functools.partial(_in_even_first_kernel, n_ctx_tiles=nct)
        src_specs = [pl.BlockSpec((nb, ROW_TT, d), lambda i: (0, jnp.minimum(i, nct - 1), 0)),
                     pl.BlockSpec((nb, ROW_TT, d), lambda i: (0, jnp.maximum(i - nct, 0), 0))]
        extra_out_specs = [pl.BlockSpec((ROWS, d), row)]
        extra_out_shape = [jax.ShapeDtypeStruct((nrows, d), F32)]
        extra_scratch = [pltpu.VMEM((d // LANES, ROWS, LANES), F32)]
        args = (ctx, x)
    else:
        kern = _in_even_kernel
        src_specs = [pl.BlockSpec((ROWS, d), row)]
        extra_out_specs, extra_out_shape, extra_scratch = [], [], []
        args = (src,)
    return pl.pallas_call(
        kern,
        grid=(nt,),
        in_specs=src_specs + [
            _mod_spec(CTX_LEN // ROW_TT),
            pl.BlockSpec(w.shape, const),
            pl.BlockSpec((ROW_TT, LANES), row),
            pl.BlockSpec((ROW_TT, LANES), row),
            pl.BlockSpec(gq.shape, const),
            pl.BlockSpec(gk.shape, const),
            pl.BlockSpec(ones_q.shape, const),
            pl.BlockSpec(ones_k.shape, const),
        ],
        out_specs=[
            pl.BlockSpec((SUBLANES, ROW_TT, aw), per_batch),
            pl.BlockSpec((SUBLANES, ROW_TT, kw), per_batch),
            pl.BlockSpec((SUBLANES, ROW_TT, kw), per_batch),
            pl.BlockSpec((ROWS, aw), row),
            pl.BlockSpec((ROWS, rw), row),
            pl.BlockSpec((ROWS, rw), row),
        ] + extra_out_specs,
        out_shape=[
            jax.ShapeDtypeStruct((SUBLANES, lt, aw), BF16),
            jax.ShapeDtypeStruct((SUBLANES, lt, kw), BF16),
            jax.ShapeDtypeStruct((SUBLANES, lt, kw), BF16),
            jax.ShapeDtypeStruct((nrows, aw), BF16),
            jax.ShapeDtypeStruct((nrows, rw), F32),
            jax.ShapeDtypeStruct((nrows, rw), BF16),
        ] + extra_out_shape,
        scratch_shapes=[pltpu.VMEM((aw // LANES, ROWS, LANES), F32),
                        pltpu.VMEM((2, ROWS, LANES), F32)] + extra_scratch,
        compiler_params=_cparams(("parallel",)),
        name="even_in_proj_first" if first else "even_in_proj",
    )(*args, mods, w, cos, sin, gq, gk, ones_q, ones_k)


def _attn_kernel(q_ref, k_ref, v_ref, o_ref, vt_scr, s_scr, acc_scr, *, n_ctx_tiles, n_x_chunks):
    qt = pl.program_id(1)
    lane = lax.broadcasted_iota(jnp.int32, (1, LANES), 1)
    tq = ATT_TQ
    cols = ATT_GROUP * tq
    lt = v_ref.shape[1]
    n = n_x_chunks
    kt = 4 * LANES

    @pl.when(qt == 0)
    def _():
        def fill(i, carry):
            rs = pl.ds(pl.multiple_of(i * CTX_LEN, CTX_LEN), CTX_LEN)
            vt = v_ref[0, rs, :].astype(F32).T
            for h in range(ATT_KV_HEADS):
                vt_scr[h, 0:HEAD_DIM, rs] = vt[HEAD_DIM * h:HEAD_DIM * (h + 1)].astype(BF16)
                vt_scr[h, HEAD_DIM:ATT_VROWS, rs] = jnp.ones((ATT_VROWS - HEAD_DIM, CTX_LEN), BF16)
            return carry
        lax.fori_loop(0, lt // CTX_LEN, fill, 0)

    def key_range(c):
        return (CTX_LEN + c * ATT_TK, ATT_TK) if c < n else (0, CTX_LEN)

    for h in range(ATT_KV_HEADS):
        in_head = (lane // HEAD_DIM) == h
        qs = jnp.concatenate(
            [jnp.where(in_head, q_ref[0, :, LANES * g:LANES * (g + 1)], jnp.zeros((), BF16))
             for g in range(ATT_GROUP)], axis=0)
        qst = qs.astype(F32).T.astype(BF16)

        def scores(rng, slot, qst=qst):
            start, width = rng
            s_scr[slot, 0:width, :] = jnp.dot(k_ref[0, start:start + width, :], qst,
                                              preferred_element_type=F32)
            return jnp.max(s_scr[slot, 0:width, :], axis=0, keepdims=True)

        def consume(rng, slot, m, mloc, acc, h=h):
            start, width = rng
            m_new = jnp.maximum(m, mloc)
            acc = acc * jnp.exp2(m - m_new)
            kw = min(kt, width)
            for t in range(width // kw):
                p = jnp.exp2(s_scr[slot, kw * t:kw * (t + 1), :] - m_new).astype(BF16)
                acc = acc + jnp.dot(vt_scr[h, :, start + kw * t:start + kw * (t + 1)], p,
                                    preferred_element_type=F32)
            return m_new, acc

        zero = jnp.zeros((ATT_VROWS, cols), F32)

        @pl.when(qt < n_ctx_tiles)
        def _(h=h, scores=scores, consume=consume):
            mloc = scores((0, CTX_LEN), 0)
            acc_scr[h] = consume((0, CTX_LEN), 0, mloc, mloc, zero)[1]

        @pl.when(qt >= n_ctx_tiles)
        def _(h=h, scores=scores, consume=consume):
            mloc = scores(key_range(0), 0)
            m, acc = mloc, zero
            for i in range(n + 1):
                mloc_next = scores(key_range(i + 1), (i + 1) % 2) if i + 1 <= n else None
                m, acc = consume(key_range(i), i % 2, m, mloc, acc)
                mloc = mloc_next
            acc_scr[h] = acc

    first = lane < HEAD_DIM
    pad = jnp.zeros((LANES - ATT_VROWS, tq), F32)
    for g in range(ATT_GROUP):
        cs = slice(tq * g, tq * (g + 1))
        outs = []
        for h in range(ATT_KV_HEADS):
            acc = jnp.concatenate([acc_scr[h, :, cs], pad], axis=0).T
            outs.append(acc / acc[:, HEAD_DIM:HEAD_DIM + 1])
        o_ref[0, :, LANES * g:LANES * (g + 1)] = jnp.where(
            first, outs[0], pltpu.roll(outs[1], HEAD_DIM, axis=1))


def _attention(q, k, v):
    nb, lt, _ = q.shape
    nq = lt // ATT_TQ
    cols = ATT_GROUP * ATT_TQ
    kern = functools.partial(_attn_kernel, n_ctx_tiles=CTX_LEN // ATT_TQ,
                             n_x_chunks=(lt - CTX_LEN) // ATT_TK)
    return pl.pallas_call(
        kern,
        grid=(nb, nq),
        in_specs=[
            pl.BlockSpec((1, ATT_TQ, ATT_WIDTH), lambda b, i: (b, i, 0)),
            pl.BlockSpec((1, lt, KV_WIDTH), lambda b, i: (b, 0, 0)),
            pl.BlockSpec((1, lt, KV_WIDTH), lambda b, i: (b, 0, 0)),
        ],
        out_specs=pl.BlockSpec((1, ATT_TQ, ATT_WIDTH), lambda b, i: (b, i, 0)),
        out_shape=jax.ShapeDtypeStruct((nb, lt, ATT_WIDTH), F32),
        scratch_shapes=[
            pltpu.VMEM((ATT_KV_HEADS, ATT_VROWS, lt), BF16),
            pltpu.VMEM((2, ATT_TK, cols), F32),
            pltpu.VMEM((ATT_KV_HEADS, ATT_VROWS, cols), F32),
        ],
        compiler_params=_cparams(("arbitrary", "arbitrary")),
        name="attention",
    )(q, k, v)


def _scan_tile(j, reverse, n_ctx_tiles, n_tiles):
    if not reverse:
        return j
    return jnp.where(j < n_ctx_tiles, n_ctx_tiles - 1 - j, n_tiles + n_ctx_tiles - 1 - j)


def _rnn_kernel(*refs, reverse, n_ctx_tiles, n_tiles, lt):
    if reverse:
        (u_ref, prev_ref, next_ref, cw_ref, cb_ref, wg_ref, bg_ref, clam_ref, yin_ref,
         y_ref, a_scr, b_scr, h_scr) = refs
    else:
        (u_ref, prev_ref, next_ref, cw_ref, cb_ref, wg_ref, bg_ref, clam_ref,
         y_ref, a_scr, b_scr, h_scr) = refs
        yin_ref = None
    j = pl.program_id(0)
    tile = _scan_tile(j, reverse, n_ctx_tiles, n_tiles)
    rows = RNN_TT * SUBLANES
    rw = RNN_WIDTH

    @pl.when(j == 0)
    def _():
        h_scr[...] = jnp.zeros_like(h_scr)

    t0 = tile * RNN_TT
    t1 = t0 + RNN_TT
    has_prev = jnp.logical_and(t0 != 0, t0 != CTX_LEN)
    has_next = jnp.logical_and(t1 != CTX_LEN, t1 != lt)
    u = u_ref[...]
    prev = jnp.where(has_prev, prev_ref[...], 0.0)
    nxt = jnp.where(has_next, next_ref[0:SUBLANES, :], 0.0)
    ext = jnp.concatenate([prev, u, nxt], axis=0)
    cw = cw_ref[...]
    s = SUBLANES
    cv = (cw[0:1] * ext[0:rows] + cw[1:2] * ext[s:rows + s] + cw[2:3] * ext[2 * s:rows + 2 * s]
          + cw[3:4] * ext[3 * s:rows + 3 * s] + cb_ref[...])
    g = jnp.dot(cv.astype(BF16), wg_ref[...], preferred_element_type=F32) + bg_ref[...]
    r = _sigmoid(g[:, 0:rw])
    i = _sigmoid(g[:, rw:2 * rw])
    log_a = clam_ref[...] * r
    a_scr[...] = jnp.exp(log_a)
    th = jnp.tanh(log_a)
    b_scr[...] = jnp.sqrt(-2.0 * th / (1.0 - th)) * (i * cv)

    def step(k, h):
        t = (RNN_TT - 1 - k) if reverse else k
        rs = pl.ds(pl.multiple_of(t * SUBLANES, SUBLANES), SUBLANES)
        h = a_scr[rs, :] * h + b_scr[rs, :]
        if reverse:
            y_ref[rs, :] = h + yin_ref[rs, :]
        else:
            y_ref[rs, :] = h
        return h

    h_scr[...] = lax.fori_loop(0, RNN_TT, step, h_scr[...], unroll=8)


def _rnn_dir(u, conv_w, conv_b, wg, bg, clam, yin, reverse):
    nrows = u.shape[0]
    rows = RNN_TT * SUBLANES
    n_tiles = nrows // rows
    n_ctx = CTX_LEN // RNN_TT
    lt = nrows // SUBLANES
    halo = 2 * SUBLANES
    per = rows // halo
    nhalo = nrows // halo
    tile_of = lambda j: _scan_tile(j, reverse, n_ctx, n_tiles)
    rw = RNN_WIDTH
    const = lambda j: (0, 0)
    in_specs = [
        pl.BlockSpec((rows, rw), lambda j: (tile_of(j), 0)),
        pl.BlockSpec((halo, rw), lambda j: (jnp.maximum(tile_of(j) * per - 1, 0), 0)),
        pl.BlockSpec((halo, rw), lambda j: (jnp.minimum((tile_of(j) + 1) * per, nhalo - 1), 0)),
        pl.BlockSpec((4, rw), const),
        pl.BlockSpec((1, rw), const),
        pl.BlockSpec((rw, 2 * rw), const),
        pl.BlockSpec((1, 2 * rw), const),
        pl.BlockSpec((1, rw), const),
    ]
    args = [u, u, u, conv_w, conv_b, wg, bg, clam]
    if reverse:
        in_specs.append(pl.BlockSpec((rows, rw), lambda j: (tile_of(j), 0)))
        args.append(yin)
    kern = functools.partial(_rnn_kernel, reverse=reverse, n_ctx_tiles=n_ctx, n_tiles=n_tiles, lt=lt)
    return pl.pallas_call(
        kern,
        grid=(n_tiles,),
        in_specs=in_specs,
        out_specs=pl.BlockSpec((rows, rw), lambda j: (tile_of(j), 0)),
        out_shape=jax.ShapeDtypeStruct((nrows, rw), F32),
        scratch_shapes=[pltpu.VMEM((rows, rw), F32), pltpu.VMEM((rows, rw), F32),
                        pltpu.VMEM((SUBLANES, rw), F32)],
        compiler_params=_cparams(("arbitrary",)),
        name="rglru_bwd" if reverse else "rglru_fwd",
    )(*args)


def _out_even_in_odd_kernel(att_ref, ga_ref, y_ref, gr_ref, x_ref, mod_ref, wa_ref, wr_ref,
                            nmod_ref, wn_ref, o_ref, u_ref, g_ref, a_scr):
    nblk = ATT_WIDTH // LANES
    for b in range(SUBLANES):
        for j in range(nblk):
            a_scr[j, _batch_rows(b), :] = att_ref[b, :, LANES * j:LANES * (j + 1)]
    att = jnp.concatenate([a_scr[j] for j in range(nblk)], axis=1)
    m1 = (att * _silu(ga_ref[...].astype(F32))).astype(BF16)
    m2 = (y_ref[...] * _silu(gr_ref[...].astype(F32))).astype(BF16)
    o = (jnp.dot(m1, wa_ref[...], preferred_element_type=F32)
         + jnp.dot(m2, wr_ref[...], preferred_element_type=F32))
    xn = x_ref[...] + _per_batch(o, mod_ref[0, 2])
    o_ref[...] = xn
    h = _norm_mod(xn, nmod_ref).astype(BF16)
    z = jnp.dot(h, wn_ref[...], preferred_element_type=F32)
    u_ref[...] = z[:, 0:S5_WIDTH]
    g_ref[...] = z[:, S5_WIDTH:2 * S5_WIDTH].astype(BF16)


def _out_even_in_odd(att, ga, y, gr, xc, mods, wa, wr, next_mods, wn):
    nrows = xc.shape[0]
    nt = nrows // ROWS
    d = D_MODEL
    row = lambda i: (i, 0)
    const = lambda i: (0, 0)
    return pl.pallas_call(
        _out_even_in_odd_kernel,
        grid=(nt,),
        in_specs=[
            pl.BlockSpec((SUBLANES, ROW_TT, ATT_WIDTH), lambda i: (0, i, 0)),
            pl.BlockSpec((ROWS, ATT_WIDTH), row),
            pl.BlockSpec((ROWS, RNN_WIDTH), row),
            pl.BlockSpec((ROWS, RNN_WIDTH), row),
            pl.BlockSpec((ROWS, d), row),
            _mod_spec(CTX_LEN // ROW_TT),
            pl.BlockSpec(wa.shape, const),
            pl.BlockSpec(wr.shape, const),
            _mod_spec(CTX_LEN // ROW_TT),
            pl.BlockSpec(wn.shape, const),
        ],
        out_specs=[pl.BlockSpec((ROWS, d), row), pl.BlockSpec((ROWS, S5_WIDTH), row),
                   pl.BlockSpec((ROWS, S5_WIDTH), row)],
        out_shape=[jax.ShapeDtypeStruct(xc.shape, F32),
                   jax.ShapeDtypeStruct((nrows, S5_WIDTH), F32),
                   jax.ShapeDtypeStruct((nrows, S5_WIDTH), BF16)],
        scratch_shapes=[pltpu.VMEM((ATT_WIDTH // LANES, ROWS, LANES), F32)],
        compiler_params=_cparams(("parallel",)),
        name="even_out_odd_in_proj",
    )(att, ga, y, gr, xc, mods, wa, wr, next_mods, wn)


def _s5_kernel(u_ref, msc_ref, mic_ref, moc_ref, etj_ref, eap_ref, a_ref, extra_ref, y_ref,
               ms_scr, mio_scr, hp_scr, h_scr, *, reverse):
    j = pl.program_id(1)
    nc = S5_TT // S5_TC
    rows = nc * SUBLANES
    sw = S5_SW

    @pl.when(j == 0)
    def _():
        h_scr[...] = jnp.zeros_like(h_scr)
        cw = S5_TC * LANES
        r = lax.broadcasted_iota(jnp.int32, (cw, 1), 0)
        c = lax.broadcasted_iota(jnp.int32, (1, cw), 1)
        row_group_ch = (r // S5_GROUP) % S5_GB
        row_group_st = (r // S5_STATE) % S5_GB
        col_group_ch = (c // S5_GROUP) % S5_GB
        col_group_st = (c // S5_STATE) % S5_GB

        def expand(compact_ref, e_ref, keep):
            full = jnp.dot(compact_ref[0], e_ref[...], preferred_element_type=F32)
            return jnp.where(keep, full, 0.0).astype(BF16)

        ms_scr[...] = expand(msc_ref, eap_ref, row_group_ch == col_group_st)
        mio_scr[0:cw, :] = expand(mic_ref, etj_ref, row_group_ch == col_group_ch)
        mio_scr[cw:2 * cw, :] = expand(moc_ref, etj_ref, row_group_st == col_group_ch)

    def chunk_major(ref):
        return jnp.concatenate(
            [ref[:, SUBLANES * t:SUBLANES * (t + 1), :].reshape(rows, LANES) for t in range(S5_TC)], axis=1)

    lhs32 = chunk_major(u_ref)
    lhs = lhs32.astype(BF16)
    local = jnp.dot(lhs, ms_scr[...], preferred_element_type=F32)
    ar = a_ref[0, :, 0:sw]
    ai = a_ref[0, :, sw:2 * sw]
    hr = h_scr[:, 0:sw]
    hi = h_scr[:, sw:2 * sw]
    for c in (range(nc - 1, -1, -1) if reverse else range(nc)):
        rs = slice(SUBLANES * c, SUBLANES * (c + 1))
        hp_scr[rs, 0:sw] = hr
        hp_scr[rs, sw:2 * sw] = hi
        hr, hi = (ar * hr - ai * hi + local[rs, 0:sw], ar * hi + ai * hr + local[rs, sw:2 * sw])
    h_scr[:, 0:sw] = hr
    h_scr[:, sw:2 * sw] = hi
    cw = S5_TC * LANES
    tile = 2 * LANES
    hp = hp_scr[...].astype(BF16)
    parts = []
    for b in range(cw // tile):
        cs = slice(tile * b, tile * (b + 1))
        ks = slice(tile * b, cw) if reverse else slice(0, tile * (b + 1))
        parts.append(jnp.dot(lhs[:, ks], mio_scr[ks, cs], preferred_element_type=F32)
                     + jnp.dot(hp, mio_scr[cw:2 * cw, cs], preferred_element_type=F32))
    y = jnp.concatenate(parts, axis=1)
    if reverse:
        y = y + chunk_major(extra_ref)
    else:
        y = y + extra_ref[...] * lhs32
    for t in range(S5_TC):
        y_ref[:, SUBLANES * t:SUBLANES * (t + 1), :] = (
            y[:, LANES * t:LANES * (t + 1)].reshape(nc, SUBLANES, LANES))


def _s5_dir(u3, msc, mic, moc, etj, eap, a, extra, reverse):
    nchunks, crow, width = u3.shape
    nc = S5_TT // S5_TC
    n_tiles = nchunks // nc
    n_ctx = CTX_LEN // S5_TT
    tile_of = lambda j: _scan_tile(j, reverse, n_ctx, n_tiles)
    blk = lambda k, j: (tile_of(j), 0, k)
    wblk = lambda k, j: (k, 0, 0)
    cw = S5_TC * LANES
    if reverse:
        extra_spec = pl.BlockSpec((nc, crow, LANES), blk)
    else:
        extra_spec = pl.BlockSpec((1, cw), lambda k, j: (0, k))
    return pl.pallas_call(
        functools.partial(_s5_kernel, reverse=reverse),
        grid=(S5_NB, n_tiles),
        in_specs=[
            pl.BlockSpec((nc, crow, LANES), blk),
            pl.BlockSpec((1, cw, LANES), wblk),
            pl.BlockSpec((1, cw, LANES), wblk),
            pl.BlockSpec((1, cw, LANES), wblk),
            pl.BlockSpec(etj.shape, lambda k, j: (0, 0)),
            pl.BlockSpec(eap.shape, lambda k, j: (0, 0)),
            pl.BlockSpec((1, SUBLANES, 2 * S5_SW), wblk),
            extra_spec,
        ],
        out_specs=pl.BlockSpec((nc, crow, LANES), blk),
        out_shape=jax.ShapeDtypeStruct(u3.shape, F32),
        scratch_shapes=[pltpu.VMEM((cw, 2 * S5_SW), BF16), pltpu.VMEM((cw + 2 * S5_SW, cw), BF16),
                        pltpu.VMEM((nc * SUBLANES, 2 * S5_SW), F32), pltpu.VMEM((SUBLANES, 2 * S5_SW), F32)],
        compiler_params=_cparams(("arbitrary", "arbitrary")),
        name="s5_bwd" if reverse else "s5_fwd",
    )(u3, msc, mic, moc, etj, eap, a, extra)


def _out_odd_kernel(y_ref, g_ref, x_ref, mod_ref, gw_ref, gb_ref, wo_ref, fw_ref, o_ref, *scr, final):
    yy = jax.nn.gelu(y_ref[...]).astype(BF16)
    z = jnp.dot(yy, gw_ref[...], preferred_element_type=F32) + gb_ref[...]
    g = g_ref[...].astype(F32)
    m = z[:, 0:S5_WIDTH] * _sigmoid(z[:, S5_WIDTH:2 * S5_WIDTH]) * _silu(g)
    o = jnp.dot(m.astype(BF16), wo_ref[...], preferred_element_type=F32)
    xn = x_ref[...] + _per_batch(o, mod_ref[0, 2])
    if final:
        (o_scr,) = scr
        ms = jnp.mean(xn * xn, axis=-1, keepdims=True)
        xo = xn * lax.rsqrt(ms + EPS) * fw_ref[...]
        nblk = D_MODEL // LANES
        for j in range(nblk):
            o_scr[j] = xo[:, LANES * j:LANES * (j + 1)]
        for b in range(SUBLANES):
            for j in range(nblk):
                o_ref[b, :, LANES * j:LANES * (j + 1)] = o_scr[j, _batch_rows(b), :]
    else:
        o_ref[...] = xn


def _out_odd(y, g, xc, mods, gw, gb, wo, fw, final):
    nrows = xc.shape[0]
    d = D_MODEL
    nct = CTX_LEN // ROW_TT
    lt = nrows // SUBLANES
    if final:
        nt = (lt - CTX_LEN) // ROW_TT
        row = lambda i: (i + nct, 0)
        mod_spec = _mod_spec(nct, offset=nct)
        out_spec = pl.BlockSpec((SUBLANES, ROW_TT, d), lambda i: (0, i, 0))
        out_shape = jax.ShapeDtypeStruct((SUBLANES, lt - CTX_LEN, d), F32)
        scratch = [pltpu.VMEM((d // LANES, ROWS, LANES), F32)]
    else:
        nt = lt // ROW_TT
        row = lambda i: (i, 0)
        mod_spec = _mod_spec(nct)
        out_spec = pl.BlockSpec((ROWS, d), row)
        out_shape = jax.ShapeDtypeStruct(xc.shape, F32)
        scratch = []
    const = lambda i: (0, 0)
    return pl.pallas_call(
        functools.partial(_out_odd_kernel, final=final),
        grid=(nt,),
        in_specs=[
            pl.BlockSpec((ROWS, S5_WIDTH), row),
            pl.BlockSpec((ROWS, S5_WIDTH), row),
            pl.BlockSpec((ROWS, d), row),
            mod_spec,
            pl.BlockSpec(gw.shape, const),
            pl.BlockSpec(gb.shape, const),
            pl.BlockSpec(wo.shape, const),
            pl.BlockSpec(fw.shape, const),
        ],
        out_specs=out_spec,
        out_shape=out_shape,
        scratch_shapes=scratch,
        compiler_params=_cparams(("parallel",)),
        name="odd_out_proj_final" if final else "odd_out_proj",
    )(y, g, xc, mods, gw, gb, wo, fw)


def _rope_tables(lt):
    n = lt - CTX_LEN
    rows = n // GRID_W
    row = jnp.repeat(jnp.arange(rows, dtype=F32), GRID_W)
    col = jnp.tile(jnp.arange(GRID_W, dtype=F32), rows)
    n_freq = HEAD_DIM // 4
    inv = ROPE_THETA ** (-jnp.arange(n_freq, dtype=F32) / n_freq)
    ang = jnp.concatenate([row[:, None] * inv, col[:, None] * inv], axis=-1)
    half = HEAD_DIM // 2
    cos = jnp.concatenate([jnp.ones((CTX_LEN, half), F32), jnp.cos(ang)], axis=0)
    sin = jnp.concatenate([jnp.zeros((CTX_LEN, half), F32), jnp.sin(ang)], axis=0)
    reps = LANES // HEAD_DIM
    ct = jnp.tile(jnp.concatenate([cos, cos], axis=-1), (1, reps))
    st = jnp.tile(jnp.concatenate([-sin, sin], axis=-1), (1, reps))
    return ct, st


def _rope_gains(gain, scale):
    g_eo = jnp.concatenate([gain[0::2], gain[1::2]])
    g_oe = jnp.concatenate([gain[1::2], gain[0::2]])
    reps = LANES // HEAD_DIM
    return jnp.stack([jnp.tile(g_eo, reps), jnp.tile(g_oe, reps)]).astype(F32) * scale


def _even_weights(w_in, w_out):
    aw, kw, rw = ATT_WIDTH, KV_WIDTH, RNN_WIDTH
    eo = jnp.concatenate([jnp.arange(0, HEAD_DIM, 2), jnp.arange(1, HEAD_DIM, 2)])
    slot_head = jnp.array([h * ATT_GROUP + g for g in range(ATT_GROUP) for h in range(ATT_KV_HEADS)])
    q_cols = (slot_head[:, None] * HEAD_DIM + eo[None, :]).reshape(-1)
    kv_heads = jnp.arange(ATT_KV_HEADS)
    k_cols = aw + (kv_heads[:, None] * HEAD_DIM + eo[None, :]).reshape(-1)
    v_cols = aw + kw + jnp.arange(kw)
    slot_cols = (slot_head[:, None] * HEAD_DIM + jnp.arange(HEAD_DIM)[None, :]).reshape(-1)
    ga_cols = aw + 2 * kw + slot_cols
    u_cols = 2 * aw + 2 * kw + jnp.arange(rw)
    gr_cols = 2 * aw + 2 * kw + rw + jnp.arange(rw)
    cols = jnp.concatenate([q_cols, k_cols, v_cols, ga_cols, u_cols, gr_cols])
    w = w_in[:, cols].astype(BF16)
    wa = w_out[:aw][slot_cols].astype(BF16)
    wr = w_out[aw:].astype(BF16)
    return w, wa, wr


def _block_ones(width):
    idx = jnp.arange(width) // HEAD_DIM
    return (idx[:, None] == idx[None, :]).astype(BF16)


def _rnn_gate_weights(wa, ba, wx, bx):
    eye = jnp.eye(RNN_BLOCKS, dtype=F32)

    def dense(w):
        return jnp.einsum('hij,hk->hikj', w, eye).reshape(RNN_WIDTH, RNN_WIDTH)

    wg = jnp.concatenate([dense(wa), dense(wx)], axis=1).astype(BF16)
    bg = jnp.concatenate([ba.reshape(-1), bx.reshape(-1)])[None, :]
    return wg, bg


def _s5_discretize(lam_re, lam_im, log_step, b_re, b_im):
    dt = jnp.exp(log_step)[:, None]
    mag = jnp.exp(lam_re * dt)
    ab_re = mag * jnp.cos(lam_im * dt)
    ab_im = mag * jnp.sin(lam_im * dt)
    den = lam_re * lam_re + lam_im * lam_im
    nr, ni = ab_re - 1.0, ab_im
    f_re = (nr * lam_re + ni * lam_im) / den
    f_im = (ni * lam_re - nr * lam_im) / den
    bb_re = f_re[..., None] * b_re - f_im[..., None] * b_im
    bb_im = f_re[..., None] * b_im + f_im[..., None] * b_re
    return ab_re, ab_im, bb_re, bb_im


def _s5_weights(lam_re, lam_im, log_step, b_re, b_im, c_re, c_im, reverse):
    hp = lax.Precision.HIGHEST
    f = lambda t: t.astype(F32)
    lam_re, lam_im, log_step, b_re, b_im, c_re, c_im = map(f, (lam_re, lam_im, log_step, b_re, b_im, c_re, c_im))
    tc, nbk, gb, p, c = S5_TC, S5_NB, S5_GB, S5_STATE, S5_GROUP
    _, _, bb_re, bb_im = _s5_discretize(lam_re, lam_im, log_step, b_re, b_im)
    dt = jnp.exp(log_step)[:, None]
    e = jnp.arange(tc + 1, dtype=F32)[:, None, None]
    mag = jnp.exp(e * (lam_re * dt))
    pw_re = mag * jnp.cos(e * (lam_im * dt))
    pw_im = mag * jnp.sin(e * (lam_im * dt))
    c_re_t = c_re.transpose(0, 2, 1)
    c_im_t = c_im.transpose(0, 2, 1)
    ca_re = c_re_t[None] * pw_re[..., None] - c_im_t[None] * pw_im[..., None]
    ca_im = c_re_t[None] * pw_im[..., None] + c_im_t[None] * pw_re[..., None]
    kern = (jnp.einsum('tgpj,gpi->tgij', ca_re[:tc], bb_re, precision=hp)
            - jnp.einsum('tgpj,gpi->tgij', ca_im[:tc], bb_im, precision=hp))
    none = jnp.zeros_like(kern[0])

    def lagged(s, t):
        lag = (s - t) if reverse else (t - s)
        return kern[lag] if lag >= 0 else none

    mic = jnp.stack([jnp.concatenate([lagged(s, t) for t in range(tc)], axis=-1) for s in range(tc)], 0)
    bb_re_t = bb_re.transpose(0, 2, 1)
    bb_im_t = bb_im.transpose(0, 2, 1)

    def leaving(s):
        e_s = s if reverse else tc - 1 - s
        ar, ai = pw_re[e_s][:, None, :], pw_im[e_s][:, None, :]
        return jnp.concatenate([ar * bb_re_t - ai * bb_im_t, ar * bb_im_t + ai * bb_re_t], axis=-1)

    msc = jnp.stack([leaving(s) for s in range(tc)], 0)
    e_t = [(tc - t) if reverse else (t + 1) for t in range(tc)]
    moc = jnp.stack([jnp.concatenate([ca_re[e] for e in e_t], axis=-1),
                     jnp.concatenate([-ca_im[e] for e in e_t], axis=-1)], 0)

    def blocked(m):
        lead, _, x, w = m.shape
        return m.reshape(lead, nbk, gb, x, w).transpose(1, 0, 2, 3, 4).reshape(nbk, lead * gb * x, w)

    mic, msc, moc = blocked(mic), blocked(msc), blocked(moc)
    a = jnp.stack([pw_re[tc], pw_im[tc]], 0).reshape(2, nbk, gb, p).transpose(1, 0, 2, 3).reshape(nbk, 1, 2 * gb * p)
    a = jnp.broadcast_to(a, (nbk, SUBLANES, 2 * gb * p))
    return msc.astype(BF16), mic.astype(BF16), moc.astype(BF16), a


def _s5_expanders():
    cw = S5_TC * LANES
    col = jnp.arange(cw)
    src_tj = (col // LANES) * S5_GROUP + col % S5_GROUP
    src_ap = (col // S5_SW) * S5_STATE + col % S5_STATE
    etj = (jnp.arange(S5_TC * S5_GROUP)[:, None] == src_tj[None, :]).astype(BF16)
    eap = (jnp.arange(2 * S5_STATE)[:, None] == src_ap[None, :]).astype(BF16)
    return etj, eap


def kernel(x, c, ctx, c_ctx, ada_w, ada_b, ev_w_in, ev_w_out, q_norm_w, k_norm_w, rg_conv_w, rg_conv_b, rg_wa, rg_ba, rg_wx, rg_bx, rg_lambda, od_w_in, s5_lambda_re, s5_lambda_im, s5_log_step, s5_b_re, s5_b_im, s5_c_re, s5_c_im, s5_d, glu_w, glu_b, od_w_out, final_norm_w):
    nb, seq, d = x.shape
    assert nb == SUBLANES and d == D_MODEL and ctx.shape[1] == CTX_LEN
    assert seq % ATT_TK == 0 and DEPTH % 2 == 0
    lt = CTX_LEN + seq

    xc = None

    cond = jnp.zeros((16, d), F32).at[:nb].set(c).at[nb].set(c_ctx)
    mods = _modulation(cond, ada_w, ada_b).reshape(DEPTH, 16, 3, d)
    mods_x = mods[:, :nb].transpose(0, 2, 1, 3)
    mods_c = jnp.broadcast_to(mods[:, nb][:, :, None, :], mods_x.shape)
    mods = jnp.stack([mods_c, mods_x], axis=1)

    cos, sin = _rope_tables(lt)
    ones_q = _block_ones(ATT_WIDTH)
    ones_k = _block_ones(KV_WIDTH)
    etj, eap = _s5_expanders()
    fw = final_norm_w.reshape(1, d)
    out = None
    for layer in range(DEPTH):
        j = layer // 2
        ml = mods[layer]
        if layer % 2 == 0:
            w, wa, wr = _even_weights(ev_w_in[j], ev_w_out[j])
            gq = _rope_gains(q_norm_w[j], HEAD_DIM ** -0.5 * math.log2(math.e))
            gk = _rope_gains(k_norm_w[j], 1.0)
            if layer == 0:
                q, k, v, ga, u, gr, xc = _in_even((ctx, x), ml, w, cos, sin, gq, gk, ones_q, ones_k)
            else:
                q, k, v, ga, u, gr = _in_even(xc, ml, w, cos, sin, gq, gk, ones_q, ones_k)
            att = _attention(q, k, v)
            conv_b = rg_conv_b[j].reshape(1, RNN_WIDTH)
            y = None
            for direction, reverse in enumerate((False, True)):
                wg, bg = _rnn_gate_weights(rg_wa[j, direction], rg_ba[j, direction],
                                           rg_wx[j, direction], rg_bx[j, direction])
                clam = (-LRU_C * jax.nn.softplus(-rg_lambda[j, direction].astype(F32))).reshape(1, RNN_WIDTH)
                y = _rnn_dir(u, rg_conv_w[j], conv_b, wg, bg, clam, y, reverse)
            xc, u_odd, g_odd = _out_even_in_odd(att, ga, y, gr, xc, ml, wa, wr, mods[layer + 1],
                                                od_w_in[j].astype(BF16))
        else:
            u, g = u_odd, g_odd
            u3 = u.reshape(lt // S5_TC, S5_TC * nb, S5_WIDTH)
            y = jnp.tile(s5_d[j].astype(F32).reshape(S5_NB, 1, LANES), (1, S5_TC, 1)).reshape(1, -1)
            for direction, reverse in enumerate((False, True)):
                msc, mic, moc, a = _s5_weights(s5_lambda_re[j, direction], s5_lambda_im[j, direction],
                                               s5_log_step[j, direction], s5_b_re[j, direction],
                                               s5_b_im[j, direction], s5_c_re[j, direction],
                                               s5_c_im[j, direction], reverse)
                y = _s5_dir(u3, msc, mic, moc, etj, eap, a, y, reverse)
            y = y.reshape(lt * nb, S5_WIDTH)
            final = layer == DEPTH - 1
            res = _out_odd(y, g, xc, ml, glu_w[j].astype(BF16), glu_b[j].reshape(1, -1),
                           od_w_out[j].astype(BF16), fw, final)
            if final:
                out = res
            else:
                xc = res
    return out
```

```python
import functools
import math

import jax
import jax.numpy as jnp
from jax import lax
from jax.experimental import pallas as pl
from jax.experimental.pallas import tpu as pltpu

F32 = jnp.float32
BF16 = jnp.bfloat16

D_MODEL = 1024
DEPTH = 4
CTX_LEN = 256
GRID_W = 64
EPS = 1e-6

ATT_HEADS = 8
ATT_KV_HEADS = 2
HEAD_DIM = 64
ATT_GROUP = ATT_HEADS // ATT_KV_HEADS
ATT_WIDTH = ATT_HEADS * HEAD_DIM
KV_WIDTH = ATT_KV_HEADS * HEAD_DIM
ROPE_THETA = 10000.0

RNN_WIDTH = D_MODEL // 2
RNN_BLOCKS = 8
RNN_BLOCK_DIM = RNN_WIDTH // RNN_BLOCKS
LRU_C = 8.0

S5_WIDTH = D_MODEL
S5_GROUP = 16
S5_GROUPS = S5_WIDTH // S5_GROUP
S5_STATE = 64

SUBLANES = 8
LANES = 128
ROW_TT = 128
ROWS = ROW_TT * SUBLANES
ATT_TQ = 256
ATT_TK = 2048
ATT_VROWS = HEAD_DIM + 16
RNN_TT = 128
S5_TT = 256
S5_TC = 8
S5_GB = LANES // S5_GROUP
S5_NB = S5_GROUPS // S5_GB
S5_SW = S5_GB * S5_STATE
VMEM_LIMIT = 56 * 1024 * 1024


def _cparams(sem):
    return pltpu.CompilerParams(dimension_semantics=sem, vmem_limit_bytes=VMEM_LIMIT)


def _sigmoid(x):
    return 0.5 * jnp.tanh(0.5 * x) + 0.5


def _silu(x):
    return x * _sigmoid(x)


def _batch_rows(b):
    return pl.ds(b, ROW_TT, stride=SUBLANES)


def _mod_kernel(c_ref, w_ref, b_ref, o_ref):
    c = c_ref[...]
    sc = _silu(c).astype(BF16)
    o_ref[0] = jnp.dot(sc, w_ref[0].astype(BF16), preferred_element_type=F32) + b_ref[0]


def _modulation(cond, ada_w, ada_b):
    depth, d, d3 = ada_w.shape
    nblk = d3 // d
    return pl.pallas_call(
        _mod_kernel,
        grid=(depth, nblk),
        in_specs=[
            pl.BlockSpec((16, d), lambda l, n: (0, 0)),
            pl.BlockSpec((1, d, d), lambda l, n: (l, 0, n)),
            pl.BlockSpec((1, 1, d), lambda l, n: (l, 0, n)),
        ],
        out_specs=pl.BlockSpec((1, 16, d), lambda l, n: (l, 0, n)),
        out_shape=jax.ShapeDtypeStruct((depth, 16, d3), F32),
        compiler_params=_cparams(("parallel", "parallel")),
        name="ada_mod",
    )(cond, ada_w, ada_b.reshape(depth, 1, d3))


def _to_rows_kernel(c_ref, x_ref, o_ref, scr, *, n_ctx_tiles):
    i = pl.program_id(0)
    nblk = D_MODEL // LANES

    def move(src_ref):
        for b in range(SUBLANES):
            for j in range(nblk):
                scr[j, _batch_rows(b), :] = src_ref[b, :, LANES * j:LANES * (j + 1)]
        for j in range(nblk):
            o_ref[:, LANES * j:LANES * (j + 1)] = scr[j]

    @pl.when(i < n_ctx_tiles)
    def _():
        move(c_ref)

    @pl.when(i >= n_ctx_tiles)
    def _():
        move(x_ref)


def _per_batch(x, vec):
    r, d = x.shape
    return (x.reshape(r // SUBLANES, SUBLANES, d) * vec[None]).reshape(r, d)


def _norm_mod(x, mod_ref):
    ms = jnp.mean(x * x, axis=-1, keepdims=True)
    xn = x * lax.rsqrt(ms + EPS)
    r, d = x.shape
    x3 = xn.reshape(r // SUBLANES, SUBLANES, d)
    return (x3 * (1.0 + mod_ref[0, 1])[None] + mod_ref[0, 0][None]).reshape(r, d)


def _mod_spec(n_ctx_tiles, offset=0):
    return pl.BlockSpec((1, 3, SUBLANES, D_MODEL),
                        lambda i: (jnp.where(i + offset < n_ctx_tiles, 0, 1), 0, 0, 0))


def _group_mean_sq(y, ones_ref):
    sq = y * y
    hi = sq.astype(BF16)
    lo = (sq - hi.astype(F32)).astype(BF16)
    ones = ones_ref[...]
    s = jnp.dot(hi, ones, preferred_element_type=F32) + jnp.dot(lo, ones, preferred_element_type=F32)
    return s * (1.0 / HEAD_DIM)


def _in_even_kernel(x_ref, mod_ref, w_ref, cos_ref, sin_ref, gq_ref, gk_ref, oq_ref, ok_ref,
                    q_ref, k_ref, v_ref, ga_ref, u_ref, gr_ref, q_scr, kv_scr):
    h = _norm_mod(x_ref[...], mod_ref).astype(BF16)
    y = jnp.dot(h, w_ref[...], preferred_element_type=F32)
    aw, kw = ATT_WIDTH, KV_WIDTH
    cos = cos_ref[...]
    sin = sin_ref[...]
    lane = lax.broadcasted_iota(jnp.int32, (1, LANES), 1)
    lower = (lane % HEAD_DIM) < HEAD_DIM // 2

    def swap(t):
        return jnp.where(lower, pltpu.roll(t, LANES - HEAD_DIM // 2, axis=1), pltpu.roll(t, HEAD_DIM // 2, axis=1))

    def rotation(gain_ref):
        ca = gain_ref[0:1, :] * cos
        sa = gain_ref[1:2, :] * sin
        return lambda t: t * ca + swap(t) * sa

    rotate_q = rotation(gq_ref)
    rotate_k = rotation(gk_ref)

    qp = y[:, 0:aw]
    rq = lax.rsqrt(_group_mean_sq(qp, oq_ref) + EPS)
    for j in range(aw // LANES):
        sl = slice(LANES * j, LANES * (j + 1))
        q_scr[j] = rq[:, sl] * qp[:, sl]
    o = aw
    kp = y[:, o:o + kw]
    kv_scr[0] = lax.rsqrt(_group_mean_sq(kp, ok_ref) + EPS) * kp
    o += kw
    kv_scr[1] = y[:, o:o + kw]
    o += kw
    for b in range(SUBLANES):
        for j in range(aw // LANES):
            q_ref[b, :, LANES * j:LANES * (j + 1)] = rotate_q(q_scr[j, _batch_rows(b), :]).astype(BF16)
        k_ref[b] = rotate_k(kv_scr[0, _batch_rows(b), :]).astype(BF16)
        v_ref[b] = kv_scr[1, _batch_rows(b), :].astype(BF16)
    ga_ref[...] = y[:, o:o + aw].astype(BF16)
    o += aw
    u_ref[...] = y[:, o:o + RNN_WIDTH]
    o += RNN_WIDTH
    gr_ref[...] = y[:, o:o + RNN_WIDTH].astype(BF16)


def _in_even_first_kernel(c_ref, xb_ref, *refs, n_ctx_tiles):
    *main, xrow_ref, q_scr, kv_scr, x_scr = refs
    _to_rows_kernel(c_ref, xb_ref, xrow_ref, x_scr, n_ctx_tiles=n_ctx_tiles)
    _in_even_kernel(xrow_ref, *main, q_scr, kv_scr)


def _in_even(src, mods, w, cos, sin, gq, gk, ones_q, ones_k):
    first = isinstance(src, tuple)
    d = D_MODEL
    if first:
        ctx, x = src
        nb, seq, _ = x.shape
        nct = ctx.shape[1] // ROW_TT
        nrows = (ctx.shape[1] + seq) * nb
    else:
        nrows = src.shape[0]
    nt = nrows // ROWS
    lt = nrows // SUBLANES
    aw, kw, rw = ATT_WIDTH, KV_WIDTH, RNN_WIDTH
    row = lambda i: (i, 0)
    const = lambda i: (0, 0)
    per_batch = lambda i: (0, i, 0)
    if first:
        kern = functools.partial(_in_even_first_kernel, n_ctx_tiles=nct)
        src_specs = [pl.BlockSpec((nb, ROW_TT, d), lambda i: (0, jnp.minimum(i, nct - 1), 0)),
                     pl.BlockSpec((nb, ROW_TT, d), lambda i: (0, jnp.maximum(i - nct, 0), 0))]
        extra_out_specs = [pl.BlockSpec((ROWS, d), row)]
        extra_out_shape = [jax.ShapeDtypeStruct((nrows, d), F32)]
        extra_scratch = [pltpu.VMEM((d // LANES, ROWS, LANES), F32)]
        args = (ctx, x)
    else:
        kern = _in_even_kernel
        src_specs = [pl.BlockSpec((ROWS, d), row)]
        extra_out_specs, extra_out_shape, extra_scratch = [], [], []
        args = (src,)
    return pl.pallas_call(
        kern,
        grid=(nt,),
        in_specs=src_specs + [
            _mod_spec(CTX_LEN // ROW_TT),
            pl.BlockSpec(w.shape, const),
            pl.BlockSpec((ROW_TT, LANES), row),
            pl.BlockSpec((ROW_TT, LANES), row),
            pl.BlockSpec(gq.shape, const),
            pl.BlockSpec(gk.shape, const),
            pl.BlockSpec(ones_q.shape, const),
            pl.BlockSpec(ones_k.shape, const),
        ],
        out_specs=[
            pl.BlockSpec((SUBLANES, ROW_TT, aw), per_batch),
            pl.BlockSpec((SUBLANES, ROW_TT, kw), per_batch),
            pl.BlockSpec((SUBLANES, ROW_TT, kw), per_batch),
            pl.BlockSpec((ROWS, aw), row),
            pl.BlockSpec((ROWS, rw), row),
            pl.BlockSpec((ROWS, rw), row),
        ] + extra_out_specs,
        out_shape=[
            jax.ShapeDtypeStruct((SUBLANES, lt, aw), BF16),
            jax.ShapeDtypeStruct((SUBLANES, lt, kw), BF16),
            jax.ShapeDtypeStruct((SUBLANES, lt, kw), BF16),
            jax.ShapeDtypeStruct((nrows, aw), BF16),
            jax.ShapeDtypeStruct((nrows, rw), F32),
            jax.ShapeDtypeStruct((nrows, rw), BF16),
        ] + extra_out_shape,
        scratch_shapes=[pltpu.VMEM((aw // LANES, ROWS, LANES), F32),
                        pltpu.VMEM((2, ROWS, LANES), F32)] + extra_scratch,
        compiler_params=_cparams(("parallel",)),
        name="even_in_proj_first" if first else "even_in_proj",
    )(*args, mods, w, cos, sin, gq, gk, ones_q, ones_k)


def _attn_kernel(q_ref, k_ref, v_ref, o_ref, vt_scr, s_scr, acc_scr, *, n_ctx_tiles, n_x_chunks):
    qt = pl.program_id(1)
    lane = lax.broadcasted_iota(jnp.int32, (1, LANES), 1)
    tq = ATT_TQ
    cols = ATT_GROUP * tq
    lt = v_ref.shape[1]
    n = n_x_chunks
    kt = 4 * LANES

    @pl.when(qt == 0)
    def _():
        def fill(i, carry):
            rs = pl.ds(pl.multiple_of(i * CTX_LEN, CTX_LEN), CTX_LEN)
            vt = v_ref[0, rs, :].astype(F32).T
            for h in range(ATT_KV_HEADS):
                vt_scr[h, 0:HEAD_DIM, rs] = vt[HEAD_DIM * h:HEAD_DIM * (h + 1)].astype(BF16)
                vt_scr[h, HEAD_DIM:ATT_VROWS, rs] = jnp.ones((ATT_VROWS - HEAD_DIM, CTX_LEN), BF16)
            return carry
        lax.fori_loop(0, lt // CTX_LEN, fill, 0)

    def key_range(c):
        return (CTX_LEN + c * ATT_TK, ATT_TK) if c < n else (0, CTX_LEN)

    for h in range(ATT_KV_HEADS):
        in_head = (lane // HEAD_DIM) == h
        qs = jnp.concatenate(
            [jnp.where(in_head, q_ref[0, :, LANES * g:LANES * (g + 1)], jnp.zeros((), BF16))
             for g in range(ATT_GROUP)], axis=0)
        qst = qs.astype(F32).T.astype(BF16)

        def scores(rng, slot, qst=qst):
            start, width = rng
            s_scr[slot, 0:width, :] = jnp.dot(k_ref[0, start:start + width, :], qst,
                                              preferred_element_type=F32)
            return jnp.max(s_scr[slot, 0:width, :], axis=0, keepdims=True)

        def consume(rng, slot, m, mloc, acc, h=h):
            start, width = rng
            m_new = jnp.maximum(m, mloc)
            acc = acc * jnp.exp2(m - m_new)
            kw = min(kt, width)
            for t in range(width // kw):
                p = jnp.exp2(s_scr[slot, kw * t:kw * (t + 1), :] - m_new).astype(BF16)
                acc = acc + jnp.dot(vt_scr[h, :, start + kw * t:start + kw * (t + 1)], p,
                                    preferred_element_type=F32)
            return m_new, acc

        zero = jnp.zeros((ATT_VROWS, cols), F32)

        @pl.when(qt < n_ctx_tiles)
        def _(h=h, scores=scores, consume=consume):
            mloc = scores((0, CTX_LEN), 0)
            acc_scr[h] = consume((0, CTX_LEN), 0, mloc, mloc, zero)[1]

        @pl.when(qt >= n_ctx_tiles)
        def _(h=h, scores=scores, consume=consume):
            mloc = scores(key_range(0), 0)
            m, acc = mloc, zero
            for i in range(n + 1):
                mloc_next = scores(key_range(i + 1), (i + 1) % 2) if i + 1 <= n else None
                m, acc = consume(key_range(i), i % 2, m, mloc, acc)
                mloc = mloc_next
            acc_scr[h] = acc

    first = lane < HEAD_DIM
    pad = jnp.zeros((LANES - ATT_VROWS, tq), F32)
    for g in range(ATT_GROUP):
        cs = slice(tq * g, tq * (g + 1))
        outs = []
        for h in range(ATT_KV_HEADS):
            acc = jnp.concatenate([acc_scr[h, :, cs], pad], axis=0).T
            outs.append(acc / acc[:, HEAD_DIM:HEAD_DIM + 1])
        o_ref[0, :, LANES * g:LANES * (g + 1)] = jnp.where(
            first, outs[0], pltpu.roll(outs[1], HEAD_DIM, axis=1))


def _attention(q, k, v):
    nb, lt, _ = q.shape
    nq = lt // ATT_TQ
    cols = ATT_GROUP * ATT_TQ
    kern = functools.partial(_attn_kernel, n_ctx_tiles=CTX_LEN // ATT_TQ,
                             n_x_chunks=(lt - CTX_LEN) // ATT_TK)
    return pl.pallas_call(
        kern,
        grid=(nb, nq),
        in_specs=[
            pl.BlockSpec((1, ATT_TQ, ATT_WIDTH), lambda b, i: (b, i, 0)),
            pl.BlockSpec((1, lt, KV_WIDTH), lambda b, i: (b, 0, 0)),
            pl.BlockSpec((1, lt, KV_WIDTH), lambda b, i: (b, 0, 0)),
        ],
        out_specs=pl.BlockSpec((1, ATT_TQ, ATT_WIDTH), lambda b, i: (b, i, 0)),
        out_shape=jax.ShapeDtypeStruct((nb, lt, ATT_WIDTH), F32),
        scratch_shapes=[
            pltpu.VMEM((ATT_KV_HEADS, ATT_VROWS, lt), BF16),
            pltpu.VMEM((2, ATT_TK, cols), F32),
            pltpu.VMEM((ATT_KV_HEADS, ATT_VROWS, cols), F32),
        ],
        compiler_params=_cparams(("arbitrary", "arbitrary")),
        name="attention",
    )(q, k, v)


def _scan_tile(j, reverse, n_ctx_tiles, n_tiles):
    if not reverse:
        return j
    return jnp.where(j < n_ctx_tiles, n_ctx_tiles - 1 - j, n_tiles + n_ctx_tiles - 1 - j)


def _rnn_kernel(*refs, reverse, n_ctx_tiles, n_tiles, lt):
    if reverse:
        (u_ref, prev_ref, next_ref, cw_ref, cb_ref, wg_ref, bg_ref, clam_ref, yin_ref,
         y_ref, a_scr, b_scr, h_scr) = refs
    else:
        (u_ref, prev_ref, next_ref, cw_ref, cb_ref, wg_ref, bg_ref, clam_ref,
         y_ref, a_scr, b_scr, h_scr) = refs
        yin_ref = None
    j = pl.program_id(0)
    tile = _scan_tile(j, reverse, n_ctx_tiles, n_tiles)
    rows = RNN_TT * SUBLANES
    rw = RNN_WIDTH

    @pl.when(j == 0)
    def _():
        h_scr[...] = jnp.zeros_like(h_scr)

    t0 = tile * RNN_TT
    t1 = t0 + RNN_TT
    has_prev = jnp.logical_and(t0 != 0, t0 != CTX_LEN)
    has_next = jnp.logical_and(t1 != CTX_LEN, t1 != lt)
    u = u_ref[...]
    prev = jnp.where(has_prev, prev_ref[...], 0.0)
    nxt = jnp.where(has_next, next_ref[0:SUBLANES, :], 0.0)
    ext = jnp.concatenate([prev, u, nxt], axis=0)
    cw = cw_ref[...]
    s = SUBLANES
    cv = (cw[0:1] * ext[0:rows] + cw[1:2] * ext[s:rows + s] + cw[2:3] * ext[2 * s:rows + 2 * s]
          + cw[3:4] * ext[3 * s:rows + 3 * s] + cb_ref[...])
    g = jnp.dot(cv.astype(BF16), wg_ref[...], preferred_element_type=F32) + bg_ref[...]
    r = _sigmoid(g[:, 0:rw])
    i = _sigmoid(g[:, rw:2 * rw])
    log_a = clam_ref[...] * r
    a_scr[...] = jnp.exp(log_a)
    th = jnp.tanh(log_a)
    b_scr[...] = jnp.sqrt(-2.0 * th / (1.0 - th)) * (i * cv)

    def step(k, h):
        t = (RNN_TT - 1 - k) if reverse else k
        rs = pl.ds(pl.multiple_of(t * SUBLANES, SUBLANES), SUBLANES)
        h = a_scr[rs, :] * h + b_scr[rs, :]
        if reverse:
            y_ref[rs, :] = h + yin_ref[rs, :]
        else:
            y_ref[rs, :] = h
        return h

    h_scr[...] = lax.fori_loop(0, RNN_TT, step, h_scr[...], unroll=8)


def _rnn_dir(u, conv_w, conv_b, wg, bg, clam, yin, reverse):
    nrows = u.shape[0]
    rows = RNN_TT * SUBLANES
    n_tiles = nrows // rows
    n_ctx = CTX_LEN // RNN_TT
    lt = nrows // SUBLANES
    halo = 2 * SUBLANES
    per = rows // halo
    nhalo = nrows // halo
    tile_of = lambda j: _scan_tile(j, reverse, n_ctx, n_tiles)
    rw = RNN_WIDTH
    const = lambda j: (0, 0)
    in_specs = [
        pl.BlockSpec((rows, rw), lambda j: (tile_of(j), 0)),
        pl.BlockSpec((halo, rw), lambda j: (jnp.maximum(tile_of(j) * per - 1, 0), 0)),
        pl.BlockSpec((halo, rw), lambda j: (jnp.minimum((tile_of(j) + 1) * per, nhalo - 1), 0)),
        pl.BlockSpec((4, rw), const),
        pl.BlockSpec((1, rw), const),
        pl.BlockSpec((rw, 2 * rw), const),
        pl.BlockSpec((1, 2 * rw), const),
        pl.BlockSpec((1, rw), const),
    ]
    args = [u, u, u, conv_w, conv_b, wg, bg, clam]
    if reverse:
        in_specs.append(pl.BlockSpec((rows, rw), lambda j: (tile_of(j), 0)))
        args.append(yin)
    kern = functools.partial(_rnn_kernel, reverse=reverse, n_ctx_tiles=n_ctx, n_tiles=n_tiles, lt=lt)
    return pl.pallas_call(
        kern,
        grid=(n_tiles,),
        in_specs=in_specs,
        out_specs=pl.BlockSpec((rows, rw), lambda j: (tile_of(j), 0)),
        out_shape=jax.ShapeDtypeStruct((nrows, rw), F32),
        scratch_shapes=[pltpu.VMEM((rows, rw), F32), pltpu.VMEM((rows, rw), F32),
                        pltpu.VMEM((SUBLANES, rw), F32)],
        compiler_params=_cparams(("arbitrary",)),
        name="rglru_bwd" if reverse else "rglru_fwd",
    )(*args)


def _out_even_in_odd_kernel(att_ref, ga_ref, y_ref, gr_ref, x_ref, mod_ref, wa_ref, wr_ref,
                            nmod_ref, wn_ref, o_ref, u_ref, g_ref, a_scr):
    nblk = ATT_WIDTH // LANES
    for b in range(SUBLANES):
        for j in range(nblk):
            a_scr[j, _batch_rows(b), :] = att_ref[b, :, LANES * j:LANES * (j + 1)]
    att = jnp.concatenate([a_scr[j] for j in range(nblk)], axis=1)
    m1 = (att * _silu(ga_ref[...].astype(F32))).astype(BF16)
    m2 = (y_ref[...] * _silu(gr_ref[...].astype(F32))).astype(BF16)
    o = (jnp.dot(m1, wa_ref[...], preferred_element_type=F32)
         + jnp.dot(m2, wr_ref[...], preferred_element_type=F32))
    xn = x_ref[...] + _per_batch(o, mod_ref[0, 2])
    o_ref[...] = xn
    h = _norm_mod(xn, nmod_ref).astype(BF16)
    z = jnp.dot(h, wn_ref[...], preferred_element_type=F32)
    u_ref[...] = z[:, 0:S5_WIDTH]
    g_ref[...] = z[:, S5_WIDTH:2 * S5_WIDTH].astype(BF16)


def _out_even_in_odd(att, ga, y, gr, xc, mods, wa, wr, next_mods, wn):
    nrows = xc.shape[0]
    nt = nrows // ROWS
    d = D_MODEL
    row = lambda i: (i, 0)
    const = lambda i: (0, 0)
    return pl.pallas_call(
        _out_even_in_odd_kernel,
        grid=(nt,),
        in_specs=[
            pl.BlockSpec((SUBLANES, ROW_TT, ATT_WIDTH), lambda i: (0, i, 0)),
            pl.BlockSpec((ROWS, ATT_WIDTH), row),
            pl.BlockSpec((ROWS, RNN_WIDTH), row),
            pl.BlockSpec((ROWS, RNN_WIDTH), row),
            pl.BlockSpec((ROWS, d), row),
            _mod_spec(CTX_LEN // ROW_TT),
            pl.BlockSpec(wa.shape, const),
            pl.BlockSpec(wr.shape, const),
            _mod_spec(CTX_LEN // ROW_TT),
            pl.BlockSpec(wn.shape, const),
        ],
        out_specs=[pl.BlockSpec((ROWS, d), row), pl.BlockSpec((ROWS, S5_WIDTH), row),
                   pl.BlockSpec((ROWS, S5_WIDTH), row)],
        out_shape=[jax.ShapeDtypeStruct(xc.shape, F32),
                   jax.ShapeDtypeStruct((nrows, S5_WIDTH), F32),
                   jax.ShapeDtypeStruct((nrows, S5_WIDTH), BF16)],
        scratch_shapes=[pltpu.VMEM((ATT_WIDTH // LANES, ROWS, LANES), F32)],
        compiler_params=_cparams(("parallel",)),
        name="even_out_odd_in_proj",
    )(att, ga, y, gr, xc, mods, wa, wr, next_mods, wn)


def _s5_kernel(u_ref, msc_ref, mic_ref, moc_ref, etj_ref, eap_ref, a_ref, extra_ref, y_ref,
               ms_scr, mio_scr, hp_scr, h_scr, *, reverse):
    j = pl.program_id(1)
    nc = S5_TT // S5_TC
    rows = nc * SUBLANES
    sw = S5_SW

    @pl.when(j == 0)
    def _():
        h_scr[...] = jnp.zeros_like(h_scr)
        cw = S5_TC * LANES
        r = lax.broadcasted_iota(jnp.int32, (cw, 1), 0)
        c = lax.broadcasted_iota(jnp.int32, (1, cw), 1)
        row_group_ch = (r // S5_GROUP) % S5_GB
        row_group_st = (r // S5_STATE) % S5_GB
        col_group_ch = (c // S5_GROUP) % S5_GB
        col_group_st = (c // S5_STATE) % S5_GB

        def expand(compact_ref, e_ref, keep):
            full = jnp.dot(compact_ref[0], e_ref[...], preferred_element_type=F32)
            return jnp.where(keep, full, 0.0).astype(BF16)

        ms_scr[...] = expand(msc_ref, eap_ref, row_group_ch == col_group_st)
        mio_scr[0:cw, :] = expand(mic_ref, etj_ref, row_group_ch == col_group_ch)
        mio_scr[cw:2 * cw, :] = expand(moc_ref, etj_ref, row_group_st == col_group_ch)

    def chunk_major(ref):
        return jnp.concatenate(
            [ref[:, SUBLANES * t:SUBLANES * (t + 1), :].reshape(rows, LANES) for t in range(S5_TC)], axis=1)

    lhs32 = chunk_major(u_ref)
    lhs = lhs32.astype(BF16)
    local = jnp.dot(lhs, ms_scr[...], preferred_element_type=F32)
    ar = a_ref[0, :, 0:sw]
    ai = a_ref[0, :, sw:2 * sw]
    hr = h_scr[:, 0:sw]
    hi = h_scr[:, sw:2 * sw]
    for c in (range(nc - 1, -1, -1) if reverse else range(nc)):
        rs = slice(SUBLANES * c, SUBLANES * (c + 1))
        hp_scr[rs, 0:sw] = hr
        hp_scr[rs, sw:2 * sw] = hi
        hr, hi = (ar * hr - ai * hi + local[rs, 0:sw], ar * hi + ai * hr + local[rs, sw:2 * sw])
    h_scr[:, 0:sw] = hr
    h_scr[:, sw:2 * sw] = hi
    cw = S5_TC * LANES
    tile = 2 * LANES
    hp = hp_scr[...].astype(BF16)
    parts = []
    for b in range(cw // tile):
        cs = slice(tile * b, tile * (b + 1))
        ks = slice(tile * b, cw) if reverse else slice(0, tile * (b + 1))
        parts.append(jnp.dot(lhs[:, ks], mio_scr[ks, cs], preferred_element_type=F32)
                     + jnp.dot(hp, mio_scr[cw:2 * cw, cs], preferred_element_type=F32))
    y = jnp.concatenate(parts, axis=1)
    if reverse:
        y = y + chunk_major(extra_ref)
    else:
        y = y + extra_ref[...] * lhs32
    for t in range(S5_TC):
        y_ref[:, SUBLANES * t:SUBLANES * (t + 1), :] = (
            y[:, LANES * t:LANES * (t + 1)].reshape(nc, SUBLANES, LANES))


def _s5_dir(u3, msc, mic, moc, etj, eap, a, extra, reverse):
    nchunks, crow, width = u3.shape
    nc = S5_TT // S5_TC
    n_tiles = nchunks // nc
    n_ctx = CTX_LEN // S5_TT
    tile_of = lambda j: _scan_tile(j, reverse, n_ctx, n_tiles)
    blk = lambda k, j: (tile_of(j), 0, k)
    wblk = lambda k, j: (k, 0, 0)
    cw = S5_TC * LANES
    if reverse:
        extra_spec = pl.BlockSpec((nc, crow, LANES), blk)
    else:
        extra_spec = pl.BlockSpec((1, cw), lambda k, j: (0, k))
    return pl.pallas_call(
        functools.partial(_s5_kernel, reverse=reverse),
        grid=(S5_NB, n_tiles),
        in_specs=[
            pl.BlockSpec((nc, crow, LANES), blk),
            pl.BlockSpec((1, cw, LANES), wblk),
            pl.BlockSpec((1, cw, LANES), wblk),
            pl.BlockSpec((1, cw, LANES), wblk),
            pl.BlockSpec(etj.shape, lambda k, j: (0, 0)),
            pl.BlockSpec(eap.shape, lambda k, j: (0, 0)),
            pl.BlockSpec((1, SUBLANES, 2 * S5_SW), wblk),
            extra_spec,
        ],
        out_specs=pl.BlockSpec((nc, crow, LANES), blk),
        out_shape=jax.ShapeDtypeStruct(u3.shape, F32),
        scratch_shapes=[pltpu.VMEM((cw, 2 * S5_SW), BF16), pltpu.VMEM((cw + 2 * S5_SW, cw), BF16),
                        pltpu.VMEM((nc * SUBLANES, 2 * S5_SW), F32), pltpu.VMEM((SUBLANES, 2 * S5_SW), F32)],
        compiler_params=_cparams(("arbitrary", "arbitrary")),
        name="s5_bwd" if reverse else "s5_fwd",
    )(u3, msc, mic, moc, etj, eap, a, extra)


def _out_odd_kernel(y_ref, g_ref, x_ref, mod_ref, gw_ref, gb_ref, wo_ref, fw_ref, o_ref, *scr, final):
    yy = jax.nn.gelu(y_ref[...]).astype(BF16)
    z = jnp.dot(yy, gw_ref[...], preferred_element_type=F32) + gb_ref[...]
    g = g_ref[...].astype(F32)
    m = z[:, 0:S5_WIDTH] * _sigmoid(z[:, S5_WIDTH:2 * S5_WIDTH]) * _silu(g)
    o = jnp.dot(m.astype(BF16), wo_ref[...], preferred_element_type=F32)
    xn = x_ref[...] + _per_batch(o, mod_ref[0, 2])
    if final:
        (o_scr,) = scr
        ms = jnp.mean(xn * xn, axis=-1, keepdims=True)
        xo = xn * lax.rsqrt(ms + EPS) * fw_ref[...]
        nblk = D_MODEL // LANES
        for j in range(nblk):
            o_scr[j] = xo[:, LANES * j:LANES * (j + 1)]
        for b in range(SUBLANES):
            for j in range(nblk):
                o_ref[b, :, LANES * j:LANES * (j + 1)] = o_scr[j, _batch_rows(b), :]
    else:
        o_ref[...] = xn


def _out_odd(y, g, xc, mods, gw, gb, wo, fw, final):
    nrows = xc.shape[0]
    d = D_MODEL
    nct = CTX_LEN // ROW_TT
    lt = nrows // SUBLANES
    if final:
        nt = (lt - CTX_LEN) // ROW_TT
        row = lambda i: (i + nct, 0)
        mod_spec = _mod_spec(nct, offset=nct)
        out_spec = pl.BlockSpec((SUBLANES, ROW_TT, d), lambda i: (0, i, 0))
        out_shape = jax.ShapeDtypeStruct((SUBLANES, lt - CTX_LEN, d), F32)
        scratch = [pltpu.VMEM((d // LANES, ROWS, LANES), F32)]
    else:
        nt = lt // ROW_TT
        row = lambda i: (i, 0)
        mod_spec = _mod_spec(nct)
        out_spec = pl.BlockSpec((ROWS, d), row)
        out_shape = jax.ShapeDtypeStruct(xc.shape, F32)
        scratch = []
    const = lambda i: (0, 0)
    return pl.pallas_call(
        functools.partial(_out_odd_kernel, final=final),
        grid=(nt,),
        in_specs=[
            pl.BlockSpec((ROWS, S5_WIDTH), row),
            pl.BlockSpec((ROWS, S5_WIDTH), row),
            pl.BlockSpec((ROWS, d), row),
            mod_spec,
            pl.BlockSpec(gw.shape, const),
            pl.BlockSpec(gb.shape, const),
            pl.BlockSpec(wo.shape, const),
            pl.BlockSpec(fw.shape, const),
        ],
        out_specs=out_spec,
        out_shape=out_shape,
        scratch_shapes=scratch,
        compiler_params=_cparams(("parallel",)),
        name="odd_out_proj_final" if final else "odd_out_proj",
    )(y, g, xc, mods, gw, gb, wo, fw)


def _rope_tables(lt):
    n = lt - CTX_LEN
    rows = n // GRID_W
    row = jnp.repeat(jnp.arange(rows, dtype=F32), GRID_W)
    col = jnp.tile(jnp.arange(GRID_W, dtype=F32), rows)
    n_freq = HEAD_DIM // 4
    inv = ROPE_THETA ** (-jnp.arange(n_freq, dtype=F32) / n_freq)
    ang = jnp.concatenate([row[:, None] * inv, col[:, None] * inv], axis=-1)
    half = HEAD_DIM // 2
    cos = jnp.concatenate([jnp.ones((CTX_LEN, half), F32), jnp.cos(ang)], axis=0)
    sin = jnp.concatenate([jnp.zeros((CTX_LEN, half), F32), jnp.sin(ang)], axis=0)
    reps = LANES // HEAD_DIM
    ct = jnp.tile(jnp.concatenate([cos, cos], axis=-1), (1, reps))
    st = jnp.tile(jnp.concatenate([-sin, sin], axis=-1), (1, reps))
    return ct, st


def _rope_gains(gain, scale):
    g_eo = jnp.concatenate([gain[0::2], gain[1::2]])
    g_oe = jnp.concatenate([gain[1::2], gain[0::2]])
    reps = LANES // HEAD_DIM
    return jnp.stack([jnp.tile(g_eo, reps), jnp.tile(g_oe, reps)]).astype(F32) * scale


def _even_weights(w_in, w_out):
    aw, kw, rw = ATT_WIDTH, KV_WIDTH, RNN_WIDTH
    eo = jnp.concatenate([jnp.arange(0, HEAD_DIM, 2), jnp.arange(1, HEAD_DIM, 2)])
    slot_head = jnp.array([h * ATT_GROUP + g for g in range(ATT_GROUP) for h in range(ATT_KV_HEADS)])
    q_cols = (slot_head[:, None] * HEAD_DIM + eo[None, :]).reshape(-1)
    kv_heads = jnp.arange(ATT_KV_HEADS)
    k_cols = aw + (kv_heads[:, None] * HEAD_DIM + eo[None, :]).reshape(-1)
    v_cols = aw + kw + jnp.arange(kw)
    slot_cols = (slot_head[:, None] * HEAD_DIM + jnp.arange(HEAD_DIM)[None, :]).reshape(-1)
    ga_cols = aw + 2 * kw + slot_cols
    u_cols = 2 * aw + 2 * kw + jnp.arange(rw)
    gr_cols = 2 * aw + 2 * kw + rw + jnp.arange(rw)
    cols = jnp.concatenate([q_cols, k_cols, v_cols, ga_cols, u_cols, gr_cols])
    w = w_in[:, cols].astype(BF16)
    wa = w_out[:aw][slot_cols].astype(BF16)
    wr = w_out[aw:].astype(BF16)
    return w, wa, wr


def _block_ones(width):
    idx = jnp.arange(width) // HEAD_DIM
    return (idx[:, None] == idx[None, :]).astype(BF16)


def _rnn_gate_weights(wa, ba, wx, bx):
    eye = jnp.eye(RNN_BLOCKS, dtype=F32)

    def dense(w):
        return jnp.einsum('hij,hk->hikj', w, eye).reshape(RNN_WIDTH, RNN_WIDTH)

    wg = jnp.concatenate([dense(wa), dense(wx)], axis=1).astype(BF16)
    bg = jnp.concatenate([ba.reshape(-1), bx.reshape(-1)])[None, :]
    return wg, bg


def _s5_discretize(lam_re, lam_im, log_step, b_re, b_im):
    dt = jnp.exp(log_step)[:, None]
    mag = jnp.exp(lam_re * dt)
    ab_re = mag * jnp.cos(lam_im * dt)
    ab_im = mag * jnp.sin(lam_im * dt)
    den = lam_re * lam_re + lam_im * lam_im
    nr, ni = ab_re - 1.0, ab_im
    f_re = (nr * lam_re + ni * lam_im) / den
    f_im = (ni * lam_re - nr * lam_im) / den
    bb_re = f_re[..., None] * b_re - f_im[..., None] * b_im
    bb_im = f_re[..., None] * b_im + f_im[..., None] * b_re
    return ab_re, ab_im, bb_re, bb_im


def _s5_weights(lam_re, lam_im, log_step, b_re, b_im, c_re, c_im, reverse):
    hp = lax.Precision.HIGHEST
    f = lambda t: t.astype(F32)
    lam_re, lam_im, log_step, b_re, b_im, c_re, c_im = map(f, (lam_re, lam_im, log_step, b_re, b_im, c_re, c_im))
    tc, nbk, gb, p, c = S5_TC, S5_NB, S5_GB, S5_STATE, S5_GROUP
    _, _, bb_re, bb_im = _s5_discretize(lam_re, lam_im, log_step, b_re, b_im)
    dt = jnp.exp(log_step)[:, None]
    e = jnp.arange(tc + 1, dtype=F32)[:, None, None]
    mag = jnp.exp(e * (lam_re * dt))
    pw_re = mag * jnp.cos(e * (lam_im * dt))
    pw_im = mag * jnp.sin(e * (lam_im * dt))
    c_re_t = c_re.transpose(0, 2, 1)
    c_im_t = c_im.transpose(0, 2, 1)
    ca_re = c_re_t[None] * pw_re[..., None] - c_im_t[None] * pw_im[..., None]
    ca_im = c_re_t[None] * pw_im[..., None] + c_im_t[None] * pw_re[..., None]
    kern = (jnp.einsum('tgpj,gpi->tgij', ca_re[:tc], bb_re, precision=hp)
            - jnp.einsum('tgpj,gpi->tgij', ca_im[:tc], bb_im, precision=hp))
    none = jnp.zeros_like(kern[0])

    def lagged(s, t):
        lag = (s - t) if reverse else (t - s)
        return kern[lag] if lag >= 0 else none

    mic = jnp.stack([jnp.concatenate([lagged(s, t) for t in range(tc)], axis=-1) for s in range(tc)], 0)
    bb_re_t = bb_re.transpose(0, 2, 1)
    bb_im_t = bb_im.transpose(0, 2, 1)

    def leaving(s):
        e_s = s if reverse else tc - 1 - s
        ar, ai = pw_re[e_s][:, None, :], pw_im[e_s][:, None, :]
        return jnp.concatenate([ar * bb_re_t - ai * bb_im_t, ar * bb_im_t + ai * bb_re_t], axis=-1)

    msc = jnp.stack([leaving(s) for s in range(tc)], 0)
    e_t = [(tc - t) if reverse else (t + 1) for t in range(tc)]
    moc = jnp.stack([jnp.concatenate([ca_re[e] for e in e_t], axis=-1),
                     jnp.concatenate([-ca_im[e] for e in e_t], axis=-1)], 0)

    def blocked(m):
        lead, _, x, w = m.shape
        return m.reshape(lead, nbk, gb, x, w).transpose(1, 0, 2, 3, 4).reshape(nbk, lead * gb * x, w)

    mic, msc, moc = blocked(mic), blocked(msc), blocked(moc)
    a = jnp.stack([pw_re[tc], pw_im[tc]], 0).reshape(2, nbk, gb, p).transpose(1, 0, 2, 3).reshape(nbk, 1, 2 * gb * p)
    a = jnp.broadcast_to(a, (nbk, SUBLANES, 2 * gb * p))
    return msc.astype(BF16), mic.astype(BF16), moc.astype(BF16), a


def _s5_expanders():
    cw = S5_TC * LANES
    col = jnp.arange(cw)
    src_tj = (col // LANES) * S5_GROUP + col % S5_GROUP
    src_ap = (col // S5_SW) * S5_STATE + col % S5_STATE
    etj = (jnp.arange(S5_TC * S5_GROUP)[:, None] == src_tj[None, :]).astype(BF16)
    eap = (jnp.arange(2 * S5_STATE)[:, None] == src_ap[None, :]).astype(BF16)
    return etj, eap


def kernel(x, c, ctx, c_ctx, ada_w, ada_b, ev_w_in, ev_w_out, q_norm_w, k_norm_w, rg_conv_w, rg_conv_b, rg_wa, rg_ba, rg_wx, rg_bx, rg_lambda, od_w_in, s5_lambda_re, s5_lambda_im, s5_log_step, s5_b_re, s5_b_im, s5_c_re, s5_c_im, s5_d, glu_w, glu_b, od_w_out, final_norm_w):
    nb, seq, d = x.shape
    assert nb == SUBLANES and d == D_MODEL and ctx.shape[1] == CTX_LEN
    assert seq % ATT_TK == 0 and DEPTH % 2 == 0
    lt = CTX_LEN + seq

    xc = None

    cond = jnp.zeros((16, d), F32).at[:nb].set(c).at[nb].set(c_ctx)
    mods = _modulation(cond, ada_w, ada_b).reshape(DEPTH, 16, 3, d)
    mods_x = mods[:, :nb].transpose(0, 2, 1, 3)
    mods_c = jnp.broadcast_to(mods[:, nb][:, :, None, :], mods_x.shape)
    mods = jnp.stack([mods_c, mods_x], axis=1)

    cos, sin = _rope_tables(lt)
    ones_q = _block_ones(ATT_WIDTH)
    ones_k = _block_ones(KV_WIDTH)
    etj, eap = _s5_expanders()
    fw = final_norm_w.reshape(1, d)
    out = None
    for layer in range(DEPTH):
        j = layer // 2
        ml = mods[layer]
        if layer % 2 == 0:
            w, wa, wr = _even_weights(ev_w_in[j], ev_w_out[j])
            gq = _rope_gains(q_norm_w[j], HEAD_DIM ** -0.5 * math.log2(math.e))
            gk = _rope_gains(k_norm_w[j], 1.0)
            if layer == 0:
                q, k, v, ga, u, gr, xc = _in_even((ctx, x), ml, w, cos, sin, gq, gk, ones_q, ones_k)
            else:
                q, k, v, ga, u, gr = _in_even(xc, ml, w, cos, sin, gq, gk, ones_q, ones_k)
            att = _attention(q, k, v)
            conv_b = rg_conv_b[j].reshape(1, RNN_WIDTH)
            y = None
            for direction, reverse in enumerate((False, True)):
                wg, bg = _rnn_gate_weights(rg_wa[j, direction], rg_ba[j, direction],
                                           rg_wx[j, direction], rg_bx[j, direction])
                clam = (-LRU_C * jax.nn.softplus(-rg_lambda[j, direction].astype(F32))).reshape(1, RNN_WIDTH)
                y = _rnn_dir(u, rg_conv_w[j], conv_b, wg, bg, clam, y, reverse)
            xc, u_odd, g_odd = _out_even_in_odd(att, ga, y, gr, xc, ml, wa, wr, mods[layer + 1],
                                                od_w_in[j].astype(BF16))
        else:
            u, g = u_odd, g_odd
            u3 = u.reshape(lt // S5_TC, S5_TC * nb, S5_WIDTH)
            y = jnp.tile(s5_d[j].astype(F32).reshape(S5_NB, 1, LANES), (1, S5_TC, 1)).reshape(1, -1)
            for direction, reverse in enumerate((False, True)):
                msc, mic, moc, a = _s5_weights(s5_lambda_re[j, direction], s5_lambda_im[j, direction],
                                               s5_log_step[j, direction], s5_b_re[j, direction],
                                               s5_b_im[j, direction], s5_c_re[j, direction],
                                               s5_c_im[j, direction], reverse)
                y = _s5_dir(u3, msc, mic, moc, etj, eap, a, y, reverse)
            y = y.reshape(lt * nb, S5_WIDTH)
            final = layer == DEPTH - 1
            res = _out_odd(y, g, xc, ml, glu_w[j].astype(BF16), glu_b[j].reshape(1, -1),
                           od_w_out[j].astype(BF16), fw, final)
            if final:
                out = res
            else:
                xc = res
    return out
```

```python
import functools
import math

import jax
import jax.numpy as jnp
from jax import lax
from jax.experimental import pallas as pl
from jax.experimental.pallas import tpu as pltpu

F32 = jnp.float32
BF16 = jnp.bfloat16

D_MODEL = 1024
DEPTH = 4
CTX_LEN = 256
GRID_W = 64
EPS = 1e-6

ATT_HEADS = 8
ATT_KV_HEADS = 2
HEAD_DIM = 64
ATT_GROUP = ATT_HEADS // ATT_KV_HEADS
ATT_WIDTH = ATT_HEADS * HEAD_DIM
KV_WIDTH = ATT_KV_HEADS * HEAD_DIM
ROPE_THETA = 10000.0

RNN_WIDTH = D_MODEL // 2
RNN_BLOCKS = 8
RNN_BLOCK_DIM = RNN_WIDTH // RNN_BLOCKS
LRU_C = 8.0

S5_WIDTH = D_MODEL
S5_GROUP = 16
S5_GROUPS = S5_WIDTH // S5_GROUP
S5_STATE = 64

SUBLANES = 8
LANES = 128
ROW_TT = 128
ROWS = ROW_TT * SUBLANES
ATT_TQ = 256
ATT_TK = 4096
ATT_VROWS = HEAD_DIM + 16
RNN_TT = 128
S5_TT = 256
S5_TC = 8
S5_GB = LANES // S5_GROUP
S5_NB = S5_GROUPS // S5_GB
S5_SW = S5_GB * S5_STATE
VMEM_LIMIT = 56 * 1024 * 1024


def _cparams(sem):
    return pltpu.CompilerParams(dimension_semantics=sem, vmem_limit_bytes=VMEM_LIMIT)


def _sigmoid(x):
    return 0.5 * jnp.tanh(0.5 * x) + 0.5


def _silu(x):
    return x * _sigmoid(x)


def _batch_rows(b):
    return pl.ds(b, ROW_TT, stride=SUBLANES)


def _mod_kernel(c_ref, w_ref, b_ref, o_ref):
    c = c_ref[...]
    sc = _silu(c).astype(BF16)
    o_ref[0] = jnp.dot(sc, w_ref[0].astype(BF16), preferred_element_type=F32) + b_ref[0]


def _modulation(cond, ada_w, ada_b):
    depth, d, d3 = ada_w.shape
    nblk = d3 // d
    return pl.pallas_call(
        _mod_kernel,
        grid=(depth, nblk),
        in_specs=[
            pl.BlockSpec((16, d), lambda l, n: (0, 0)),
            pl.BlockSpec((1, d, d), lambda l, n: (l, 0, n)),
            pl.BlockSpec((1, 1, d), lambda l, n: (l, 0, n)),
        ],
        out_specs=pl.BlockSpec((1, 16, d), lambda l, n: (l, 0, n)),
        out_shape=jax.ShapeDtypeStruct((depth, 16, d3), F32),
        compiler_params=_cparams(("parallel", "parallel")),
        name="ada_mod",
    )(cond, ada_w, ada_b.reshape(depth, 1, d3))


def _to_rows_kernel(c_ref, x_ref, o_ref, scr, *, n_ctx_tiles):
    i = pl.program_id(0)
    nblk = D_MODEL // LANES

    def move(src_ref):
        for b in range(SUBLANES):
            for j in range(nblk):
                scr[j, _batch_rows(b), :] = src_ref[b, :, LANES * j:LANES * (j + 1)]
        for j in range(nblk):
            o_ref[:, LANES * j:LANES * (j + 1)] = scr[j]

    @pl.when(i < n_ctx_tiles)
    def _():
        move(c_ref)

    @pl.when(i >= n_ctx_tiles)
    def _():
        move(x_ref)


def _per_batch(x, vec):
    r, d = x.shape
    return (x.reshape(r // SUBLANES, SUBLANES, d) * vec[None]).reshape(r, d)


def _norm_mod(x, mod_ref):
    ms = jnp.mean(x * x, axis=-1, keepdims=True)
    xn = x * lax.rsqrt(ms + EPS)
    r, d = x.shape
    x3 = xn.reshape(r // SUBLANES, SUBLANES, d)
    return (x3 * (1.0 + mod_ref[0, 1])[None] + mod_ref[0, 0][None]).reshape(r, d)


def _mod_spec(n_ctx_tiles, offset=0):
    return pl.BlockSpec((1, 3, SUBLANES, D_MODEL),
                        lambda i: (jnp.where(i + offset < n_ctx_tiles, 0, 1), 0, 0, 0))


def _group_mean_sq(y, ones_ref):
    sq = y * y
    hi = sq.astype(BF16)
    lo = (sq - hi.astype(F32)).astype(BF16)
    ones = ones_ref[...]
    s = jnp.dot(hi, ones, preferred_element_type=F32) + jnp.dot(lo, ones, preferred_element_type=F32)
    return s * (1.0 / HEAD_DIM)


def _in_even_kernel(x_ref, mod_ref, w_ref, cos_ref, sin_ref, gq_ref, gk_ref, oq_ref, ok_ref,
                    q_ref, k_ref, v_ref, ga_ref, u_ref, gr_ref, q_scr, kv_scr):
    h = _norm_mod(x_ref[...], mod_ref).astype(BF16)
    y = jnp.dot(h, w_ref[...], preferred_element_type=F32)
    aw, kw = ATT_WIDTH, KV_WIDTH
    cos = cos_ref[...]
    sin = sin_ref[...]
    lane = lax.broadcasted_iota(jnp.int32, (1, LANES), 1)
    lower = (lane % HEAD_DIM) < HEAD_DIM // 2

    def swap(t):
        return jnp.where(lower, pltpu.roll(t, LANES - HEAD_DIM // 2, axis=1), pltpu.roll(t, HEAD_DIM // 2, axis=1))

    def rotation(gain_ref):
        ca = gain_ref[0:1, :] * cos
        sa = gain_ref[1:2, :] * sin
        return lambda t: t * ca + swap(t) * sa

    rotate_q = rotation(gq_ref)
    rotate_k = rotation(gk_ref)

    qp = y[:, 0:aw]
    rq = lax.rsqrt(_group_mean_sq(qp, oq_ref) + EPS)
    for j in range(aw // LANES):
        sl = slice(LANES * j, LANES * (j + 1))
        q_scr[j] = rq[:, sl] * qp[:, sl]
    o = aw
    kp = y[:, o:o + kw]
    kv_scr[0] = lax.rsqrt(_group_mean_sq(kp, ok_ref) + EPS) * kp
    o += kw
    kv_scr[1] = y[:, o:o + kw]
    o += kw
    for b in range(SUBLANES):
        for j in range(aw // LANES):
            q_ref[b, :, LANES * j:LANES * (j + 1)] = rotate_q(q_scr[j, _batch_rows(b), :]).astype(BF16)
        k_ref[b] = rotate_k(kv_scr[0, _batch_rows(b), :]).astype(BF16)
        v_ref[b] = kv_scr[1, _batch_rows(b), :].astype(BF16)
    ga_ref[...] = y[:, o:o + aw].astype(BF16)
    o += aw
    u_ref[...] = y[:, o:o + RNN_WIDTH]
    o += RNN_WIDTH
    gr_ref[...] = y[:, o:o + RNN_WIDTH].astype(BF16)


def _in_even_first_kernel(c_ref, xb_ref, *refs, n_ctx_tiles):
    *main, xrow_ref, q_scr, kv_scr, x_scr = refs
    _to_rows_kernel(c_ref, xb_ref, xrow_ref, x_scr, n_ctx_tiles=n_ctx_tiles)
    _in_even_kernel(xrow_ref, *main, q_scr, kv_scr)


def _in_even(src, mods, w, cos, sin, gq, gk, ones_q, ones_k):
    first = isinstance(src, tuple)
    d = D_MODEL
    if first:
        ctx, x = src
        nb, seq, _ = x.shape
        nct = ctx.shape[1] // ROW_TT
        nrows = (ctx.shape[1] + seq) * nb
    else:
        nrows = src.shape[0]
    nt = nrows // ROWS
    lt = nrows // SUBLANES
    aw, kw, rw = ATT_WIDTH, KV_WIDTH, RNN_WIDTH
    row = lambda i: (i, 0)
    const = lambda i: (0, 0)
    per_batch = lambda i: (0, i, 0)
    if first:
        kern = functools.partial(_in_even_first_kernel, n_ctx_tiles=nct)
        src_specs = [pl.BlockSpec((nb, ROW_TT, d), lambda i: (0, jnp.minimum(i, nct - 1), 0)),
                     pl.BlockSpec((nb, ROW_TT, d), lambda i: (0, jnp.maximum(i - nct, 0), 0))]
        extra_out_specs = [pl.BlockSpec((ROWS, d), row)]
        extra_out_shape = [jax.ShapeDtypeStruct((nrows, d), F32)]
        extra_scratch = [pltpu.VMEM((d // LANES, ROWS, LANES), F32)]
        args = (ctx, x)
    else:
        kern = _in_even_kernel
        src_specs = [pl.BlockSpec((ROWS, d), row)]
        extra_out_specs, extra_out_shape, extra_scratch = [], [], []
        args = (src,)
    return pl.pallas_call(
        kern,
        grid=(nt,),
        in_specs=src_specs + [
            _mod_spec(CTX_LEN // ROW_TT),
            pl.BlockSpec(w.shape, const),
            pl.BlockSpec((ROW_TT, LANES), row),
            pl.BlockSpec((ROW_TT, LANES), row),
            pl.BlockSpec(gq.shape, const),
            pl.BlockSpec(gk.shape, const),
            pl.BlockSpec(ones_q.shape, const),
            pl.BlockSpec(ones_k.shape, const),
        ],
        out_specs=[
            pl.BlockSpec((SUBLANES, ROW_TT, aw), per_batch),
            pl.BlockSpec((SUBLANES, ROW_TT, kw), per_batch),
            pl.BlockSpec((SUBLANES, ROW_TT, kw), per_batch),
            pl.BlockSpec((ROWS, aw), row),
            pl.BlockSpec((ROWS, rw), row),
            pl.BlockSpec((ROWS, rw), row),
        ] + extra_out_specs,
        out_shape=[
            jax.ShapeDtypeStruct((SUBLANES, lt, aw), BF16),
            jax.ShapeDtypeStruct((SUBLANES, lt, kw), BF16),
            jax.ShapeDtypeStruct((SUBLANES, lt, kw), BF16),
            jax.ShapeDtypeStruct((nrows, aw), BF16),
            jax.ShapeDtypeStruct((nrows, rw), F32),
            jax.ShapeDtypeStruct((nrows, rw), BF16),
        ] + extra_out_shape,
        scratch_shapes=[pltpu.VMEM((aw // LANES, ROWS, LANES), F32),
                        pltpu.VMEM((2, ROWS, LANES), F32)] + extra_scratch,
        compiler_params=_cparams(("parallel",)),
        name="even_in_proj_first" if first else "even_in_proj",
    )(*args, mods, w, cos, sin, gq, gk, ones_q, ones_k)


def _attn_kernel(q_ref, k_ref, v_ref, o_ref, vt_scr, s_scr, acc_scr, *, n_ctx_tiles, n_x_chunks):
    qt = pl.program_id(1)
    lane = lax.broadcasted_iota(jnp.int32, (1, LANES), 1)
    tq = ATT_TQ
    cols = ATT_GROUP * tq
    lt = v_ref.shape[1]
    n = n_x_chunks
    kt = 4 * LANES

    @pl.when(qt == 0)
    def _():
        def fill(i, carry):
            rs = pl.ds(pl.multiple_of(i * CTX_LEN, CTX_LEN), CTX_LEN)
            vt = v_ref[0, rs, :].astype(F32).T
            for h in range(ATT_KV_HEADS):
                vt_scr[h, 0:HEAD_DIM, rs] = vt[HEAD_DIM * h:HEAD_DIM * (h + 1)].astype(BF16)
                vt_scr[h, HEAD_DIM:ATT_VROWS, rs] = jnp.ones((ATT_VROWS - HEAD_DIM, CTX_LEN), BF16)
            return carry
        lax.fori_loop(0, lt // CTX_LEN, fill, 0)

    def key_range(c):
        return (CTX_LEN + c * ATT_TK, ATT_TK) if c < n else (0, CTX_LEN)

    for h in range(ATT_KV_HEADS):
        in_head = (lane // HEAD_DIM) == h
        qs = jnp.concatenate(
            [jnp.where(in_head, q_ref[0, :, LANES * g:LANES * (g + 1)], jnp.zeros((), BF16))
             for g in range(ATT_GROUP)], axis=0)
        qst = qs.astype(F32).T.astype(BF16)

        def scores(rng, slot, qst=qst):
            start, width = rng
            s_scr[slot, 0:width, :] = jnp.dot(k_ref[0, start:start + width, :], qst,
                                              preferred_element_type=F32)
            return jnp.max(s_scr[slot, 0:width, :], axis=0, keepdims=True)

        def consume(rng, slot, m, mloc, acc, h=h):
            start, width = rng
            m_new = jnp.maximum(m, mloc)
            acc = acc * jnp.exp2(m - m_new)
            kw = min(kt, width)
            for t in range(width // kw):
                p = jnp.exp2(s_scr[slot, kw * t:kw * (t + 1), :] - m_new).astype(BF16)
                acc = acc + jnp.dot(vt_scr[h, :, start + kw * t:start + kw * (t + 1)], p,
                                    preferred_element_type=F32)
            return m_new, acc

        zero = jnp.zeros((ATT_VROWS, cols), F32)

        @pl.when(qt < n_ctx_tiles)
        def _(h=h, scores=scores, consume=consume):
            mloc = scores((0, CTX_LEN), 0)
            acc_scr[h] = consume((0, CTX_LEN), 0, mloc, mloc, zero)[1]

        @pl.when(qt >= n_ctx_tiles)
        def _(h=h, scores=scores, consume=consume):
            mloc = scores(key_range(0), 0)
            m, acc = mloc, zero
            for i in range(n + 1):
                mloc_next = scores(key_range(i + 1), (i + 1) % 2) if i + 1 <= n else None
                m, acc = consume(key_range(i), i % 2, m, mloc, acc)
                mloc = mloc_next
            acc_scr[h] = acc

    first = lane < HEAD_DIM
    pad = jnp.zeros((LANES - ATT_VROWS, tq), F32)
    for g in range(ATT_GROUP):
        cs = slice(tq * g, tq * (g + 1))
        outs = []
        for h in range(ATT_KV_HEADS):
            acc = jnp.concatenate([acc_scr[h, :, cs], pad], axis=0).T
            outs.append(acc / acc[:, HEAD_DIM:HEAD_DIM + 1])
        o_ref[0, :, LANES * g:LANES * (g + 1)] = jnp.where(
            first, outs[0], pltpu.roll(outs[1], HEAD_DIM, axis=1))


def _attention(q, k, v):
    nb, lt, _ = q.shape
    nq = lt // ATT_TQ
    cols = ATT_GROUP * ATT_TQ
    kern = functools.partial(_attn_kernel, n_ctx_tiles=CTX_LEN // ATT_TQ,
                             n_x_chunks=(lt - CTX_LEN) // ATT_TK)
    return pl.pallas_call(
        kern,
        grid=(nb, nq),
        in_specs=[
            pl.BlockSpec((1, ATT_TQ, ATT_WIDTH), lambda b, i: (b, i, 0)),
            pl.BlockSpec((1, lt, KV_WIDTH), lambda b, i: (b, 0, 0), pipeline_mode=pl.Buffered(1)),
            pl.BlockSpec((1, lt, KV_WIDTH), lambda b, i: (b, 0, 0), pipeline_mode=pl.Buffered(1)),
        ],
        out_specs=pl.BlockSpec((1, ATT_TQ, ATT_WIDTH), lambda b, i: (b, i, 0)),
        out_shape=jax.ShapeDtypeStruct((nb, lt, ATT_WIDTH), F32),
        scratch_shapes=[
            pltpu.VMEM((ATT_KV_HEADS, ATT_VROWS, lt), BF16),
            pltpu.VMEM((2, ATT_TK, cols), F32),
            pltpu.VMEM((ATT_KV_HEADS, ATT_VROWS, cols), F32),
        ],
        compiler_params=_cparams(("arbitrary", "arbitrary")),
        name="attention",
    )(q, k, v)


def _scan_tile(j, reverse, n_ctx_tiles, n_tiles):
    if not reverse:
        return j
    return jnp.where(j < n_ctx_tiles, n_ctx_tiles - 1 - j, n_tiles + n_ctx_tiles - 1 - j)


def _rnn_kernel(*refs, reverse, n_ctx_tiles, n_tiles, lt):
    if reverse:
        (u_ref, prev_ref, next_ref, cw_ref, cb_ref, wg_ref, bg_ref, clam_ref, yin_ref,
         y_ref, a_scr, b_scr, h_scr) = refs
    else:
        (u_ref, prev_ref, next_ref, cw_ref, cb_ref, wg_ref, bg_ref, clam_ref,
         y_ref, a_scr, b_scr, h_scr) = refs
        yin_ref = None
    j = pl.program_id(0)
    tile = _scan_tile(j, reverse, n_ctx_tiles, n_tiles)
    rows = RNN_TT * SUBLANES
    rw = RNN_WIDTH

    @pl.when(j == 0)
    def _():
        h_scr[...] = jnp.zeros_like(h_scr)

    t0 = tile * RNN_TT
    t1 = t0 + RNN_TT
    has_prev = jnp.logical_and(t0 != 0, t0 != CTX_LEN)
    has_next = jnp.logical_and(t1 != CTX_LEN, t1 != lt)
    u = u_ref[...]
    prev = jnp.where(has_prev, prev_ref[...], 0.0)
    nxt = jnp.where(has_next, next_ref[0:SUBLANES, :], 0.0)
    ext = jnp.concatenate([prev, u, nxt], axis=0)
    cw = cw_ref[...]
    s = SUBLANES
    cv = (cw[0:1] * ext[0:rows] + cw[1:2] * ext[s:rows + s] + cw[2:3] * ext[2 * s:rows + 2 * s]
          + cw[3:4] * ext[3 * s:rows + 3 * s] + cb_ref[...])
    g = jnp.dot(cv.astype(BF16), wg_ref[...], preferred_element_type=F32) + bg_ref[...]
    r = _sigmoid(g[:, 0:rw])
    i = _sigmoid(g[:, rw:2 * rw])
    log_a = clam_ref[...] * r
    a_scr[...] = jnp.exp(log_a)
    th = jnp.tanh(log_a)
    b_scr[...] = jnp.sqrt(-2.0 * th / (1.0 - th)) * (i * cv)

    def step(k, h):
        t = (RNN_TT - 1 - k) if reverse else k
        rs = pl.ds(pl.multiple_of(t * SUBLANES, SUBLANES), SUBLANES)
        h = a_scr[rs, :] * h + b_scr[rs, :]
        if reverse:
            y_ref[rs, :] = h + yin_ref[rs, :]
        else:
            y_ref[rs, :] = h
        return h

    h_scr[...] = lax.fori_loop(0, RNN_TT, step, h_scr[...], unroll=8)


def _rnn_dir(u, conv_w, conv_b, wg, bg, clam, yin, reverse):
    nrows = u.shape[0]
    rows = RNN_TT * SUBLANES
    n_tiles = nrows // rows
    n_ctx = CTX_LEN // RNN_TT
    lt = nrows // SUBLANES
    halo = 2 * SUBLANES
    per = rows // halo
    nhalo = nrows // halo
    tile_of = lambda j: _scan_tile(j, reverse, n_ctx, n_tiles)
    rw = RNN_WIDTH
    const = lambda j: (0, 0)
    in_specs = [
        pl.BlockSpec((rows, rw), lambda j: (tile_of(j), 0)),
        pl.BlockSpec((halo, rw), lambda j: (jnp.maximum(tile_of(j) * per - 1, 0), 0)),
        pl.BlockSpec((halo, rw), lambda j: (jnp.minimum((tile_of(j) + 1) * per, nhalo - 1), 0)),
        pl.BlockSpec((4, rw), const),
        pl.BlockSpec((1, rw), const),
        pl.BlockSpec((rw, 2 * rw), const),
        pl.BlockSpec((1, 2 * rw), const),
        pl.BlockSpec((1, rw), const),
    ]
    args = [u, u, u, conv_w, conv_b, wg, bg, clam]
    if reverse:
        in_specs.append(pl.BlockSpec((rows, rw), lambda j: (tile_of(j), 0)))
        args.append(yin)
    kern = functools.partial(_rnn_kernel, reverse=reverse, n_ctx_tiles=n_ctx, n_tiles=n_tiles, lt=lt)
    return pl.pallas_call(
        kern,
        grid=(n_tiles,),
        in_specs=in_specs,
        out_specs=pl.BlockSpec((rows, rw), lambda j: (tile_of(j), 0)),
        out_shape=jax.ShapeDtypeStruct((nrows, rw), F32),
        scratch_shapes=[pltpu.VMEM((rows, rw), F32), pltpu.VMEM((rows, rw), F32),
                        pltpu.VMEM((SUBLANES, rw), F32)],
        compiler_params=_cparams(("arbitrary",)),
        name="rglru_bwd" if reverse else "rglru_fwd",
    )(*args)


def _out_even_in_odd_kernel(att_ref, ga_ref, y_ref, gr_ref, x_ref, mod_ref, wa_ref, wr_ref,
                            nmod_ref, wn_ref, o_ref, u_ref, g_ref, a_scr):
    nblk = ATT_WIDTH // LANES
    for b in range(SUBLANES):
        for j in range(nblk):
            a_scr[j, _batch_rows(b), :] = att_ref[b, :, LANES * j:LANES * (j + 1)]
    att = jnp.concatenate([a_scr[j] for j in range(nblk)], axis=1)
    m1 = (att * _silu(ga_ref[...].astype(F32))).astype(BF16)
    m2 = (y_ref[...] * _silu(gr_ref[...].astype(F32))).astype(BF16)
    o = (jnp.dot(m1, wa_ref[...], preferred_element_type=F32)
         + jnp.dot(m2, wr_ref[...], preferred_element_type=F32))
    xn = x_ref[...] + _per_batch(o, mod_ref[0, 2])
    o_ref[...] = xn
    h = _norm_mod(xn, nmod_ref).astype(BF16)
    z = jnp.dot(h, wn_ref[...], preferred_element_type=F32)
    u_ref[...] = z[:, 0:S5_WIDTH]
    g_ref[...] = z[:, S5_WIDTH:2 * S5_WIDTH].astype(BF16)


def _out_even_in_odd(att, ga, y, gr, xc, mods, wa, wr, next_mods, wn):
    nrows = xc.shape[0]
    nt = nrows // ROWS
    d = D_MODEL
    row = lambda i: (i, 0)
    const = lambda i: (0, 0)
    return pl.pallas_call(
        _out_even_in_odd_kernel,
        grid=(nt,),
        in_specs=[
            pl.BlockSpec((SUBLANES, ROW_TT, ATT_WIDTH), lambda i: (0, i, 0)),
            pl.BlockSpec((ROWS, ATT_WIDTH), row),
            pl.BlockSpec((ROWS, RNN_WIDTH), row),
            pl.BlockSpec((ROWS, RNN_WIDTH), row),
            pl.BlockSpec((ROWS, d), row),
            _mod_spec(CTX_LEN // ROW_TT),
            pl.BlockSpec(wa.shape, const),
            pl.BlockSpec(wr.shape, const),
            _mod_spec(CTX_LEN // ROW_TT),
            pl.BlockSpec(wn.shape, const),
        ],
        out_specs=[pl.BlockSpec((ROWS, d), row), pl.BlockSpec((ROWS, S5_WIDTH), row),
                   pl.BlockSpec((ROWS, S5_WIDTH), row)],
        out_shape=[jax.ShapeDtypeStruct(xc.shape, F32),
                   jax.ShapeDtypeStruct((nrows, S5_WIDTH), F32),
                   jax.ShapeDtypeStruct((nrows, S5_WIDTH), BF16)],
        scratch_shapes=[pltpu.VMEM((ATT_WIDTH // LANES, ROWS, LANES), F32)],
        compiler_params=_cparams(("parallel",)),
        name="even_out_odd_in_proj",
    )(att, ga, y, gr, xc, mods, wa, wr, next_mods, wn)


def _s5_kernel(u_ref, msc_ref, mic_ref, moc_ref, etj_ref, eap_ref, a_ref, extra_ref, y_ref,
               ms_scr, mio_scr, hp_scr, h_scr, *, reverse):
    j = pl.program_id(1)
    nc = S5_TT // S5_TC
    rows = nc * SUBLANES
    sw = S5_SW

    @pl.when(j == 0)
    def _():
        h_scr[...] = jnp.zeros_like(h_scr)
        cw = S5_TC * LANES
        r = lax.broadcasted_iota(jnp.int32, (cw, 1), 0)
        c = lax.broadcasted_iota(jnp.int32, (1, cw), 1)
        row_group_ch = (r // S5_GROUP) % S5_GB
        row_group_st = (r // S5_STATE) % S5_GB
        col_group_ch = (c // S5_GROUP) % S5_GB
        col_group_st = (c // S5_STATE) % S5_GB

        def expand(compact_ref, e_ref, keep):
            full = jnp.dot(compact_ref[0], e_ref[...], preferred_element_type=F32)
            return jnp.where(keep, full, 0.0).astype(BF16)

        ms_scr[...] = expand(msc_ref, eap_ref, row_group_ch == col_group_st)
        mio_scr[0:cw, :] = expand(mic_ref, etj_ref, row_group_ch == col_group_ch)
        mio_scr[cw:2 * cw, :] = expand(moc_ref, etj_ref, row_group_st == col_group_ch)

    def chunk_major(ref):
        return jnp.concatenate(
            [ref[:, SUBLANES * t:SUBLANES * (t + 1), :].reshape(rows, LANES) for t in range(S5_TC)], axis=1)

    lhs32 = chunk_major(u_ref)
    lhs = lhs32.astype(BF16)
    local = jnp.dot(lhs, ms_scr[...], preferred_element_type=F32)
    ar = a_ref[0, :, 0:sw]
    ai = a_ref[0, :, sw:2 * sw]
    hr = h_scr[:, 0:sw]
    hi = h_scr[:, sw:2 * sw]
    for c in (range(nc - 1, -1, -1) if reverse else range(nc)):
        rs = slice(SUBLANES * c, SUBLANES * (c + 1))
        hp_scr[rs, 0:sw] = hr
        hp_scr[rs, sw:2 * sw] = hi
        hr, hi = (ar * hr - ai * hi + local[rs, 0:sw], ar * hi + ai * hr + local[rs, sw:2 * sw])
    h_scr[:, 0:sw] = hr
    h_scr[:, sw:2 * sw] = hi
    cw = S5_TC * LANES
    tile = 2 * LANES
    hp = hp_scr[...].astype(BF16)
    parts = []
    for b in range(cw // tile):
        cs = slice(tile * b, tile * (b + 1))
        ks = slice(tile * b, cw) if reverse else slice(0, tile * (b + 1))
        parts.append(jnp.dot(lhs[:, ks], mio_scr[ks, cs], preferred_element_type=F32)
                     + jnp.dot(hp, mio_scr[cw:2 * cw, cs], preferred_element_type=F32))
    y = jnp.concatenate(parts, axis=1)
    if reverse:
        y = y + chunk_major(extra_ref)
    else:
        y = y + extra_ref[...] * lhs32
    for t in range(S5_TC):
        y_ref[:, SUBLANES * t:SUBLANES * (t + 1), :] = (
            y[:, LANES * t:LANES * (t + 1)].reshape(nc, SUBLANES, LANES))


def _s5_dir(u3, msc, mic, moc, etj, eap, a, extra, reverse):
    nchunks, crow, width = u3.shape
    nc = S5_TT // S5_TC
    n_tiles = nchunks // nc
    n_ctx = CTX_LEN // S5_TT
    tile_of = lambda j: _scan_tile(j, reverse, n_ctx, n_tiles)
    blk = lambda k, j: (tile_of(j), 0, k)
    wblk = lambda k, j: (k, 0, 0)
    cw = S5_TC * LANES
    if reverse:
        extra_spec = pl.BlockSpec((nc, crow, LANES), blk)
    else:
        extra_spec = pl.BlockSpec((1, cw), lambda k, j: (0, k))
    return pl.pallas_call(
        functools.partial(_s5_kernel, reverse=reverse),
        grid=(S5_NB, n_tiles),
        in_specs=[
            pl.BlockSpec((nc, crow, LANES), blk),
            pl.BlockSpec((1, cw, LANES), wblk),
            pl.BlockSpec((1, cw, LANES), wblk),
            pl.BlockSpec((1, cw, LANES), wblk),
            pl.BlockSpec(etj.shape, lambda k, j: (0, 0)),
            pl.BlockSpec(eap.shape, lambda k, j: (0, 0)),
            pl.BlockSpec((1, SUBLANES, 2 * S5_SW), wblk),
            extra_spec,
        ],
        out_specs=pl.BlockSpec((nc, crow, LANES), blk),
        out_shape=jax.ShapeDtypeStruct(u3.shape, F32),
        scratch_shapes=[pltpu.VMEM((cw, 2 * S5_SW), BF16), pltpu.VMEM((cw + 2 * S5_SW, cw), BF16),
                        pltpu.VMEM((nc * SUBLANES, 2 * S5_SW), F32), pltpu.VMEM((SUBLANES, 2 * S5_SW), F32)],
        compiler_params=_cparams(("arbitrary", "arbitrary")),
        name="s5_bwd" if reverse else "s5_fwd",
    )(u3, msc, mic, moc, etj, eap, a, extra)


def _out_odd_kernel(y_ref, g_ref, x_ref, mod_ref, gw_ref, gb_ref, wo_ref, fw_ref, o_ref, *scr, final):
    yy = jax.nn.gelu(y_ref[...]).astype(BF16)
    z = jnp.dot(yy, gw_ref[...], preferred_element_type=F32) + gb_ref[...]
    g = g_ref[...].astype(F32)
    m = z[:, 0:S5_WIDTH] * _sigmoid(z[:, S5_WIDTH:2 * S5_WIDTH]) * _silu(g)
    o = jnp.dot(m.astype(BF16), wo_ref[...], preferred_element_type=F32)
    xn = x_ref[...] + _per_batch(o, mod_ref[0, 2])
    if final:
        (o_scr,) = scr
        ms = jnp.mean(xn * xn, axis=-1, keepdims=True)
        xo = xn * lax.rsqrt(ms + EPS) * fw_ref[...]
        nblk = D_MODEL // LANES
        for j in range(nblk):
            o_scr[j] = xo[:, LANES * j:LANES * (j + 1)]
        for b in range(SUBLANES):
            for j in range(nblk):
                o_ref[b, :, LANES * j:LANES * (j + 1)] = o_scr[j, _batch_rows(b), :]
    else:
        o_ref[...] = xn


def _out_odd(y, g, xc, mods, gw, gb, wo, fw, final):
    nrows = xc.shape[0]
    d = D_MODEL
    nct = CTX_LEN // ROW_TT
    lt = nrows // SUBLANES
    if final:
        nt = (lt - CTX_LEN) // ROW_TT
        row = lambda i: (i + nct, 0)
        mod_spec = _mod_spec(nct, offset=nct)
        out_spec = pl.BlockSpec((SUBLANES, ROW_TT, d), lambda i: (0, i, 0))
        out_shape = jax.ShapeDtypeStruct((SUBLANES, lt - CTX_LEN, d), F32)
        scratch = [pltpu.VMEM((d // LANES, ROWS, LANES), F32)]
    else:
        nt = lt // ROW_TT
        row = lambda i: (i, 0)
        mod_spec = _mod_spec(nct)
        out_spec = pl.BlockSpec((ROWS, d), row)
        out_shape = jax.ShapeDtypeStruct(xc.shape, F32)
        scratch = []
    const = lambda i: (0, 0)
    return pl.pallas_call(
        functools.partial(_out_odd_kernel, final=final),
        grid=(nt,),
        in_specs=[
            pl.BlockSpec((ROWS, S5_WIDTH), row),
            pl.BlockSpec((ROWS, S5_WIDTH), row),
            pl.BlockSpec((ROWS, d), row),
            mod_spec,
            pl.BlockSpec(gw.shape, const),
            pl.BlockSpec(gb.shape, const),
            pl.BlockSpec(wo.shape, const),
            pl.BlockSpec(fw.shape, const),
        ],
        out_specs=out_spec,
        out_shape=out_shape,
        scratch_shapes=scratch,
        compiler_params=_cparams(("parallel",)),
        name="odd_out_proj_final" if final else "odd_out_proj",
    )(y, g, xc, mods, gw, gb, wo, fw)


def _rope_tables(lt):
    n = lt - CTX_LEN
    rows = n // GRID_W
    row = jnp.repeat(jnp.arange(rows, dtype=F32), GRID_W)
    col = jnp.tile(jnp.arange(GRID_W, dtype=F32), rows)
    n_freq = HEAD_DIM // 4
    inv = ROPE_THETA ** (-jnp.arange(n_freq, dtype=F32) / n_freq)
    ang = jnp.concatenate([row[:, None] * inv, col[:, None] * inv], axis=-1)
    half = HEAD_DIM // 2
    cos = jnp.concatenate([jnp.ones((CTX_LEN, half), F32), jnp.cos(ang)], axis=0)
    sin = jnp.concatenate([jnp.zeros((CTX_LEN, half), F32), jnp.sin(ang)], axis=0)
    reps = LANES // HEAD_DIM
    ct = jnp.tile(jnp.concatenate([cos, cos], axis=-1), (1, reps))
    st = jnp.tile(jnp.concatenate([-sin, sin], axis=-1), (1, reps))
    return ct, st


def _rope_gains(gain, scale):
    g_eo = jnp.concatenate([gain[0::2], gain[1::2]])
    g_oe = jnp.concatenate([gain[1::2], gain[0::2]])
    reps = LANES // HEAD_DIM
    return jnp.stack([jnp.tile(g_eo, reps), jnp.tile(g_oe, reps)]).astype(F32) * scale


def _even_weights(w_in, w_out):
    aw, kw, rw = ATT_WIDTH, KV_WIDTH, RNN_WIDTH
    eo = jnp.concatenate([jnp.arange(0, HEAD_DIM, 2), jnp.arange(1, HEAD_DIM, 2)])
    slot_head = jnp.array([h * ATT_GROUP + g for g in range(ATT_GROUP) for h in range(ATT_KV_HEADS)])
    q_cols = (slot_head[:, None] * HEAD_DIM + eo[None, :]).reshape(-1)
    kv_heads = jnp.arange(ATT_KV_HEADS)
    k_cols = aw + (kv_heads[:, None] * HEAD_DIM + eo[None, :]).reshape(-1)
    v_cols = aw + kw + jnp.arange(kw)
    slot_cols = (slot_head[:, None] * HEAD_DIM + jnp.arange(HEAD_DIM)[None, :]).reshape(-1)
    ga_cols = aw + 2 * kw + slot_cols
    u_cols = 2 * aw + 2 * kw + jnp.arange(rw)
    gr_cols = 2 * aw + 2 * kw + rw + jnp.arange(rw)
    cols = jnp.concatenate([q_cols, k_cols, v_cols, ga_cols, u_cols, gr_cols])
    w = w_in[:, cols].astype(BF16)
    wa = w_out[:aw][slot_cols].astype(BF16)
    wr = w_out[aw:].astype(BF16)
    return w, wa, wr


def _block_ones(width):
    idx = jnp.arange(width) // HEAD_DIM
    return (idx[:, None] == idx[None, :]).astype(BF16)


def _rnn_gate_weights(wa, ba, wx, bx):
    eye = jnp.eye(RNN_BLOCKS, dtype=F32)

    def dense(w):
        return jnp.einsum('hij,hk->hikj', w, eye).reshape(RNN_WIDTH, RNN_WIDTH)

    wg = jnp.concatenate([dense(wa), dense(wx)], axis=1).astype(BF16)
    bg = jnp.concatenate([ba.reshape(-1), bx.reshape(-1)])[None, :]
    return wg, bg


def _s5_discretize(lam_re, lam_im, log_step, b_re, b_im):
    dt = jnp.exp(log_step)[:, None]
    mag = jnp.exp(lam_re * dt)
    ab_re = mag * jnp.cos(lam_im * dt)
    ab_im = mag * jnp.sin(lam_im * dt)
    den = lam_re * lam_re + lam_im * lam_im
    nr, ni = ab_re - 1.0, ab_im
    f_re = (nr * lam_re + ni * lam_im) / den
    f_im = (ni * lam_re - nr * lam_im) / den
    bb_re = f_re[..., None] * b_re - f_im[..., None] * b_im
    bb_im = f_re[..., None] * b_im + f_im[..., None] * b_re
    return ab_re, ab_im, bb_re, bb_im


def _s5_weights(lam_re, lam_im, log_step, b_re, b_im, c_re, c_im, reverse):
    hp = lax.Precision.HIGHEST
    f = lambda t: t.astype(F32)
    lam_re, lam_im, log_step, b_re, b_im, c_re, c_im = map(f, (lam_re, lam_im, log_step, b_re, b_im, c_re, c_im))
    tc, nbk, gb, p, c = S5_TC, S5_NB, S5_GB, S5_STATE, S5_GROUP
    _, _, bb_re, bb_im = _s5_discretize(lam_re, lam_im, log_step, b_re, b_im)
    dt = jnp.exp(log_step)[:, None]
    e = jnp.arange(tc + 1, dtype=F32)[:, None, None]
    mag = jnp.exp(e * (lam_re * dt))
    pw_re = mag * jnp.cos(e * (lam_im * dt))
    pw_im = mag * jnp.sin(e * (lam_im * dt))
    c_re_t = c_re.transpose(0, 2, 1)
    c_im_t = c_im.transpose(0, 2, 1)
    ca_re = c_re_t[None] * pw_re[..., None] - c_im_t[None] * pw_im[..., None]
    ca_im = c_re_t[None] * pw_im[..., None] + c_im_t[None] * pw_re[..., None]
    kern = (jnp.einsum('tgpj,gpi->tgij', ca_re[:tc], bb_re, precision=hp)
            - jnp.einsum('tgpj,gpi->tgij', ca_im[:tc], bb_im, precision=hp))
    none = jnp.zeros_like(kern[0])

    def lagged(s, t):
        lag = (s - t) if reverse else (t - s)
        return kern[lag] if lag >= 0 else none

    mic = jnp.stack([jnp.concatenate([lagged(s, t) for t in range(tc)], axis=-1) for s in range(tc)], 0)
    bb_re_t = bb_re.transpose(0, 2, 1)
    bb_im_t = bb_im.transpose(0, 2, 1)

    def leaving(s):
        e_s = s if reverse else tc - 1 - s
        ar, ai = pw_re[e_s][:, None, :], pw_im[e_s][:, None, :]
        return jnp.concatenate([ar * bb_re_t - ai * bb_im_t, ar * bb_im_t + ai * bb_re_t], axis=-1)

    msc = jnp.stack([leaving(s) for s in range(tc)], 0)
    e_t = [(tc - t) if reverse else (t + 1) for t in range(tc)]
    moc = jnp.stack([jnp.concatenate([ca_re[e] for e in e_t], axis=-1),
                     jnp.concatenate([-ca_im[e] for e in e_t], axis=-1)], 0)

    def blocked(m):
        lead, _, x, w = m.shape
        return m.reshape(lead, nbk, gb, x, w).transpose(1, 0, 2, 3, 4).reshape(nbk, lead * gb * x, w)

    mic, msc, moc = blocked(mic), blocked(msc), blocked(moc)
    a = jnp.stack([pw_re[tc], pw_im[tc]], 0).reshape(2, nbk, gb, p).transpose(1, 0, 2, 3).reshape(nbk, 1, 2 * gb * p)
    a = jnp.broadcast_to(a, (nbk, SUBLANES, 2 * gb * p))
    return msc.astype(BF16), mic.astype(BF16), moc.astype(BF16), a


def _s5_expanders():
    cw = S5_TC * LANES
    col = jnp.arange(cw)
    src_tj = (col // LANES) * S5_GROUP + col % S5_GROUP
    src_ap = (col // S5_SW) * S5_STATE + col % S5_STATE
    etj = (jnp.arange(S5_TC * S5_GROUP)[:, None] == src_tj[None, :]).astype(BF16)
    eap = (jnp.arange(2 * S5_STATE)[:, None] == src_ap[None, :]).astype(BF16)
    return etj, eap


def kernel(x, c, ctx, c_ctx, ada_w, ada_b, ev_w_in, ev_w_out, q_norm_w, k_norm_w, rg_conv_w, rg_conv_b, rg_wa, rg_ba, rg_wx, rg_bx, rg_lambda, od_w_in, s5_lambda_re, s5_lambda_im, s5_log_step, s5_b_re, s5_b_im, s5_c_re, s5_c_im, s5_d, glu_w, glu_b, od_w_out, final_norm_w):
    nb, seq, d = x.shape
    assert nb == SUBLANES and d == D_MODEL and ctx.shape[1] == CTX_LEN
    assert seq % ATT_TK == 0 and DEPTH % 2 == 0
    lt = CTX_LEN + seq

    xc = None

    cond = jnp.zeros((16, d), F32).at[:nb].set(c).at[nb].set(c_ctx)
    mods = _modulation(cond, ada_w, ada_b).reshape(DEPTH, 16, 3, d)
    mods_x = mods[:, :nb].transpose(0, 2, 1, 3)
    mods_c = jnp.broadcast_to(mods[:, nb][:, :, None, :], mods_x.shape)
    mods = jnp.stack([mods_c, mods_x], axis=1)

    cos, sin = _rope_tables(lt)
    ones_q = _block_ones(ATT_WIDTH)
    ones_k = _block_ones(KV_WIDTH)
    etj, eap = _s5_expanders()
    fw = final_norm_w.reshape(1, d)
    out = None
    for layer in range(DEPTH):
        j = layer // 2
        ml = mods[layer]
        if layer % 2 == 0:
            w, wa, wr = _even_weights(ev_w_in[j], ev_w_out[j])
            gq = _rope_gains(q_norm_w[j], HEAD_DIM ** -0.5 * math.log2(math.e))
            gk = _rope_gains(k_norm_w[j], 1.0)
            if layer == 0:
                q, k, v, ga, u, gr, xc = _in_even((ctx, x), ml, w, cos, sin, gq, gk, ones_q, ones_k)
            else:
                q, k, v, ga, u, gr = _in_even(xc, ml, w, cos, sin, gq, gk, ones_q, ones_k)
            att = _attention(q, k, v)
            conv_b = rg_conv_b[j].reshape(1, RNN_WIDTH)
            y = None
            for direction, reverse in enumerate((False, True)):
                wg, bg = _rnn_gate_weights(rg_wa[j, direction], rg_ba[j, direction],
                                           rg_wx[j, direction], rg_bx[j, direction])
                clam = (-LRU_C * jax.nn.softplus(-rg_lambda[j, direction].astype(F32))).reshape(1, RNN_WIDTH)
                y = _rnn_dir(u, rg_conv_w[j], conv_b, wg, bg, clam, y, reverse)
            xc, u_odd, g_odd = _out_even_in_odd(att, ga, y, gr, xc, ml, wa, wr, mods[layer + 1],
                                                od_w_in[j].astype(BF16))
        else:
            u, g = u_odd, g_odd
            u3 = u.reshape(lt // S5_TC, S5_TC * nb, S5_WIDTH)
            y = jnp.tile(s5_d[j].astype(F32).reshape(S5_NB, 1, LANES), (1, S5_TC, 1)).reshape(1, -1)
            for direction, reverse in enumerate((False, True)):
                msc, mic, moc, a = _s5_weights(s5_lambda_re[j, direction], s5_lambda_im[j, direction],
                                               s5_log_step[j, direction], s5_b_re[j, direction],
                                               s5_b_im[j, direction], s5_c_re[j, direction],
                                               s5_c_im[j, direction], reverse)
                y = _s5_dir(u3, msc, mic, moc, etj, eap, a, y, reverse)
            y = y.reshape(lt * nb, S5_WIDTH)
            final = layer == DEPTH - 1
            res = _out_odd(y, g, xc, ml, glu_w[j].astype(BF16), glu_b[j].reshape(1, -1),
                           od_w_out[j].astype(BF16), fw, final)
            if final:
                out = res
            else:
                xc = res
    return out
```
